```python
import math
import jax
import jax.numpy as jnp
from jax import lax
import numpy as np

D_MODEL = 1024
BATCH = 32
SEQ = 256
DEPTH = 2
DEC_BATCH = 4
DEC_SEQ = 4096
PAST_LEN = 256

F32 = jnp.float32
GRID_W = 64
N_MOD = 9
D_FF = 2816
NORM_EPS = 1e-6
HEAD_NORM_EPS = 1e-5
RWKV_LN_EPS = 64e-5
ROPE_BASE = 10000.0

W_A = 512
H_A = 4
DK_A = W_A // H_A
DV_A = W_A // H_A
CHUNK_A = 64
W_B = 512
NB_B = 8
BS_B = W_B // NB_B
CONV_B = 4
CONV_PAD_LEFT = 2
LRU_C = 8.0
W_C = 512
GS_C = 16
G_C = W_C // GS_C
N_C = 64
W_D = 512
HD_D = 64
H_D = W_D // HD_D
R_W = 64
R_A = 64
R_G = 128
IN_EVEN = 4 * W_A + 2 * W_B
IN_ODD = W_C + 4 * W_D
MIX_OUT = W_A + W_B

kernel_name = 'bidir_hybrid_flow_step'


def flip(t):
    return jnp.flip(t, axis=1)


def rmsnorm(x, g):
    xf = x.astype(F32)
    y = xf * lax.rsqrt(jnp.mean(xf * xf, axis=-1, keepdims=True) + NORM_EPS)
    return (y * g.astype(F32)).astype(x.dtype)


def head_norm(o, eps):
    mu = jnp.mean(o, axis=-1, keepdims=True)
    oc = o - mu
    return oc * lax.rsqrt(jnp.mean(oc * oc, axis=-1, keepdims=True) + eps)


def swiglu(h, w13, w2):
    gate, up = jnp.split(h @ w13, 2, axis=-1)
    return (jax.nn.silu(gate) * up) @ w2


def centred_shift(z):
    zp = jnp.pad(z, ((0, 0), (1, 1), (0, 0)))
    return 0.5 * (zp[:, :-2] + zp[:, 2:])


def dwconv(x, w, b):
    L = x.shape[1]
    xp = jnp.pad(x, ((0, 0), (CONV_PAD_LEFT, CONV_B - 1 - CONV_PAD_LEFT), (0, 0)))
    out = b
    for j in range(CONV_B):
        out = out + xp[:, j:j + L] * w[j]
    return out


def rope_2d(x):
    Bn, L, H, dk = x.shape
    rows = L // GRID_W
    row = jnp.repeat(jnp.arange(rows, dtype=F32), GRID_W)
    col = jnp.tile(jnp.arange(GRID_W, dtype=F32), rows)
    nf = dk // 4
    freqs = ROPE_BASE ** (-jnp.arange(nf, dtype=F32) / nf)
    ang = jnp.stack([row, col], axis=-1)[:, :, None] * freqs
    cos = jnp.cos(ang)[None, :, None]
    sin = jnp.sin(ang)[None, :, None]
    xr = x.reshape(Bn, L, H, 2, 2, nf)
    x1, x2 = xr[..., 0, :], xr[..., 1, :]
    out = jnp.stack([x1 * cos - x2 * sin, x1 * sin + x2 * cos], axis=-2)
    return out.reshape(Bn, L, H, dk)


def linear_scan(a, b, h0):
    def comb(l, r):
        return (r[0] * l[0], r[0] * l[1] + r[1])
    a_cum, b_cum = lax.associative_scan(comb, (a, b), axis=1)
    h = b_cum + a_cum * h0[:, None]
    return h, h[:, -1]


def complex_scan(ar, ai, br, bi, h0r, h0i):
    def comb(l, r):
        lar, lai, lbr, lbi = l
        rar, rai, rbr, rbi = r
        return (rar * lar - rai * lai, rar * lai + rai * lar,
                rar * lbr - rai * lbi + rbr, rar * lbi + rai * lbr + rbi)
    car, cai, cbr, cbi = lax.associative_scan(comb, (ar, ai, br, bi), axis=1)
    hr = cbr + car * h0r[:, None] - cai * h0i[:, None]
    hi = cbi + car * h0i[:, None] + cai * h0r[:, None]
    return hr, hi


def retention_dir(q, k, v, lg, s0):
    Bn, L, H, dk = q.shape
    dv = v.shape[-1]
    n = L // CHUNK_A
    qc = q.reshape(Bn, n, CHUNK_A, H, dk).transpose(1, 0, 3, 2, 4)
    kc = k.reshape(Bn, n, CHUNK_A, H, dk).transpose(1, 0, 3, 2, 4)
    vc = v.reshape(Bn, n, CHUNK_A, H, dv).transpose(1, 0, 3, 2, 4)
    idx = jnp.arange(CHUNK_A, dtype=F32)
    rel = idx[:, None] - idx[None, :]
    decay_mask = jnp.where(rel >= 0, jnp.exp(lg[:, None, None] * jnp.maximum(rel, 0.0)), 0.0)
    q_decay = jnp.exp(lg[:, None] * (idx + 1.0))[..., None]
    k_decay = jnp.exp(lg[:, None] * (CHUNK_A - 1.0 - idx))[..., None]
    chunk_decay = jnp.exp(lg * CHUNK_A)[:, None, None]

    def step(S, inp):
        qi, ki, vi = inp
        scores = jnp.einsum('bhid,bhjd->bhij', qi, ki) * decay_mask
        inner = jnp.einsum('bhij,bhjv->bhiv', scores, vi)
        cross = jnp.einsum('bhid,bhdv->bhiv', qi * q_decay, S)
        S_new = S * chunk_decay + jnp.einsum('bhjd,bhjv->bhdv', ki * k_decay, vi)
        return S_new, inner + cross

    s_fin, out = lax.scan(step, s0, (qc, kc, vc))
    out = out.transpose(1, 0, 3, 2, 4).reshape(Bn, L, H, dv)
    return out, s_fin


def rglru_dir(x, lam, wa, ba, wx, bx, h0):
    Bn, L, W = x.shape
    xb = x.reshape(Bn, L, NB_B, BS_B)
    r = jax.nn.sigmoid(jnp.einsum('blnc,ncd->blnd', xb, wa).reshape(Bn, L, W) + ba)
    i = jax.nn.sigmoid(jnp.einsum('blnc,ncd->blnd', xb, wx).reshape(Bn, L, W) + bx)
    log_a = LRU_C * r * jax.nn.log_sigmoid(lam)
    b = jnp.sqrt(-jnp.expm1(2.0 * log_a)) * (i * x)
    return linear_scan(jnp.exp(log_a), b, h0)


def s5_dir(u, a_re, a_im, log_dt, b_re, b_im, c_re, c_im, h0r, h0i):
    dt = jnp.exp(log_dt)[:, None]
    mag = jnp.exp(a_re * dt)
    abr = mag * jnp.cos(a_im * dt)
    abi = mag * jnp.sin(a_im * dt)
    den = a_re * a_re + a_im * a_im
    fr = ((abr - 1.0) * a_re + abi * a_im) / den
    fi = (abi * a_re - (abr - 1.0) * a_im) / den
    bbr = fr[..., None] * b_re - fi[..., None] * b_im
    bbi = fr[..., None] * b_im + fi[..., None] * b_re
    bur = jnp.einsum('blgs,gns->blgn', u, bbr)
    bui = jnp.einsum('blgs,gns->blgn', u, bbi)
    hr, hi = complex_scan(jnp.broadcast_to(abr, bur.shape), jnp.broadcast_to(abi, bur.shape),
                          bur, bui, h0r, h0i)
    y = jnp.einsum('blgn,gsn->blgs', hr, c_re) - jnp.einsum('blgn,gsn->blgs', hi, c_im)
    return y, hr[:, -1], hi[:, -1]


def rwkv_scan(r, w, k, v, kk, a, s0):
    def step(S, inp):
        rt, wt, kt, vt, kkt, at = inp
        sk = jnp.einsum('bhvk,bhk->bhv', S, kkt)
        S = (S * wt[:, :, None, :] - sk[..., None] * (kkt * at)[:, :, None, :]
             + vt[..., None] * kt[:, :, None, :])
        return S, jnp.einsum('bhvk,bhk->bhv', S, rt)
    xs = tuple(jnp.moveaxis(t, 1, 0) for t in (r, w, k, v, kk, a))
    s_fin, ys = lax.scan(step, s0, xs)
    return jnp.moveaxis(ys, 0, 1), s_fin


def mix_even(h, s_ret, s_lru, is_latent, params):
    (w_in, w_out, ret_decay, conv_w, conv_b, lru_lam, lru_wa, lru_ba, lru_wx, lru_bx) = params
    Bn, L, _ = h.shape
    p = (h @ w_in).astype(F32)
    q, k, v, g, gb, xb = jnp.split(p, [W_A, 2 * W_A, 3 * W_A, 4 * W_A, 4 * W_A + W_B], axis=-1)
    q = q.reshape(Bn, L, H_A, DK_A)
    k = k.reshape(Bn, L, H_A, DK_A)
    v = v.reshape(Bn, L, H_A, DV_A)
    if is_latent:
        q = rope_2d(q)
        k = rope_2d(k)
    q = q * (DK_A ** -0.5)
    lg = jax.nn.log_sigmoid(ret_decay.astype(F32))
    s_ret = s_ret.astype(F32)
    o_f, sf = retention_dir(q, k, v, lg[0], s_ret[:, 0])
    o_b, sb = retention_dir(flip(q), flip(k), flip(v), lg[1], s_ret[:, 1])
    o = head_norm(o_f + flip(o_b), HEAD_NORM_EPS).reshape(Bn, L, W_A)
    y_a = o * jax.nn.silu(g)
    xc = dwconv(xb, conv_w.astype(F32), conv_b.astype(F32))
    s_lru = s_lru.astype(F32)
    hf, lf = rglru_dir(xc, lru_lam[0], lru_wa[0], lru_ba[0], lru_wx[0], lru_bx[0], s_lru[:, 0])
    hb, lb = rglru_dir(flip(xc), lru_lam[1], lru_wa[1], lru_ba[1], lru_wx[1], lru_bx[1], s_lru[:, 1])
    y_b = jax.nn.gelu(gb) * (hf + flip(hb))
    y = jnp.concatenate([y_a, y_b], axis=-1).astype(h.dtype) @ w_out
    return y, jnp.stack([sf, sb], axis=1), jnp.stack([lf, lb], axis=1)


def mix_odd(h, s_s5, s_rwkv, params):
    (w_in, w_out, a_re, a_im, log_dt, b_re, b_im, c_re, c_im, s5_d, glu_w, glu_b,
     rw_mu, rw_w0, rw_w1, rw_w2, rw_a0, rw_a1, rw_a2, rw_g1, rw_g2, rw_kk, rw_ka, rw_rk,
     ln_w, ln_b) = params
    Bn, L, _ = h.shape
    p = (h @ w_in).astype(F32)
    u, r, k, v, xd = jnp.split(p, [W_C, W_C + W_D, W_C + 2 * W_D, W_C + 3 * W_D], axis=-1)
    ub = u.reshape(Bn, L, G_C, GS_C)
    ss = s_s5.astype(F32)
    yf, hfr, hfi = s5_dir(ub, a_re[0], a_im[0], log_dt[0], b_re[0], b_im[0], c_re[0], c_im[0],
                          ss[:, 0, 0], ss[:, 0, 1])
    yb, hbr, hbi = s5_dir(flip(ub), a_re[1], a_im[1], log_dt[1], b_re[1], b_im[1], c_re[1], c_im[1],
                          ss[:, 1, 0], ss[:, 1, 1])
    ys = (yf + flip(yb)).reshape(Bn, L, W_C) + s5_d * u
    z = jax.nn.gelu(ys)
    y_c = z * jax.nn.sigmoid(z @ glu_w + glu_b)
    new_s5 = jnp.stack([jnp.stack([hfr, hfi], axis=1), jnp.stack([hbr, hbi], axis=1)], axis=1)
    mu = rw_mu.astype(F32)
    r = r + (centred_shift(r) - r) * mu[0]
    k = k + (centred_shift(k) - k) * mu[1]
    v = v + (centred_shift(v) - v) * mu[2]
    dxd = centred_shift(xd) - xd
    xw = xd + dxd * mu[3]
    xa = xd + dxd * mu[4]
    xg = xd + dxd * mu[5]
    g = jax.nn.sigmoid(xg @ rw_g1) @ rw_g2
    hs = (Bn, L, H_D, HD_D)
    kk = (k * rw_kk).reshape(hs)
    kk = kk / jnp.maximum(jnp.linalg.norm(kk, axis=-1, keepdims=True), 1e-12)
    rh = r.reshape(hs)
    vh = v.reshape(hs)
    rk = rw_rk.astype(F32).reshape(H_D, HD_D)
    sr = s_rwkv.astype(F32)
    y_sum = 0.0
    bonus = 0.0
    finals = []
    for d in range(2):
        wl = -jax.nn.softplus(-(rw_w0[d] + jnp.tanh(xw @ rw_w1[d]) @ rw_w2[d])) - 0.5
        decay = jnp.exp(-jnp.exp(wl)).reshape(hs)
        a = jax.nn.sigmoid(rw_a0[d] + (xa @ rw_a1[d]) @ rw_a2[d])
        kd = (k * (1.0 + (a - 1.0) * rw_ka)).reshape(hs)
        ah = a.reshape(hs)
        seqs = (rh, decay, kd, vh, kk, ah)
        if d == 1:
            seqs = tuple(flip(t) for t in seqs)
        y_d, s_fin = rwkv_scan(*seqs, sr[:, d])
        if d == 1:
            y_d = flip(y_d)
        y_sum = y_sum + y_d
        bonus = bonus + jnp.sum(rh * kd * rk, axis=-1, keepdims=True) * vh
        finals.append(s_fin)
    yd = head_norm(y_sum, RWKV_LN_EPS).reshape(Bn, L, W_D) * ln_w + ln_b
    y_dd = (yd + bonus.reshape(Bn, L, W_D)) * g
    y = jnp.concatenate([y_c, y_dd], axis=-1).astype(h.dtype) @ w_out
    return y, new_s5, jnp.stack(finals, axis=1)


def trunk(x, cond, init_states, is_latent, shared, even, odd):
    mod_w, mod_b, norm_g, f1a, f1b, f2a, f2b, final_norm = shared
    new_states = []
    for i in range(DEPTH):
        m = (jax.nn.silu(cond) @ mod_w[i] + mod_b[i]).reshape(cond.shape[0], N_MOD, 1, D_MODEL)
        h = rmsnorm(x, norm_g[i, 0]) * (1.0 + m[:, 1]) + m[:, 0]
        x = x + 0.5 * m[:, 2] * swiglu(h, f1a[i], f1b[i])
        h = rmsnorm(x, norm_g[i, 1]) * (1.0 + m[:, 4]) + m[:, 3]
        s_a, s_b = init_states[i]
        if i % 2 == 0:
            y, s_a, s_b = mix_even(h, s_a, s_b, is_latent, even)
        else:
            y, s_a, s_b = mix_odd(h, s_a, s_b, odd)
        new_states.append((s_a, s_b))
        x = x + m[:, 5] * y
        h = rmsnorm(x, norm_g[i, 2]) * (1.0 + m[:, 7]) + m[:, 6]
        x = x + 0.5 * m[:, 8] * swiglu(h, f2a[i], f2b[i])
    return rmsnorm(x, final_norm), new_states


def setup_inputs(seed: int = 0) -> dict:
    key = jax.random.key(seed)
    keys = iter(jax.random.split(key, 96))

    def nrm(shape, scale):
        return jax.random.normal(next(keys), shape, F32) * scale

    def uni(shape, lo, hi):
        return jax.random.uniform(next(keys), shape, F32, lo, hi)

    D = D_MODEL
    gam = 1.0 - 2.0 ** (-5.0 - jnp.arange(H_A, dtype=F32))
    ret_logit = jnp.log(gam / (1.0 - gam))
    lru_a = uni((2, W_B), 0.9, 0.999) ** (1.0 / LRU_C)
    lru_lam = jnp.log(lru_a / (1.0 - lru_a))
    a_im0 = math.pi * jnp.arange(N_C, dtype=F32)
    return {
        'x_prompt': nrm((BATCH, SEQ, D), 1.0),
        'x_sample': nrm((DEC_BATCH, DEC_SEQ, D), 1.0),
        'state_l0_ret': nrm((DEC_BATCH, 2, H_A, DK_A, DV_A), 2.0),
        'state_l0_lru': nrm((DEC_BATCH, 2, W_B), 0.5),
        'state_l1_s5': nrm((DEC_BATCH, 2, 2, G_C, N_C), 0.5),
        'state_l1_rwkv': nrm((DEC_BATCH, 2, H_D, HD_D, HD_D), 0.1),
        'c': nrm((DEC_BATCH, D), 1.0),
        'c_ctx': nrm((D,), 1.0),
        'mod_w': nrm((DEPTH, D, N_MOD * D), D ** -0.5),
        'mod_b': nrm((DEPTH, N_MOD * D), 0.02),
        'norm_g': 1.0 + nrm((DEPTH, 3, D), 0.02),
        'ffn1_w13': nrm((DEPTH, D, 2 * D_FF), D ** -0.5),
        'ffn1_w2': nrm((DEPTH, D_FF, D), D_FF ** -0.5),
        'ffn2_w13': nrm((DEPTH, D, 2 * D_FF), D ** -0.5),
        'ffn2_w2': nrm((DEPTH, D_FF, D), D_FF ** -0.5),
        'final_norm': 1.0 + nrm((D,), 0.02),
        'l0_w_in': nrm((D, IN_EVEN), D ** -0.5),
        'l0_w_out': nrm((MIX_OUT, D), MIX_OUT ** -0.5),
        'l0_ret_decay': ret_logit[None, :] + nrm((2, H_A), 0.05),
        'l0_conv_w': nrm((CONV_B, W_B), CONV_B ** -0.5),
        'l0_conv_b': nrm((W_B,), 0.02),
        'l0_lru_lam': lru_lam,
        'l0_lru_wa': nrm((2, NB_B, BS_B, BS_B), BS_B ** -0.5),
        'l0_lru_ba': nrm((2, W_B), 0.02),
        'l0_lru_wx': nrm((2, NB_B, BS_B, BS_B), BS_B ** -0.5),
        'l0_lru_bx': nrm((2, W_B), 0.02),
        'l1_w_in': nrm((D, IN_ODD), D ** -0.5),
        'l1_w_out': nrm((MIX_OUT, D), MIX_OUT ** -0.5),
        'l1_s5_a_re': -0.5 + nrm((2, G_C, N_C), 0.01),
        'l1_s5_a_im': a_im0 + nrm((2, G_C, N_C), 0.01),
        'l1_s5_log_dt': uni((2, G_C), math.log(0.001), math.log(0.1)),
        'l1_s5_b_re': nrm((2, G_C, N_C, GS_C), (2.0 * GS_C) ** -0.5),
        'l1_s5_b_im': nrm((2, G_C, N_C, GS_C), (2.0 * GS_C) ** -0.5),
        'l1_s5_c_re': nrm((2, G_C, GS_C, N_C), (2.0 * N_C) ** -0.5),
        'l1_s5_c_im': nrm((2, G_C, GS_C, N_C), (2.0 * N_C) ** -0.5),
        'l1_s5_d': nrm((W_C,), 0.5),
        'l1_glu_w': nrm((W_C, W_C), W_C ** -0.5),
        'l1_glu_b': nrm((W_C,), 0.02),
        'l1_rw_mu': uni((6, W_D), 0.1, 0.9),
        'l1_rw_w0': jnp.linspace(-6.5, -1.5, W_D, dtype=F32)[None, :] + nrm((2, W_D), 0.1),
        'l1_rw_w1': nrm((2, W_D, R_W), 0.1 * W_D ** -0.5),
        'l1_rw_w2': nrm((2, R_W, W_D), 0.1),
        'l1_rw_a0': nrm((2, W_D), 0.1),
        'l1_rw_a1': nrm((2, W_D, R_A), 0.1 * W_D ** -0.5),
        'l1_rw_a2': nrm((2, R_A, W_D), 0.1),
        'l1_rw_g1': nrm((W_D, R_G), W_D ** -0.5),
        'l1_rw_g2': nrm((R_G, W_D), R_G ** -0.5),
        'l1_rw_kk': 0.85 + nrm((W_D,), 0.02),
        'l1_rw_ka': 1.0 + nrm((W_D,), 0.02),
        'l1_rw_rk': nrm((W_D,), 0.1),
        'l1_ln_w': 1.0 + nrm((W_D,), 0.02),
        'l1_ln_b': nrm((W_D,), 0.02),
    }


def reference(x_prompt, x_sample, state_l0_ret, state_l0_lru, state_l1_s5, state_l1_rwkv, c, c_ctx,
              mod_w, mod_b, norm_g, ffn1_w13, ffn1_w2, ffn2_w13, ffn2_w2, final_norm,
              l0_w_in, l0_w_out, l0_ret_decay, l0_conv_w, l0_conv_b, l0_lru_lam, l0_lru_wa, l0_lru_ba,
              l0_lru_wx, l0_lru_bx,
              l1_w_in, l1_w_out, l1_s5_a_re, l1_s5_a_im, l1_s5_log_dt, l1_s5_b_re, l1_s5_b_im,
              l1_s5_c_re, l1_s5_c_im, l1_s5_d, l1_glu_w, l1_glu_b,
              l1_rw_mu, l1_rw_w0, l1_rw_w1, l1_rw_w2, l1_rw_a0, l1_rw_a1, l1_rw_a2, l1_rw_g1, l1_rw_g2,
              l1_rw_kk, l1_rw_ka, l1_rw_rk, l1_ln_w, l1_ln_b):
    shared = (mod_w, mod_b, norm_g, ffn1_w13, ffn1_w2, ffn2_w13, ffn2_w2, final_norm)
    even = (l0_w_in, l0_w_out, l0_ret_decay, l0_conv_w, l0_conv_b, l0_lru_lam, l0_lru_wa, l0_lru_ba,
            l0_lru_wx, l0_lru_bx)
    odd = (l1_w_in, l1_w_out, l1_s5_a_re, l1_s5_a_im, l1_s5_log_dt, l1_s5_b_re, l1_s5_b_im,
           l1_s5_c_re, l1_s5_c_im, l1_s5_d, l1_glu_w, l1_glu_b,
           l1_rw_mu, l1_rw_w0, l1_rw_w1, l1_rw_w2, l1_rw_a0, l1_rw_a1, l1_rw_a2, l1_rw_g1, l1_rw_g2,
           l1_rw_kk, l1_rw_ka, l1_rw_rk, l1_ln_w, l1_ln_b)
    bp = x_prompt.shape[0]
    ctx_init = [(jnp.zeros((bp, 2, H_A, DK_A, DV_A), F32), jnp.zeros((bp, 2, W_B), F32)),
                (jnp.zeros((bp, 2, 2, G_C, N_C), F32), jnp.zeros((bp, 2, H_D, HD_D, HD_D), F32))]
    y_prompt, ctx_states = trunk(x_prompt, c_ctx[None, :], ctx_init, False, shared, even, odd)
    lat_init = [(state_l0_ret, state_l0_lru), (state_l1_s5, state_l1_rwkv)]
    y_sample, _ = trunk(x_sample, c, lat_init, True, shared, even, odd)
    (new_l0_ret, new_l0_lru), (new_l1_s5, new_l1_rwkv) = ctx_states
    return (y_prompt, y_sample, new_l0_ret, new_l0_lru, new_l1_s5, new_l1_rwkv)
```

```python
import functools
import math

import jax
import jax.numpy as jnp
from jax import lax
from jax.experimental import pallas as pl
from jax.experimental.pallas import tpu as pltpu

F32 = jnp.float32
BF16 = jnp.bfloat16

N_MOD = 9
NORM_EPS = 1e-6
HEAD_NORM_EPS = 1e-5
RWKV_LN_EPS = 64e-5
ROPE_BASE = 10000.0
GRID_W = 64
H_A = 4
DK_A = 128
W_A = 512
W_B = 512
NB_B = 8
LRU_C = 8.0
W_C = 512
GS_C = 16
G_C = 32
N_C = 64
W_D = 512
HD_D = 64
H_D = 8
S5_STATE = 2 * G_C * N_C

SUBLANES = 8
LANES = 128
VMEM_LIMIT = 56 * 1024 * 1024
ROW_TILE = 512


def _cparams(sem):
    return pltpu.CompilerParams(dimension_semantics=sem, vmem_limit_bytes=VMEM_LIMIT)


def _dot(a, b):
    return jnp.dot(a, b, preferred_element_type=F32)


def _dot_nt(a, b):
    return lax.dot_general(a, b, (((1,), (1,)), ((), ())), preferred_element_type=F32)


def _dot_tn(a, b):
    return lax.dot_general(a, b, (((0,), (0,)), ((), ())), preferred_element_type=F32)


def _bdot(a, b):
    return _dot(a.astype(BF16), b.astype(BF16))


def _sigmoid(x):
    return 1.0 / (1.0 + jnp.exp(-x))


def _silu(x):
    return x * _sigmoid(x)


def _softplus(x):
    return jnp.maximum(x, 0.0) + jnp.log1p(jnp.exp(-jnp.abs(x)))


def _log_sigmoid(x):
    return -_softplus(-x)


def _gelu_tanh(x):
    return 0.5 * x * (1.0 + jnp.tanh(math.sqrt(2.0 / math.pi) * (x + 0.044715 * (x * x * x))))


def _rms(x):
    return x * lax.rsqrt(jnp.mean(x * x, axis=-1, keepdims=True) + NORM_EPS)


def _rms_mod(x, g, scale, shift):
    return (_rms(x) * g) * (1.0 + scale) + shift


def _const_spec(shape):
    nd = len(shape)
    return pl.BlockSpec(shape, lambda *_: (0,) * nd, pipeline_mode=pl.Buffered(1))


def _mod_kernel(c_ref, w_ref, b_ref, o_ref):
    o_ref[0] = _bdot(_silu(c_ref[...]), w_ref[0]) + b_ref[0]


def _modulation(cond8, mod_w, mod_b):
    depth, d, nd = mod_w.shape
    return pl.pallas_call(
        _mod_kernel,
        out_shape=jax.ShapeDtypeStruct((depth, SUBLANES, nd), F32),
        grid=(depth, nd // d),
        in_specs=[pl.BlockSpec((SUBLANES, d), lambda l, j: (0, 0)),
                  pl.BlockSpec((1, d, d), lambda l, j: (l, 0, j)),
                  pl.BlockSpec((1, 1, d), lambda l, j: (l, 0, j))],
        out_specs=pl.BlockSpec((1, SUBLANES, d), lambda l, j: (l, 0, j)),
        compiler_params=_cparams(("arbitrary", "arbitrary")),
        name="modulation",
    )(cond8, mod_w, mod_b.reshape(depth, 1, nd)).reshape(depth, SUBLANES, N_MOD, d)


class _Tiles:
    def __init__(self, batch, seq, cond_base, per_batch_cond, row_tile=ROW_TILE):
        self.batch, self.seq = batch, seq
        self.tm = min(row_tile, batch * seq)
        if seq >= self.tm:
            self.nb, self.tl = 1, self.tm
        else:
            self.nb, self.tl = self.tm // seq, seq
        assert seq % self.tl == 0 and batch % self.nb == 0
        self.tiles_per_seq = seq // self.tl
        self.n_tiles = batch * seq // self.tm
        self.cond_base = cond_base
        self.tiles_per_cond = self.tiles_per_seq if per_batch_cond else self.n_tiles

    def cond_row(self, i):
        return self.cond_base + i // self.tiles_per_cond

    def mod_spec(self, d):
        return pl.BlockSpec((1, N_MOD, d), lambda i: (self.cond_row(i), 0, 0))

    def row_spec(self, width, col=0):
        return pl.BlockSpec((self.tm, width), lambda i: (i, col))

    def tm_spec(self, width):
        return pl.BlockSpec((self.tl, self.nb * width),
                            lambda i: (i % self.tiles_per_seq, i // self.tiles_per_seq))


def _from_time_major(blk, nb, width):
    if nb == 1:
        return blk
    return jnp.concatenate([blk[:, j * width:(j + 1) * width] for j in range(nb)], axis=0)


def _ffn_kernel(*refs, mix, first, final, nb, ff_chunks):
    it = iter(refs)
    x_ref, m_ref, g_ref, w13_ref, w2_ref = (next(it) for _ in range(5))
    x = x_ref[...]
    m = m_ref[0]
    if mix == "even":
        ya_ref, gb_ref, hf_ref, hb_ref, wo_ref = (next(it) for _ in range(5))
        ya = ya_ref[...]
        yb = _gelu_tanh(gb_ref[...]) * (hf_ref[...] + hb_ref[...])
    elif mix == "odd":
        yc_ref, yf_ref, yb_ref, bon_ref, gg_ref, avg_ref, ln_ref, wo_ref = (next(it) for _ in range(8))
        ya = _from_time_major(yc_ref[...], nb, W_C)
        ys = yf_ref[...] + yb_ref[...]
        avg = avg_ref[...]
        oc = ys - _bdot(ys, avg)
        yd = oc * lax.rsqrt(_bdot(oc * oc, avg) + RWKV_LN_EPS)
        yd = yd * ln_ref[0:1] + ln_ref[1:2]
        yb = (yd + bon_ref[...]) * gg_ref[...]
    if mix is not None:
        wa = ya.shape[1]
        y = _dot(ya.astype(BF16), wo_ref[0:wa, :]) + _dot(yb.astype(BF16), wo_ref[wa:, :])
        x = x + m[5:6] * y
    fin_ref = next(it) if final else None
    o_ref = next(it)
    r0 = 0 if first else 6
    gi = 0 if first else 2
    h = _rms_mod(x, g_ref[gi:gi + 1], m[r0 + 1:r0 + 2], m[r0:r0 + 1]).astype(BF16)
    dff = w2_ref.shape[0]
    fc = dff // ff_chunks
    acc = jnp.zeros(x.shape, F32)
    for j in range(ff_chunks):
        gate = _dot(h, w13_ref[:, j * fc:(j + 1) * fc])
        up = _dot(h, w13_ref[:, dff + j * fc:dff + (j + 1) * fc])
        acc = acc + _dot((_silu(gate) * up).astype(BF16), w2_ref[j * fc:(j + 1) * fc, :])
    x = x + 0.5 * m[r0 + 2:r0 + 3] * acc
    if final:
        x = _rms(x) * fin_ref[...]
    o_ref[...] = x


def _ffn(tiles, x, mods, norm_g, w13, w2, *, first, mix=None, mix_args=(), wo=None, final_g=None):
    n, d = x.shape
    dff = w2.shape[0]
    args = [x, mods, norm_g, w13, w2]
    specs = [tiles.row_spec(d), tiles.mod_spec(d), _const_spec(norm_g.shape),
             _const_spec(w13.shape), _const_spec(w2.shape)]
    if mix == "even":
        ya, gx, hf, hb = mix_args
        args += [ya, gx, hf, hb, wo]
        specs += [tiles.row_spec(W_A), tiles.row_spec(W_B, 0), tiles.row_spec(W_B), tiles.row_spec(W_B),
                  _const_spec(wo.shape)]
    elif mix == "odd":
        yc_tm, yf, yb, bon, gg, avg, ln = mix_args
        args += [yc_tm, yf, yb, bon, gg, avg, ln, wo]
        specs += [tiles.tm_spec(W_C)] + [tiles.row_spec(W_D)] * 4 + [
            _const_spec(avg.shape), _const_spec(ln.shape), _const_spec(wo.shape)]
    if final_g is not None:
        args.append(final_g)
        specs.append(_const_spec(final_g.shape))
    kern = functools.partial(_ffn_kernel, mix=mix, first=first, final=final_g is not None,
                             nb=tiles.nb, ff_chunks=2 if dff % (2 * LANES) == 0 else 1)
    return pl.pallas_call(
        kern, out_shape=jax.ShapeDtypeStruct((n, d), F32), grid=(tiles.n_tiles,),
        in_specs=specs, out_specs=tiles.row_spec(d),
        compiler_params=_cparams(("parallel",)), name="ffn",
    )(*args)


def _proj_even_kernel(x_ref, m_ref, g_ref, w_ref, qkvg_ref, gx_ref):
    m = m_ref[0]
    h = _rms_mod(x_ref[...], g_ref[1:2], m[4:5], m[3:4]).astype(BF16)
    p = _dot(h, w_ref[...])
    nq = qkvg_ref.shape[1]
    qkvg_ref[...] = p[:, :nq]
    gx_ref[...] = p[:, nq:]


def _proj_odd_kernel(x_ref, m_ref, g_ref, w_ref, zero_ref, u_ref, rkvx_ref, *, nb, tl):
    del zero_ref
    m = m_ref[0]
    h = _rms_mod(x_ref[...], g_ref[1:2], m[4:5], m[3:4]).astype(BF16)
    p = _dot(h, w_ref[...])
    for j in range(nb):
        u_ref[:, j * W_C:(j + 1) * W_C] = p[j * tl:(j + 1) * tl, :W_C]
    rkvx_ref[...] = p[:, W_C:]


def _proj_even(tiles, x, mods, norm_g, w_in):
    n, d = x.shape
    return pl.pallas_call(
        _proj_even_kernel,
        out_shape=(jax.ShapeDtypeStruct((n, 4 * W_A), F32), jax.ShapeDtypeStruct((n, 2 * W_B), F32)),
        grid=(tiles.n_tiles,),
        in_specs=[tiles.row_spec(d), tiles.mod_spec(d), _const_spec(norm_g.shape), _const_spec(w_in.shape)],
        out_specs=(tiles.row_spec(4 * W_A), tiles.row_spec(2 * W_B)),
        compiler_params=_cparams(("parallel",)), name="proj_even",
    )(x, mods, norm_g, w_in)


def _proj_odd(tiles, x, mods, norm_g, w_in, bp):
    n, d = x.shape
    zeros = jnp.zeros((tiles.seq, bp * W_C), F32)
    kern = functools.partial(_proj_odd_kernel, nb=tiles.nb, tl=tiles.tl)
    return pl.pallas_call(
        kern,
        out_shape=(jax.ShapeDtypeStruct((tiles.seq, bp * W_C), F32), jax.ShapeDtypeStruct((n, 4 * W_D), F32)),
        grid=(tiles.n_tiles,),
        in_specs=[tiles.row_spec(d), tiles.mod_spec(d), _const_spec(norm_g.shape), _const_spec(w_in.shape),
                  pl.BlockSpec(memory_space=pl.ANY)],
        out_specs=(tiles.tm_spec(W_C), tiles.row_spec(4 * W_D)),
        input_output_aliases={4: 0},
        compiler_params=_cparams(("parallel",)), name="proj_odd",
    )(x, mods, norm_g, w_in, zeros)


def _ret_kernel(*refs, seq, chunk, rope, has_init):
    it = iter(refs)
    q_ref, k_ref, v_ref, g_ref, dec_ref = (next(it) for _ in range(5))
    cos_ref, sin_ref = (next(it), next(it)) if rope else (None, None)
    s0_ref = next(it) if has_init else None
    y_ref, sf_ref, o_scr = next(it), next(it), next(it)
    n = seq // chunk
    dk = q_ref.shape[1]
    lgf = _log_sigmoid(dec_ref[0, 0])
    lgb = _log_sigmoid(dec_ref[1, 0])
    ri = lax.broadcasted_iota(jnp.int32, (chunk, dk), 0).astype(F32)
    qf_t = jnp.exp(lgf * (ri + 1.0))
    qb_t = jnp.exp(lgb * (chunk - ri))
    kf_t = jnp.exp(lgf * (chunk - 1.0 - ri))
    kb_t = jnp.exp(lgb * ri)
    reps = chunk // dk
    lgf_c = jnp.concatenate([lgf] * reps, axis=1) if reps > 1 else lgf[:, :chunk]
    lgb_c = jnp.concatenate([lgb] * reps, axis=1) if reps > 1 else lgb[:, :chunk]
    rel = (lax.broadcasted_iota(jnp.int32, (chunk, chunk), 0)
           - lax.broadcasted_iota(jnp.int32, (chunk, chunk), 1)).astype(F32)
    dmask = (jnp.where(rel >= 0, jnp.exp(lgf_c * jnp.maximum(rel, 0.0)), 0.0)
             + jnp.where(rel <= 0, jnp.exp(lgb_c * jnp.maximum(-rel, 0.0)), 0.0))
    gcf = jnp.exp(lgf * float(chunk))
    gcb = jnp.exp(lgb * float(chunk))
    lane = lax.broadcasted_iota(jnp.int32, (chunk, dk), 1)
    first_half = (lane & (dk // 2 - 1)) < dk // 4

    def load_qkv(c):
        rows = pl.ds(pl.multiple_of(c * chunk, chunk), chunk)
        q, k, v = q_ref[rows, :], k_ref[rows, :], v_ref[rows, :]
        if rope:
            cs, sn = cos_ref[rows, :], sin_ref[rows, :]

            def rot(x):
                partner = jnp.where(first_half, pltpu.roll(x, dk - dk // 4, axis=1),
                                    pltpu.roll(x, dk // 4, axis=1))
                return x * cs + partner * sn

            q, k = rot(q), rot(k)
        return rows, q * (dk ** -0.5), k, v

    def fwd(c, s):
        rows, q, k, v = load_qkv(c)
        scores = _dot_nt(q.astype(BF16), k.astype(BF16)) * dmask
        inner = _bdot(scores, v)
        o_scr[rows, :] = inner + _bdot(q * qf_t, s)
        return s * gcf + _dot_tn((k * kf_t).astype(BF16), v.astype(BF16))

    def bwd(j, s):
        rows, q, k, v = load_qkv(n - 1 - j)
        o = o_scr[rows, :] + _bdot(q * qb_t, s)
        oc = o - jnp.mean(o, axis=-1, keepdims=True)
        on = oc * lax.rsqrt(jnp.mean(oc * oc, axis=-1, keepdims=True) + HEAD_NORM_EPS)
        y_ref[rows, :] = on * _silu(g_ref[rows, :])
        return s * gcb + _dot_tn((k * kb_t).astype(BF16), v.astype(BF16))

    zero = jnp.zeros((dk, v_ref.shape[1]), F32)
    sf = lax.fori_loop(0, n, fwd, s0_ref[0, 0, 0] if has_init else zero)
    sb = lax.fori_loop(0, n, bwd, s0_ref[0, 1, 0] if has_init else zero)
    sf_ref[0, 0, 0] = sf
    sf_ref[0, 1, 0] = sb


def _retention(qkvg, dec, batch, seq, s0=None, rope_tabs=None):
    dk = DK_A
    chunk = min(seq, 256)
    args = [qkvg] * 4 + [dec]
    specs = [pl.BlockSpec((seq, dk), lambda b, h, j=j: (b, j * H_A + h)) for j in range(4)]
    specs.append(pl.BlockSpec((2, 1, 1, dk), lambda b, h: (0, h, 0, 0)))
    if rope_tabs is not None:
        args += list(rope_tabs)
        specs += [_const_spec((seq, dk))] * 2
    if s0 is not None:
        args.append(s0)
        specs.append(pl.BlockSpec((1, 2, 1, dk, dk), lambda b, h: (b, 0, h, 0, 0)))
    kern = functools.partial(_ret_kernel, seq=seq, chunk=chunk, rope=rope_tabs is not None,
                             has_init=s0 is not None)
    return pl.pallas_call(
        kern,
        out_shape=(jax.ShapeDtypeStruct((batch * seq, W_A), F32),
                   jax.ShapeDtypeStruct((batch, 2, H_A, dk, dk), F32)),
        grid=(batch, H_A),
        in_specs=specs,
        out_specs=(pl.BlockSpec((seq, dk), lambda b, h: (b, h)),
                   pl.BlockSpec((1, 2, 1, dk, dk), lambda b, h: (b, 0, h, 0, 0))),
        scratch_shapes=[pltpu.VMEM((seq, dk), F32)],
        compiler_params=_cparams(("parallel", "arbitrary")), name="retention",
    )(*args)


def _halo_specs(tiles, width, col, n_rows):
    tm = tiles.tm
    nblk = n_rows // SUBLANES
    per = tm // SUBLANES
    prev = pl.BlockSpec((SUBLANES, width), lambda i: (jnp.maximum(i * per - 1, 0), col))
    nxt = pl.BlockSpec((SUBLANES, width), lambda i: (jnp.minimum((i + 1) * per, nblk - 1), col))
    return prev, nxt


def _seq_shift(ext, s, pos, seq, tm):
    sh = pltpu.roll(ext, s % ext.shape[0], axis=0)[SUBLANES:SUBLANES + tm]
    ok = (pos >= s) if s > 0 else (pos < seq + s)
    return jnp.where(ok, sh, 0.0)


def _seq_pos(shape, tm, seq):
    return (pl.program_id(0) * tm + lax.broadcasted_iota(jnp.int32, shape, 0)) & (seq - 1)


def _lru_prep_kernel(x_ref, xp_ref, xn_ref, cw_ref, wg_ref, bg_ref, lam_ref,
                     af_ref, bf_ref, ab_ref, bb_ref, *, seq):
    x = x_ref[...]
    tm = x.shape[0]
    ext = jnp.concatenate([xp_ref[...], x, xn_ref[...]], axis=0)
    pos = _seq_pos(x.shape, tm, seq)
    xc = (cw_ref[4:5] + cw_ref[0:1] * _seq_shift(ext, 2, pos, seq, tm)
          + cw_ref[1:2] * _seq_shift(ext, 1, pos, seq, tm) + cw_ref[2:3] * x
          + cw_ref[3:4] * _seq_shift(ext, -1, pos, seq, tm))
    gates = _dot(xc.astype(BF16), wg_ref[...]) + bg_ref[...]
    w = x.shape[1]
    for d, (a_ref, b_ref) in enumerate(((af_ref, bf_ref), (ab_ref, bb_ref))):
        r = _sigmoid(gates[:, (2 * d) * w:(2 * d + 1) * w])
        i = _sigmoid(gates[:, (2 * d + 1) * w:(2 * d + 2) * w])
        log_a = LRU_C * r * _log_sigmoid(lam_ref[d:d + 1])
        a = jnp.exp(log_a)
        a_ref[...] = a
        b_ref[...] = jnp.sqrt(1.0 - a * a) * (i * xc)


def _lru_prep(tiles, gx, conv_wb, wg, bg, lam):
    n = gx.shape[0]
    prev, nxt = _halo_specs(tiles, W_B, 1, n)
    out = jax.ShapeDtypeStruct((n, W_B), F32)
    return pl.pallas_call(
        functools.partial(_lru_prep_kernel, seq=tiles.seq),
        out_shape=(out,) * 4, grid=(tiles.n_tiles,),
        in_specs=[tiles.row_spec(W_B, 1), prev, nxt, _const_spec(conv_wb.shape), _const_spec(wg.shape),
                  _const_spec(bg.shape), _const_spec(lam.shape)],
        out_specs=(tiles.row_spec(W_B),) * 4,
        compiler_params=_cparams(("parallel",)), name="lru_prep",
    )(gx, gx, gx, conv_wb, wg, bg, lam)


def _lru_scan_kernel(*refs, nbt, tc, has_init):
    it = iter(refs)
    af_ref, bf_ref, ab_ref, bb_ref = (next(it) for _ in range(4))
    h0_ref = next(it) if has_init else None
    hf_ref, hb_ref, fin_ref, st_ref = (next(it) for _ in range(4))
    c = pl.program_id(1)

    @pl.when(c == 0)
    def _():
        st_ref[...] = h0_ref[...] if has_init else jnp.zeros(st_ref.shape, F32)

    def body(t, carry):
        tb = tc - 1 - t
        new = []
        for j in range(nbt):
            hf, hb = carry[2 * j], carry[2 * j + 1]
            hf = af_ref[j, pl.ds(t, 1), :] * hf + bf_ref[j, pl.ds(t, 1), :]
            hb = ab_ref[j, pl.ds(tb, 1), :] * hb + bb_ref[j, pl.ds(tb, 1), :]
            hf_ref[j, pl.ds(t, 1), :] = hf
            hb_ref[j, pl.ds(tb, 1), :] = hb
            new += [hf, hb]
        return tuple(new)

    init = tuple(st_ref[j, d:d + 1, :] for j in range(nbt) for d in range(2))
    out = lax.fori_loop(0, tc, body, init)
    for j in range(nbt):
        for d in range(2):
            st_ref[j, d:d + 1, :] = out[2 * j + d]
    fin_ref[...] = st_ref[...]


def _lru_scan(af, bf, ab, bb, batch, seq, h0=None):
    nbt = min(batch, 4)
    tc = min(seq, 256)
    n = seq // tc
    shp = (batch, seq, W_B)
    fspec = pl.BlockSpec((nbt, tc, W_B), lambda g, c: (g, c, 0))
    bspec = pl.BlockSpec((nbt, tc, W_B), lambda g, c: (g, n - 1 - c, 0))
    sspec = pl.BlockSpec((nbt, 2, W_B), lambda g, c: (g, 0, 0))
    args = [af.reshape(shp), bf.reshape(shp), ab.reshape(shp), bb.reshape(shp)]
    specs = [fspec, fspec, bspec, bspec]
    if h0 is not None:
        args.append(h0)
        specs.append(sspec)
    hf, hb, fin = pl.pallas_call(
        functools.partial(_lru_scan_kernel, nbt=nbt, tc=tc, has_init=h0 is not None),
        out_shape=(jax.ShapeDtypeStruct(shp, F32), jax.ShapeDtypeStruct(shp, F32),
                   jax.ShapeDtypeStruct((batch, 2, W_B), F32)),
        grid=(batch // nbt, n), in_specs=specs, out_specs=(fspec, bspec, sspec),
        scratch_shapes=[pltpu.VMEM((nbt, 2, W_B), F32)],
        compiler_params=_cparams(("parallel", "arbitrary")), name="lru_scan",
    )(*args)
    return hf.reshape(batch * seq, W_B), hb.reshape(batch * seq, W_B), fin


def _s5_param_kernel(prm_ref, bre_ref, bim_ref, bbar_ref, abar_ref):
    a_re, a_im, ldt = prm_ref[0, 0:1], prm_ref[0, 1:2], prm_ref[0, 2:3]
    dt = jnp.exp(ldt)
    mag = jnp.exp(a_re * dt)
    abr = mag * jnp.cos(a_im * dt)
    abi = mag * jnp.sin(a_im * dt)
    den = a_re * a_re + a_im * a_im
    fr = ((abr - 1.0) * a_re + abi * a_im) / den
    fi = (abi * a_re - (abr - 1.0) * a_im) / den
    bre, bim = bre_ref[0], bim_ref[0]
    gn = bre.shape[1]
    bbar_ref[0, :, 0:gn] = (fr * bre - fi * bim).astype(BF16)
    bbar_ref[0, :, gn:] = (fr * bim + fi * bre).astype(BF16)
    abar_ref[0, 0:1] = abr
    abar_ref[0, 1:2] = abi


def _s5_params(prm, bre, bim):
    gn = prm.shape[2]
    return pl.pallas_call(
        _s5_param_kernel,
        out_shape=(jax.ShapeDtypeStruct((2, W_C, 2 * gn), BF16), jax.ShapeDtypeStruct((2, 2, gn), F32)),
        grid=(2,),
        in_specs=[pl.BlockSpec((1, 3, gn), lambda d: (d, 0, 0)),
                  pl.BlockSpec((1, W_C, gn), lambda d: (d, 0, 0)),
                  pl.BlockSpec((1, W_C, gn), lambda d: (d, 0, 0))],
        out_specs=(pl.BlockSpec((1, W_C, 2 * gn), lambda d: (d, 0, 0)),
                   pl.BlockSpec((1, 2, gn), lambda d: (d, 0, 0))),
        compiler_params=_cparams(("arbitrary",)), name="s5_params",
    )(prm, bre, bim)


def _s5_scan_kernel(*refs, bp, tc, cw, has_init):
    it = iter(refs)
    uf_ref, ub_ref, bbar_ref, cc_ref, abar_ref = (next(it) for _ in range(5))
    s0_ref = next(it) if has_init else None
    yf_ref, yb_ref, fin_ref, buf_f, buf_b, st_ref = (next(it) for _ in range(6))
    c = pl.program_id(0)
    gn = abar_ref.shape[2]

    @pl.when(c == 0)
    def _():
        st_ref[...] = s0_ref[...] if has_init else jnp.zeros(st_ref.shape, F32)

    buf_f[...] = _dot(uf_ref[...].astype(BF16), bbar_ref[0])
    buf_b[...] = _dot(ub_ref[...].astype(BF16), bbar_ref[1])

    for kb in range(gn // cw):
        re = slice(kb * cw, (kb + 1) * cw)
        im = slice(gn + kb * cw, gn + (kb + 1) * cw)
        coef = []
        for d in range(2):
            coef.append((jnp.broadcast_to(abar_ref[d, 0:1, re], (bp, cw)),
                         jnp.broadcast_to(abar_ref[d, 1:2, re], (bp, cw))))

        def body(t, carry, re=re, im=im, coef=coef):
            out = []
            for d, buf in enumerate((buf_f, buf_b)):
                hr, hi = carry[2 * d], carry[2 * d + 1]
                ar, ai = coef[d]
                row = t if d == 0 else tc - 1 - t
                rows = pl.ds(pl.multiple_of(row * bp, bp), bp)
                nr = ar * hr - ai * hi + buf[rows, re]
                ni = ar * hi + ai * hr + buf[rows, im]
                buf[rows, re] = nr
                buf[rows, im] = ni
                out += [nr, ni]
            return tuple(out)

        init = (st_ref[0, :, re], st_ref[0, :, im], st_ref[1, :, re], st_ref[1, :, im])
        fr_, fi_, br_, bi_ = lax.fori_loop(0, tc, body, init)
        st_ref[0, :, re] = fr_
        st_ref[0, :, im] = fi_
        st_ref[1, :, re] = br_
        st_ref[1, :, im] = bi_

    yf_ref[...] = _dot(buf_f[...].astype(BF16), cc_ref[0])
    yb_ref[...] = _dot(buf_b[...].astype(BF16), cc_ref[1])
    fin_ref[...] = st_ref[...]


def _s5_scan(u_tm, bbar, cc, abar, bp, seq, s0=None):
    rows = 512
    tc = min(rows // bp, seq)
    rows = tc * bp
    n = seq // tc
    cw = 512 if bp <= SUBLANES else 128
    fspec = pl.BlockSpec((rows, W_C), lambda c: (c, 0))
    bspec = pl.BlockSpec((rows, W_C), lambda c: (n - 1 - c, 0))
    args = [u_tm, u_tm, bbar, cc, abar]
    specs = [fspec, bspec, _const_spec(bbar.shape), _const_spec(cc.shape), _const_spec(abar.shape)]
    if s0 is not None:
        args.append(s0)
        specs.append(_const_spec(s0.shape))
    return pl.pallas_call(
        functools.partial(_s5_scan_kernel, bp=bp, tc=tc, cw=cw, has_init=s0 is not None),
        out_shape=(jax.ShapeDtypeStruct(u_tm.shape, F32), jax.ShapeDtypeStruct(u_tm.shape, F32),
                   jax.ShapeDtypeStruct((2, bp, S5_STATE), F32)),
        grid=(n,), in_specs=specs,
        out_specs=(fspec, bspec, _const_spec((2, bp, S5_STATE))),
        scratch_shapes=[pltpu.VMEM((rows, S5_STATE), F32), pltpu.VMEM((rows, S5_STATE), F32),
                        pltpu.VMEM((2, bp, S5_STATE), F32)],
        compiler_params=_cparams(("arbitrary",)), name="s5_scan",
    )(*args)


def _s5_post_kernel(yf_ref, yb_ref, u_ref, d_ref, w_ref, b_ref, o_ref):
    ys = yf_ref[...] + yb_ref[...] + d_ref[...] * u_ref[...]
    z = _gelu_tanh(ys)
    o_ref[...] = z * _sigmoid(_dot(z.astype(BF16), w_ref[...]) + b_ref[...])


def _s5_post(yf, yb, u_tm, s5_d, glu_w, glu_b):
    n = u_tm.shape[0]
    rt = min(ROW_TILE, n)
    spec = pl.BlockSpec((rt, W_C), lambda i: (i, 0))
    return pl.pallas_call(
        _s5_post_kernel, out_shape=jax.ShapeDtypeStruct(u_tm.shape, F32), grid=(n // rt,),
        in_specs=[spec, spec, spec, _const_spec(s5_d.shape), _const_spec(glu_w.shape), _const_spec(glu_b.shape)],
        out_specs=spec, compiler_params=_cparams(("parallel",)), name="s5_post",
    )(yf, yb, u_tm, s5_d, glu_w, glu_b)


def _rwkv_prep_kernel(x_ref, xp_ref, xn_ref, vec_ref, w1_ref, w2_ref, a1_ref, a2_ref, g1_ref, g2_ref,
                      ones_ref, *out_refs, seq):
    (v_ref, kk_ref, g_ref, bon_ref) = out_refs[:4]
    dir_refs = (out_refs[4:10], out_refs[10:16])
    x = x_ref[...]
    tm = x.shape[0]
    ext = jnp.concatenate([xp_ref[...], x, xn_ref[...]], axis=0)
    pos = _seq_pos(x.shape, tm, seq)
    dx = 0.5 * (_seq_shift(ext, 1, pos, seq, tm) + _seq_shift(ext, -1, pos, seq, tm)) - x
    w = W_D
    r, k, v, xd = (x[:, j * w:(j + 1) * w] for j in range(4))
    dr, dk_, dv, dxd = (dx[:, j * w:(j + 1) * w] for j in range(4))
    vec = vec_ref[...]
    mu = vec[0:6]
    r = r + dr * mu[0:1]
    k = k + dk_ * mu[1:2]
    v = v + dv * mu[2:3]
    xw = xd + dxd * mu[3:4]
    xa = xd + dxd * mu[4:5]
    xg = xd + dxd * mu[5:6]
    rw_kk, rw_ka, rw_rk = vec[6:7], vec[7:8], vec[8:9]
    ones = ones_ref[...]
    g_ref[...] = _bdot(_sigmoid(_bdot(xg, g1_ref[...])), g2_ref[...])
    kkr = k * rw_kk
    kk = kkr / jnp.maximum(jnp.sqrt(_bdot(kkr * kkr, ones)), 1e-12)
    v_ref[...] = v
    kk_ref[...] = kk
    bonus = jnp.zeros(v.shape, F32)
    for d in range(2):
        w0, a0 = vec[9 + d:10 + d], vec[11 + d:12 + d]
        wl = -_softplus(-(w0 + _bdot(jnp.tanh(_bdot(xw, w1_ref[d])), w2_ref[d]))) - 0.5
        dec = jnp.exp(-jnp.exp(wl))
        a = _sigmoid(a0 + _bdot(_bdot(xa, a1_ref[d]), a2_ref[d]))
        kd = k * (1.0 + (a - 1.0) * rw_ka)
        ka = kk * a
        w_ref, ka_ref, kd_ref, wr_ref, c1_ref, c2_ref = dir_refs[d]
        w_ref[...] = dec
        ka_ref[...] = ka
        kd_ref[...] = kd
        wr_ref[...] = dec * r
        c1_ref[...] = _bdot(ka * r, ones)
        c2_ref[...] = _bdot(kd * r, ones)
        bonus = bonus + _bdot(r * kd * rw_rk, ones) * v
    bon_ref[...] = bonus


def _rwkv_prep(tiles, rkvx, vec, w1, w2, a1, a2, g1, g2, ones):
    n = rkvx.shape[0]
    prev, nxt = _halo_specs(tiles, 4 * W_D, 0, n)
    out = jax.ShapeDtypeStruct((n, W_D), F32)
    consts = [vec, w1, w2, a1, a2, g1, g2, ones]
    return pl.pallas_call(
        functools.partial(_rwkv_prep_kernel, seq=tiles.seq),
        out_shape=(out,) * 16, grid=(tiles.n_tiles,),
        in_specs=[tiles.row_spec(4 * W_D), prev, nxt] + [_const_spec(c.shape) for c in consts],
        out_specs=(tiles.row_spec(W_D),) * 16,
        compiler_params=_cparams(("parallel",)), name="rwkv_prep",
    )(rkvx, rkvx, rkvx, *consts)


def _rwkv_scan_kernel(*refs, nbt, tc, has_init):
    it = iter(refs)
    seqs = [tuple(next(it) for _ in range(8)) for _ in range(2)]
    ones_ref, eye_ref = next(it), next(it)
    s0_ref = next(it) if has_init else None
    yf_ref, yb_ref, fin_ref, st_ref = (next(it) for _ in range(4))
    c = pl.program_id(1)
    hd = st_ref.shape[2]

    @pl.when(c == 0)
    def _():
        st_ref[...] = s0_ref[...] if has_init else jnp.zeros(st_ref.shape, F32)

    ones = ones_ref[...]
    eye = eye_ref[...]

    def body(t, carry):
        for j in range(nbt):
            for d, y_ref in enumerate((yf_ref, yb_ref)):
                row = pl.ds(t if d == 0 else tc - 1 - t, 1)
                v, kk, w, ka, kd, wr, c1, c2 = (ref[j, row, :] for ref in seqs[d])
                s = st_ref[j, d]
                prod = jnp.concatenate([s * kk, s * wr, v * eye], axis=0).astype(BF16)
                red = _dot(prod, ones)
                sk, z, vb = red[0:hd], red[hd:2 * hd], red[2 * hd:3 * hd]
                y = z - sk * c1 + vb * c2
                y_ref[j, row, :] = jnp.sum(y * eye, axis=0, keepdims=True)
                st_ref[j, d] = s * w - sk * ka + vb * kd
        return carry

    lax.fori_loop(0, tc, body, 0)
    fin_ref[...] = st_ref[...]


def _rwkv_scan(fseqs, bseqs, ones, eye, batch, seq, s0=None):
    nbt = min(batch, 4)
    tc = min(seq, 64)
    n = seq // tc
    shp = (batch, seq, W_D)
    fspec = pl.BlockSpec((nbt, tc, W_D), lambda g, c: (g, c, 0))
    bspec = pl.BlockSpec((nbt, tc, W_D), lambda g, c: (g, n - 1 - c, 0))
    sspec = pl.BlockSpec((nbt, 2, HD_D, W_D), lambda g, c: (g, 0, 0, 0))
    args = [a.reshape(shp) for a in fseqs] + [a.reshape(shp) for a in bseqs] + [ones, eye]
    specs = [fspec] * 8 + [bspec] * 8 + [_const_spec(ones.shape), _const_spec(eye.shape)]
    if s0 is not None:
        args.append(s0)
        specs.append(sspec)
    yf, yb, fin = pl.pallas_call(
        functools.partial(_rwkv_scan_kernel, nbt=nbt, tc=tc, has_init=s0 is not None),
        out_shape=(jax.ShapeDtypeStruct(shp, F32), jax.ShapeDtypeStruct(shp, F32),
                   jax.ShapeDtypeStruct((batch, 2, HD_D, W_D), F32)),
        grid=(batch // nbt, n), in_specs=specs, out_specs=(fspec, bspec, sspec),
        scratch_shapes=[pltpu.VMEM((nbt, 2, HD_D, W_D), F32)],
        compiler_params=_cparams(("parallel", "arbitrary")), name="rwkv_scan",
    )(*args)
    return yf.reshape(batch * seq, W_D), yb.reshape(batch * seq, W_D), fin


def _block_diag(w):
    n, a, b = w.shape
    return jnp.einsum('nab,nm->namb', w, jnp.eye(n, dtype=w.dtype)).reshape(n * a, n * b)


def _rope_tables(seq, dk):
    t = jnp.arange(seq)
    row = (t // GRID_W).astype(F32)
    col = (t % GRID_W).astype(F32)
    nf = dk // 4
    freqs = ROPE_BASE ** (-jnp.arange(nf, dtype=F32) / nf)
    ar, ac = row[:, None] * freqs, col[:, None] * freqs
    cos = jnp.concatenate([jnp.cos(ar), jnp.cos(ar), jnp.cos(ac), jnp.cos(ac)], axis=1)
    sin = jnp.concatenate([-jnp.sin(ar), jnp.sin(ar), -jnp.sin(ac), jnp.sin(ac)], axis=1)
    return cos, sin


def _trunk(x3, cond_base, per_batch_cond, is_latent, states, mods, p):
    batch, seq, d = x3.shape
    tiles = _Tiles(batch, seq, cond_base, per_batch_cond)
    x = x3.reshape(batch * seq, d)
    s_ret, s_lru, s_s5, s_rwkv = states
    bp = max(batch, SUBLANES)

    m0, g0 = mods[0], p['norm_g'][0]
    x = _ffn(tiles, x, m0, g0, p['f1a'][0], p['f1b'][0], first=True)
    qkvg, gx = _proj_even(tiles, x, m0, g0, p['l0_w_in'])
    rope = _rope_tables(seq, DK_A) if is_latent else None
    y_a, new_ret = _retention(qkvg, p['ret_dec'], batch, seq, s0=s_ret, rope_tabs=rope)
    af, bf, ab, bb = _lru_prep(tiles, gx, p['conv_wb'], p['lru_wg'], p['lru_bg'], p['lru_lam'])
    hf, hb, new_lru = _lru_scan(af, bf, ab, bb, batch, seq, h0=s_lru)
    x = _ffn(tiles, x, m0, g0, p['f2a'][0], p['f2b'][0], first=False, mix="even",
             mix_args=(y_a, gx, hf, hb), wo=p['l0_w_out'])

    m1, g1 = mods[1], p['norm_g'][1]
    x = _ffn(tiles, x, m1, g1, p['f1a'][1], p['f1b'][1], first=True)
    u_tm, rkvx = _proj_odd(tiles, x, m1, g1, p['l1_w_in'], bp)
    u_tm = u_tm.reshape(seq * bp, W_C)
    s5_0 = None
    if s_s5 is not None:
        s5_0 = jnp.transpose(s_s5.reshape(batch, 2, S5_STATE), (1, 0, 2))
        s5_0 = jnp.pad(s5_0, ((0, 0), (0, bp - batch), (0, 0)))
    yf5, yb5, fin5 = _s5_scan(u_tm, p['s5_bbar'], p['s5_cc'], p['s5_abar'], bp, seq, s0=s5_0)
    yc_tm = _s5_post(yf5, yb5, u_tm, p['s5_d'], p['glu_w'], p['glu_b']).reshape(seq, bp * W_C)
    new_s5 = jnp.transpose(fin5[:, :batch], (1, 0, 2)).reshape(batch, 2, 2, G_C, N_C)
    outs = _rwkv_prep(tiles, rkvx, p['rw_vec'], p['rw_w1'], p['rw_w2'], p['rw_a1'], p['rw_a2'],
                      p['rw_g1'], p['rw_g2'], p['ones_hd'])
    v, kk, gg, bonus = outs[:4]
    fseqs = (v, kk) + tuple(outs[4:10])
    bseqs = (v, kk) + tuple(outs[10:16])
    rw0 = None
    if s_rwkv is not None:
        rw0 = jnp.transpose(s_rwkv, (0, 1, 3, 2, 4)).reshape(batch, 2, HD_D, W_D)
    yfr, ybr, finr = _rwkv_scan(fseqs, bseqs, p['ones_hd'], p['eye_hd'], batch, seq, s0=rw0)
    new_rwkv = jnp.transpose(finr.reshape(batch, 2, HD_D, H_D, HD_D), (0, 1, 3, 2, 4))
    x = _ffn(tiles, x, m1, g1, p['f2a'][1], p['f2b'][1], first=False, mix="odd",
             mix_args=(yc_tm, yfr, ybr, bonus, gg, p['avg_hd'], p['ln_wb']), wo=p['l1_w_out'],
             final_g=p['final_norm'])
    return x.reshape(batch, seq, d), (new_ret, new_lru, new_s5, new_rwkv)


def kernel(x_prompt, x_sample, state_l0_ret, state_l0_lru, state_l1_s5, state_l1_rwkv, c, c_ctx, mod_w, mod_b, norm_g, ffn1_w13, ffn1_w2, ffn2_w13, ffn2_w2, final_norm, l0_w_in, l0_w_out, l0_ret_decay, l0_conv_w, l0_conv_b, l0_lru_lam, l0_lru_wa, l0_lru_ba, l0_lru_wx, l0_lru_bx, l1_w_in, l1_w_out, l1_s5_a_re, l1_s5_a_im, l1_s5_log_dt, l1_s5_b_re, l1_s5_b_im, l1_s5_c_re, l1_s5_c_im, l1_s5_d, l1_glu_w, l1_glu_b, l1_rw_mu, l1_rw_w0, l1_rw_w1, l1_rw_w2, l1_rw_a0, l1_rw_a1, l1_rw_a2, l1_rw_g1, l1_rw_g2, l1_rw_kk, l1_rw_ka, l1_rw_rk, l1_ln_w, l1_ln_b):
    d = x_prompt.shape[-1]
    nlat = c.shape[0]
    cond8 = jnp.concatenate([c_ctx[None, :], c, jnp.zeros((SUBLANES - 1 - nlat, d), F32)], axis=0)
    mods = _modulation(cond8, mod_w, mod_b)

    gn = G_C * N_C
    eye_g = jnp.eye(G_C, dtype=F32)
    bre = jnp.einsum('dgns,gh->dgshn', l1_s5_b_re, eye_g).reshape(2, W_C, gn)
    bim = jnp.einsum('dgns,gh->dgshn', l1_s5_b_im, eye_g).reshape(2, W_C, gn)
    cre = jnp.einsum('dgsn,gh->dgnhs', l1_s5_c_re, eye_g).reshape(2, gn, W_C)
    cim = jnp.einsum('dgsn,gh->dgnhs', l1_s5_c_im, eye_g).reshape(2, gn, W_C)
    prm = jnp.stack([l1_s5_a_re.reshape(2, gn), l1_s5_a_im.reshape(2, gn),
                     jnp.repeat(l1_s5_log_dt, N_C, axis=1)], axis=1)
    s5_bbar, s5_abar = _s5_params(prm, bre, bim)
    head_id = jnp.arange(W_D) // HD_D
    ones_hd = (head_id[:, None] == head_id[None, :]).astype(BF16)
    p = dict(
        norm_g=norm_g, final_norm=final_norm.reshape(1, d),
        f1a=ffn1_w13.astype(BF16), f1b=ffn1_w2.astype(BF16),
        f2a=ffn2_w13.astype(BF16), f2b=ffn2_w2.astype(BF16),
        l0_w_in=l0_w_in.astype(BF16), l0_w_out=l0_w_out.astype(BF16),
        l1_w_in=l1_w_in.astype(BF16), l1_w_out=l1_w_out.astype(BF16),
        ret_dec=jnp.broadcast_to(l0_ret_decay.astype(F32)[:, :, None, None], (2, H_A, 1, DK_A)),
        conv_wb=jnp.concatenate([l0_conv_w, l0_conv_b[None, :]], axis=0),
        lru_wg=jnp.concatenate([_block_diag(l0_lru_wa[0]), _block_diag(l0_lru_wx[0]),
                                _block_diag(l0_lru_wa[1]), _block_diag(l0_lru_wx[1])], axis=1).astype(BF16),
        lru_bg=jnp.concatenate([l0_lru_ba[0], l0_lru_bx[0], l0_lru_ba[1], l0_lru_bx[1]])[None, :],
        lru_lam=l0_lru_lam,
        s5_bbar=s5_bbar, s5_abar=s5_abar,
        s5_cc=jnp.concatenate([cre, -cim], axis=1).astype(BF16),
        s5_d=l1_s5_d.reshape(1, W_C), glu_w=l1_glu_w.astype(BF16), glu_b=l1_glu_b.reshape(1, W_C),
        rw_vec=jnp.concatenate([l1_rw_mu, l1_rw_kk[None], l1_rw_ka[None], l1_rw_rk[None],
                                l1_rw_w0, l1_rw_a0, jnp.zeros((3, W_D), F32)], axis=0),
        rw_w1=l1_rw_w1.astype(BF16), rw_w2=l1_rw_w2.astype(BF16),
        rw_a1=l1_rw_a1.astype(BF16), rw_a2=l1_rw_a2.astype(BF16),
        rw_g1=l1_rw_g1.astype(BF16), rw_g2=l1_rw_g2.astype(BF16),
        ones_hd=ones_hd, avg_hd=(ones_hd.astype(F32) / HD_D).astype(BF16),
        eye_hd=jnp.tile(jnp.eye(HD_D, dtype=F32), (1, H_D)),
        ln_wb=jnp.stack([l1_ln_w, l1_ln_b], axis=0),
    )
    y_prompt, ctx_states = _trunk(x_prompt, 0, False, False, (None, None, None, None), mods, p)
    y_sample, _ = _trunk(x_sample, 1, True, True,
                         (state_l0_ret, state_l0_lru, state_l1_s5, state_l1_rwkv), mods, p)
    return (y_prompt, y_sample) + ctx_states
```

```python
import functools
import math

import jax
import jax.numpy as jnp
from jax import lax
from jax.experimental import pallas as pl
from jax.experimental.pallas import tpu as pltpu

F32 = jnp.float32
BF16 = jnp.bfloat16

N_MOD = 9
NORM_EPS = 1e-6
HEAD_NORM_EPS = 1e-5
RWKV_LN_EPS = 64e-5
ROPE_BASE = 10000.0
GRID_W = 64
H_A = 4
DK_A = 128
W_A = 512
W_B = 512
NB_B = 8
LRU_C = 8.0
W_C = 512
GS_C = 16
G_C = 32
N_C = 64
W_D = 512
HD_D = 64
H_D = 8
S5_STATE = 2 * G_C * N_C

SUBLANES = 8
LANES = 128
VMEM_LIMIT = 56 * 1024 * 1024
ROW_TILE = 512


def _cparams(sem):
    return pltpu.CompilerParams(dimension_semantics=sem, vmem_limit_bytes=VMEM_LIMIT)


def _dot(a, b):
    return jnp.dot(a, b, preferred_element_type=F32)


def _dot_nt(a, b):
    return lax.dot_general(a, b, (((1,), (1,)), ((), ())), preferred_element_type=F32)


def _dot_tn(a, b):
    return lax.dot_general(a, b, (((0,), (0,)), ((), ())), preferred_element_type=F32)


def _bdot(a, b):
    return _dot(a.astype(BF16), b.astype(BF16))


def _sigmoid(x):
    return 1.0 / (1.0 + jnp.exp(-x))


def _silu(x):
    return x * _sigmoid(x)


def _softplus(x):
    return jnp.maximum(x, 0.0) + jnp.log1p(jnp.exp(-jnp.abs(x)))


def _log_sigmoid(x):
    return -_softplus(-x)


def _gelu_tanh(x):
    return 0.5 * x * (1.0 + jnp.tanh(math.sqrt(2.0 / math.pi) * (x + 0.044715 * (x * x * x))))


def _rms(x):
    return x * lax.rsqrt(jnp.mean(x * x, axis=-1, keepdims=True) + NORM_EPS)


def _rms_mod(x, g, scale, shift):
    return (_rms(x) * g) * (1.0 + scale) + shift


def _const_spec(shape):
    nd = len(shape)
    return pl.BlockSpec(shape, lambda *_: (0,) * nd, pipeline_mode=pl.Buffered(1))


def _mod_kernel(c_ref, w_ref, b_ref, o_ref):
    o_ref[0] = _bdot(_silu(c_ref[...]), w_ref[0]) + b_ref[0]


def _modulation(cond8, mod_w, mod_b):
    depth, d, nd = mod_w.shape
    return pl.pallas_call(
        _mod_kernel,
        out_shape=jax.ShapeDtypeStruct((depth, SUBLANES, nd), F32),
        grid=(depth, nd // d),
        in_specs=[pl.BlockSpec((SUBLANES, d), lambda l, j: (0, 0)),
                  pl.BlockSpec((1, d, d), lambda l, j: (l, 0, j)),
                  pl.BlockSpec((1, 1, d), lambda l, j: (l, 0, j))],
        out_specs=pl.BlockSpec((1, SUBLANES, d), lambda l, j: (l, 0, j)),
        compiler_params=_cparams(("arbitrary", "arbitrary")),
        name="modulation",
    )(cond8, mod_w, mod_b.reshape(depth, 1, nd)).reshape(depth, SUBLANES, N_MOD, d)


class _Tiles:
    def __init__(self, batch, seq, cond_base, per_batch_cond, row_tile=ROW_TILE):
        self.batch, self.seq = batch, seq
        self.tm = min(row_tile, batch * seq)
        if seq >= self.tm:
            self.nb, self.tl = 1, self.tm
        else:
            self.nb, self.tl = self.tm // seq, seq
        assert seq % self.tl == 0 and batch % self.nb == 0
        self.tiles_per_seq = seq // self.tl
        self.n_tiles = batch * seq // self.tm
        self.cond_base = cond_base
        self.tiles_per_cond = self.tiles_per_seq if per_batch_cond else self.n_tiles

    def cond_row(self, i):
        return self.cond_base + i // self.tiles_per_cond

    def mod_spec(self, d):
        return pl.BlockSpec((1, N_MOD, d), lambda i: (self.cond_row(i), 0, 0))

    def row_spec(self, width, col=0):
        return pl.BlockSpec((self.tm, width), lambda i: (i, col))

    def tm_spec(self, width):
        return pl.BlockSpec((self.tl, self.nb * width),
                            lambda i: (i % self.tiles_per_seq, i // self.tiles_per_seq))


def _from_time_major(blk, nb, width):
    if nb == 1:
        return blk
    return jnp.concatenate([blk[:, j * width:(j + 1) * width] for j in range(nb)], axis=0)


def _ffn_kernel(*refs, mix, first, final, nb, ff_chunks):
    it = iter(refs)
    x_ref, m_ref, g_ref, w13_ref, w2_ref = (next(it) for _ in range(5))
    x = x_ref[...]
    m = m_ref[0]
    if mix == "even":
        ya_ref, gb_ref, hf_ref, hb_ref, wo_ref = (next(it) for _ in range(5))
        ya = ya_ref[...]
        yb = _gelu_tanh(gb_ref[...]) * (hf_ref[...] + hb_ref[...])
    elif mix == "odd":
        yc_ref, yf_ref, yb_ref, bon_ref, gg_ref, avg_ref, ln_ref, wo_ref = (next(it) for _ in range(8))
        ya = _from_time_major(yc_ref[...], nb, W_C)
        ys = yf_ref[...] + yb_ref[...]
        avg = avg_ref[...]
        oc = ys - _bdot(ys, avg)
        yd = oc * lax.rsqrt(_bdot(oc * oc, avg) + RWKV_LN_EPS)
        yd = yd * ln_ref[0:1] + ln_ref[1:2]
        yb = (yd + bon_ref[...]) * gg_ref[...]
    if mix is not None:
        wa = ya.shape[1]
        y = _dot(ya.astype(BF16), wo_ref[0:wa, :]) + _dot(yb.astype(BF16), wo_ref[wa:, :])
        x = x + m[5:6] * y
    fin_ref = next(it) if final else None
    o_ref = next(it)
    r0 = 0 if first else 6
    gi = 0 if first else 2
    h = _rms_mod(x, g_ref[gi:gi + 1], m[r0 + 1:r0 + 2], m[r0:r0 + 1]).astype(BF16)
    dff = w2_ref.shape[0]
    fc = dff // ff_chunks
    acc = jnp.zeros(x.shape, F32)
    for j in range(ff_chunks):
        gate = _dot(h, w13_ref[:, j * fc:(j + 1) * fc])
        up = _dot(h, w13_ref[:, dff + j * fc:dff + (j + 1) * fc])
        acc = acc + _dot((_silu(gate) * up).astype(BF16), w2_ref[j * fc:(j + 1) * fc, :])
    x = x + 0.5 * m[r0 + 2:r0 + 3] * acc
    if final:
        x = _rms(x) * fin_ref[...]
    o_ref[...] = x


def _ffn(tiles, x, mods, norm_g, w13, w2, *, first, mix=None, mix_args=(), wo=None, final_g=None):
    n, d = x.shape
    dff = w2.shape[0]
    args = [x, mods, norm_g, w13, w2]
    specs = [tiles.row_spec(d), tiles.mod_spec(d), _const_spec(norm_g.shape),
             _const_spec(w13.shape), _const_spec(w2.shape)]
    if mix == "even":
        ya, gx, hf, hb = mix_args
        args += [ya, gx, hf, hb, wo]
        specs += [tiles.row_spec(W_A), tiles.row_spec(W_B, 0), tiles.row_spec(W_B), tiles.row_spec(W_B),
                  _const_spec(wo.shape)]
    elif mix == "odd":
        yc_tm, yf, yb, bon, gg, avg, ln = mix_args
        args += [yc_tm, yf, yb, bon, gg, avg, ln, wo]
        specs += [tiles.tm_spec(W_C)] + [tiles.row_spec(W_D)] * 4 + [
            _const_spec(avg.shape), _const_spec(ln.shape), _const_spec(wo.shape)]
    if final_g is not None:
        args.append(final_g)
        specs.append(_const_spec(final_g.shape))
    kern = functools.partial(_ffn_kernel, mix=mix, first=first, final=final_g is not None,
                             nb=tiles.nb, ff_chunks=2 if dff % (2 * LANES) == 0 else 1)
    return pl.pallas_call(
        kern, out_shape=jax.ShapeDtypeStruct((n, d), F32), grid=(tiles.n_tiles,),
        in_specs=specs, out_specs=tiles.row_spec(d),
        compiler_params=_cparams(("parallel",)), name="ffn",
    )(*args)


def _proj_even_kernel(x_ref, m_ref, g_ref, w_ref, qkvg_ref, gx_ref):
    m = m_ref[0]
    h = _rms_mod(x_ref[...], g_ref[1:2], m[4:5], m[3:4]).astype(BF16)
    p = _dot(h, w_ref[...])
    nq = qkvg_ref.shape[1]
    qkvg_ref[...] = p[:, :nq]
    gx_ref[...] = p[:, nq:]


def _proj_odd_kernel(x_ref, m_ref, g_ref, w_ref, zero_ref, u_ref, rkvx_ref, *, nb, tl):
    del zero_ref
    m = m_ref[0]
    h = _rms_mod(x_ref[...], g_ref[1:2], m[4:5], m[3:4]).astype(BF16)
    p = _dot(h, w_ref[...])
    for j in range(nb):
        u_ref[:, j * W_C:(j + 1) * W_C] = p[j * tl:(j + 1) * tl, :W_C]
    rkvx_ref[...] = p[:, W_C:]


def _proj_even(tiles, x, mods, norm_g, w_in):
    n, d = x.shape
    return pl.pallas_call(
        _proj_even_kernel,
        out_shape=(jax.ShapeDtypeStruct((n, 4 * W_A), F32), jax.ShapeDtypeStruct((n, 2 * W_B), F32)),
        grid=(tiles.n_tiles,),
        in_specs=[tiles.row_spec(d), tiles.mod_spec(d), _const_spec(norm_g.shape), _const_spec(w_in.shape)],
        out_specs=(tiles.row_spec(4 * W_A), tiles.row_spec(2 * W_B)),
        compiler_params=_cparams(("parallel",)), name="proj_even",
    )(x, mods, norm_g, w_in)


def _proj_odd(tiles, x, mods, norm_g, w_in, bp):
    n, d = x.shape
    zeros = jnp.zeros((tiles.seq, bp * W_C), F32)
    kern = functools.partial(_proj_odd_kernel, nb=tiles.nb, tl=tiles.tl)
    return pl.pallas_call(
        kern,
        out_shape=(jax.ShapeDtypeStruct((tiles.seq, bp * W_C), F32), jax.ShapeDtypeStruct((n, 4 * W_D), F32)),
        grid=(tiles.n_tiles,),
        in_specs=[tiles.row_spec(d), tiles.mod_spec(d), _const_spec(norm_g.shape), _const_spec(w_in.shape),
                  pl.BlockSpec(memory_space=pl.ANY)],
        out_specs=(tiles.tm_spec(W_C), tiles.row_spec(4 * W_D)),
        input_output_aliases={4: 0},
        compiler_params=_cparams(("parallel",)), name="proj_odd",
    )(x, mods, norm_g, w_in, zeros)


def _ret_kernel(*refs, seq, chunk, rope, has_init):
    it = iter(refs)
    q_ref, k_ref, v_ref, g_ref, dec_ref = (next(it) for _ in range(5))
    cos_ref, sin_ref = (next(it), next(it)) if rope else (None, None)
    s0_ref = next(it) if has_init else None
    y_ref, sf_ref, o_scr = next(it), next(it), next(it)
    n = seq // chunk
    dk = q_ref.shape[1]
    lgf = _log_sigmoid(dec_ref[0, 0])
    lgb = _log_sigmoid(dec_ref[1, 0])
    ri = lax.broadcasted_iota(jnp.int32, (chunk, dk), 0).astype(F32)
    qf_t = jnp.exp(lgf * (ri + 1.0))
    qb_t = jnp.exp(lgb * (chunk - ri))
    kf_t = jnp.exp(lgf * (chunk - 1.0 - ri))
    kb_t = jnp.exp(lgb * ri)
    reps = chunk // dk
    lgf_c = jnp.concatenate([lgf] * reps, axis=1) if reps > 1 else lgf[:, :chunk]
    lgb_c = jnp.concatenate([lgb] * reps, axis=1) if reps > 1 else lgb[:, :chunk]
    rel = (lax.broadcasted_iota(jnp.int32, (chunk, chunk), 0)
           - lax.broadcasted_iota(jnp.int32, (chunk, chunk), 1)).astype(F32)
    dmask = (jnp.where(rel >= 0, jnp.exp(lgf_c * jnp.maximum(rel, 0.0)), 0.0)
             + jnp.where(rel <= 0, jnp.exp(lgb_c * jnp.maximum(-rel, 0.0)), 0.0))
    gcf = jnp.exp(lgf * float(chunk))
    gcb = jnp.exp(lgb * float(chunk))
    lane = lax.broadcasted_iota(jnp.int32, (chunk, dk), 1)
    first_half = (lane & (dk // 2 - 1)) < dk // 4

    def load_qkv(c):
        rows = pl.ds(pl.multiple_of(c * chunk, chunk), chunk)
        q, k, v = q_ref[rows, :], k_ref[rows, :], v_ref[rows, :]
        if rope:
            cs, sn = cos_ref[rows, :], sin_ref[rows, :]

            def rot(x):
                partner = jnp.where(first_half, pltpu.roll(x, dk - dk // 4, axis=1),
                                    pltpu.roll(x, dk // 4, axis=1))
                return x * cs + partner * sn

            q, k = rot(q), rot(k)
        return rows, q * (dk ** -0.5), k, v

    def fwd(c, s):
        rows, q, k, v = load_qkv(c)
        scores = _dot_nt(q.astype(BF16), k.astype(BF16)) * dmask
        inner = _bdot(scores, v)
        o_scr[rows, :] = inner + _bdot(q * qf_t, s)
        return s * gcf + _dot_tn((k * kf_t).astype(BF16), v.astype(BF16))

    def bwd(j, s):
        rows, q, k, v = load_qkv(n - 1 - j)
        o = o_scr[rows, :] + _bdot(q * qb_t, s)
        oc = o - jnp.mean(o, axis=-1, keepdims=True)
        on = oc * lax.rsqrt(jnp.mean(oc * oc, axis=-1, keepdims=True) + HEAD_NORM_EPS)
        y_ref[rows, :] = on * _silu(g_ref[rows, :])
        return s * gcb + _dot_tn((k * kb_t).astype(BF16), v.astype(BF16))

    zero = jnp.zeros((dk, v_ref.shape[1]), F32)
    sf = lax.fori_loop(0, n, fwd, s0_ref[0, 0, 0] if has_init else zero)
    sb = lax.fori_loop(0, n, bwd, s0_ref[0, 1, 0] if has_init else zero)
    sf_ref[0, 0, 0] = sf
    sf_ref[0, 1, 0] = sb


def _retention(qkvg, dec, batch, seq, s0=None, rope_tabs=None):
    dk = DK_A
    chunk = min(seq, 256)
    args = [qkvg] * 4 + [dec]
    specs = [pl.BlockSpec((seq, dk), lambda b, h, j=j: (b, j * H_A + h)) for j in range(4)]
    specs.append(pl.BlockSpec((2, 1, 1, dk), lambda b, h: (0, h, 0, 0)))
    if rope_tabs is not None:
        args += list(rope_tabs)
        specs += [_const_spec((seq, dk))] * 2
    if s0 is not None:
        args.append(s0)
        specs.append(pl.BlockSpec((1, 2, 1, dk, dk), lambda b, h: (b, 0, h, 0, 0)))
    kern = functools.partial(_ret_kernel, seq=seq, chunk=chunk, rope=rope_tabs is not None,
                             has_init=s0 is not None)
    return pl.pallas_call(
        kern,
        out_shape=(jax.ShapeDtypeStruct((batch * seq, W_A), F32),
                   jax.ShapeDtypeStruct((batch, 2, H_A, dk, dk), F32)),
        grid=(batch, H_A),
        in_specs=specs,
        out_specs=(pl.BlockSpec((seq, dk), lambda b, h: (b, h)),
                   pl.BlockSpec((1, 2, 1, dk, dk), lambda b, h: (b, 0, h, 0, 0))),
        scratch_shapes=[pltpu.VMEM((seq, dk), F32)],
        compiler_params=_cparams(("parallel", "arbitrary")), name="retention",
    )(*args)


def _halo_specs(tiles, width, col, n_rows):
    tm = tiles.tm
    nblk = n_rows // SUBLANES
    per = tm // SUBLANES
    prev = pl.BlockSpec((SUBLANES, width), lambda i: (jnp.maximum(i * per - 1, 0), col))
    nxt = pl.BlockSpec((SUBLANES, width), lambda i: (jnp.minimum((i + 1) * per, nblk - 1), col))
    return prev, nxt


def _seq_shift(ext, s, pos, seq, tm):
    sh = pltpu.roll(ext, s % ext.shape[0], axis=0)[SUBLANES:SUBLANES + tm]
    ok = (pos >= s) if s > 0 else (pos < seq + s)
    return jnp.where(ok, sh, 0.0)


def _seq_pos(shape, tm, seq):
    return (pl.program_id(0) * tm + lax.broadcasted_iota(jnp.int32, shape, 0)) & (seq - 1)


def _lru_prep_kernel(x_ref, xp_ref, xn_ref, cw_ref, wg_ref, bg_ref, lam_ref,
                     af_ref, bf_ref, ab_ref, bb_ref, *, seq):
    x = x_ref[...]
    tm = x.shape[0]
    ext = jnp.concatenate([xp_ref[...], x, xn_ref[...]], axis=0)
    pos = _seq_pos(x.shape, tm, seq)
    xc = (cw_ref[4:5] + cw_ref[0:1] * _seq_shift(ext, 2, pos, seq, tm)
          + cw_ref[1:2] * _seq_shift(ext, 1, pos, seq, tm) + cw_ref[2:3] * x
          + cw_ref[3:4] * _seq_shift(ext, -1, pos, seq, tm))
    gates = _dot(xc.astype(BF16), wg_ref[...]) + bg_ref[...]
    w = x.shape[1]
    for d, (a_ref, b_ref) in enumerate(((af_ref, bf_ref), (ab_ref, bb_ref))):
        r = _sigmoid(gates[:, (2 * d) * w:(2 * d + 1) * w])
        i = _sigmoid(gates[:, (2 * d + 1) * w:(2 * d + 2) * w])
        log_a = LRU_C * r * _log_sigmoid(lam_ref[d:d + 1])
        a = jnp.exp(log_a)
        a_ref[...] = a
        b_ref[...] = jnp.sqrt(1.0 - a * a) * (i * xc)


def _lru_prep(tiles, gx, conv_wb, wg, bg, lam):
    n = gx.shape[0]
    prev, nxt = _halo_specs(tiles, W_B, 1, n)
    out = jax.ShapeDtypeStruct((n, W_B), F32)
    return pl.pallas_call(
        functools.partial(_lru_prep_kernel, seq=tiles.seq),
        out_shape=(out,) * 4, grid=(tiles.n_tiles,),
        in_specs=[tiles.row_spec(W_B, 1), prev, nxt, _const_spec(conv_wb.shape), _const_spec(wg.shape),
                  _const_spec(bg.shape), _const_spec(lam.shape)],
        out_specs=(tiles.row_spec(W_B),) * 4,
        compiler_params=_cparams(("parallel",)), name="lru_prep",
    )(gx, gx, gx, conv_wb, wg, bg, lam)


def _lru_scan_kernel(*refs, nbt, tc, has_init):
    it = iter(refs)
    af_ref, bf_ref, ab_ref, bb_ref = (next(it) for _ in range(4))
    h0_ref = next(it) if has_init else None
    hf_ref, hb_ref, fin_ref, st_ref = (next(it) for _ in range(4))
    c = pl.program_id(1)

    @pl.when(c == 0)
    def _():
        st_ref[...] = h0_ref[...] if has_init else jnp.zeros(st_ref.shape, F32)

    def body(t, carry):
        tb = tc - 1 - t
        new = []
        for j in range(nbt):
            hf, hb = carry[2 * j], carry[2 * j + 1]
            hf = af_ref[j, pl.ds(t, 1), :] * hf + bf_ref[j, pl.ds(t, 1), :]
            hb = ab_ref[j, pl.ds(tb, 1), :] * hb + bb_ref[j, pl.ds(tb, 1), :]
            hf_ref[j, pl.ds(t, 1), :] = hf
            hb_ref[j, pl.ds(tb, 1), :] = hb
            new += [hf, hb]
        return tuple(new)

    init = tuple(st_ref[j, d:d + 1, :] for j in range(nbt) for d in range(2))
    out = lax.fori_loop(0, tc, body, init)
    for j in range(nbt):
        for d in range(2):
            st_ref[j, d:d + 1, :] = out[2 * j + d]
    fin_ref[...] = st_ref[...]


def _lru_scan(af, bf, ab, bb, batch, seq, h0=None):
    nbt = min(batch, 4)
    tc = min(seq, 256)
    n = seq // tc
    shp = (batch, seq, W_B)
    fspec = pl.BlockSpec((nbt, tc, W_B), lambda g, c: (g, c, 0))
    bspec = pl.BlockSpec((nbt, tc, W_B), lambda g, c: (g, n - 1 - c, 0))
    sspec = pl.BlockSpec((nbt, 2, W_B), lambda g, c: (g, 0, 0))
    args = [af.reshape(shp), bf.reshape(shp), ab.reshape(shp), bb.reshape(shp)]
    specs = [fspec, fspec, bspec, bspec]
    if h0 is not None:
        args.append(h0)
        specs.append(sspec)
    hf, hb, fin = pl.pallas_call(
        functools.partial(_lru_scan_kernel, nbt=nbt, tc=tc, has_init=h0 is not None),
        out_shape=(jax.ShapeDtypeStruct(shp, F32), jax.ShapeDtypeStruct(shp, F32),
                   jax.ShapeDtypeStruct((batch, 2, W_B), F32)),
        grid=(batch // nbt, n), in_specs=specs, out_specs=(fspec, bspec, sspec),
        scratch_shapes=[pltpu.VMEM((nbt, 2, W_B), F32)],
        compiler_params=_cparams(("parallel", "arbitrary")), name="lru_scan",
    )(*args)
    return hf.reshape(batch * seq, W_B), hb.reshape(batch * seq, W_B), fin


def _s5_param_kernel(prm_ref, bre_ref, bim_ref, bbar_ref, abar_ref):
    a_re, a_im, ldt = prm_ref[0, 0:1], prm_ref[0, 1:2], prm_ref[0, 2:3]
    dt = jnp.exp(ldt)
    mag = jnp.exp(a_re * dt)
    abr = mag * jnp.cos(a_im * dt)
    abi = mag * jnp.sin(a_im * dt)
    den = a_re * a_re + a_im * a_im
    fr = ((abr - 1.0) * a_re + abi * a_im) / den
    fi = (abi * a_re - (abr - 1.0) * a_im) / den
    bre, bim = bre_ref[0], bim_ref[0]
    gn = bre.shape[1]
    bbar_ref[0, :, 0:gn] = (fr * bre - fi * bim).astype(BF16)
    bbar_ref[0, :, gn:] = (fr * bim + fi * bre).astype(BF16)
    abar_ref[0, 0:1] = abr
    abar_ref[0, 1:2] = abi


def _s5_params(prm, bre, bim):
    gn = prm.shape[2]
    return pl.pallas_call(
        _s5_param_kernel,
        out_shape=(jax.ShapeDtypeStruct((2, W_C, 2 * gn), BF16), jax.ShapeDtypeStruct((2, 2, gn), F32)),
        grid=(2,),
        in_specs=[pl.BlockSpec((1, 3, gn), lambda d: (d, 0, 0)),
                  pl.BlockSpec((1, W_C, gn), lambda d: (d, 0, 0)),
                  pl.BlockSpec((1, W_C, gn), lambda d: (d, 0, 0))],
        out_specs=(pl.BlockSpec((1, W_C, 2 * gn), lambda d: (d, 0, 0)),
                   pl.BlockSpec((1, 2, gn), lambda d: (d, 0, 0))),
        compiler_params=_cparams(("arbitrary",)), name="s5_params",
    )(prm, bre, bim)


def _s5_scan_kernel(*refs, bp, tc, cw, has_init):
    it = iter(refs)
    uf_ref, ub_ref, bbar_ref, cc_ref, abar_ref = (next(it) for _ in range(5))
    s0_ref = next(it) if has_init else None
    yf_ref, yb_ref, fin_ref, buf_f, buf_b, st_ref = (next(it) for _ in range(6))
    c = pl.program_id(0)
    gn = abar_ref.shape[2]

    @pl.when(c == 0)
    def _():
        st_ref[...] = s0_ref[...] if has_init else jnp.zeros(st_ref.shape, F32)

    buf_f[...] = _dot(uf_ref[...].astype(BF16), bbar_ref[0])
    buf_b[...] = _dot(ub_ref[...].astype(BF16), bbar_ref[1])

    for kb in range(gn // cw):
        re = slice(kb * cw, (kb + 1) * cw)
        im = slice(gn + kb * cw, gn + (kb + 1) * cw)
        coef = []
        for d in range(2):
            coef.append((jnp.broadcast_to(abar_ref[d, 0:1, re], (bp, cw)),
                         jnp.broadcast_to(abar_ref[d, 1:2, re], (bp, cw))))

        def body(t, carry, re=re, im=im, coef=coef):
            out = []
            for d, buf in enumerate((buf_f, buf_b)):
                hr, hi = carry[2 * d], carry[2 * d + 1]
                ar, ai = coef[d]
                row = t if d == 0 else tc - 1 - t
                rows = pl.ds(pl.multiple_of(row * bp, bp), bp)
                nr = ar * hr - ai * hi + buf[rows, re]
                ni = ar * hi + ai * hr + buf[rows, im]
                buf[rows, re] = nr
                buf[rows, im] = ni
                out += [nr, ni]
            return tuple(out)

        init = (st_ref[0, :, re], st_ref[0, :, im], st_ref[1, :, re], st_ref[1, :, im])
        fr_, fi_, br_, bi_ = lax.fori_loop(0, tc, body, init)
        st_ref[0, :, re] = fr_
        st_ref[0, :, im] = fi_
        st_ref[1, :, re] = br_
        st_ref[1, :, im] = bi_

    yf_ref[...] = _dot(buf_f[...].astype(BF16), cc_ref[0])
    yb_ref[...] = _dot(buf_b[...].astype(BF16), cc_ref[1])
    fin_ref[...] = st_ref[...]


def _s5_scan(u_tm, bbar, cc, abar, bp, seq, s0=None):
    rows = 512
    tc = min(rows // bp, seq)
    rows = tc * bp
    n = seq // tc
    cw = 512 if bp <= SUBLANES else 128
    fspec = pl.BlockSpec((rows, W_C), lambda c: (c, 0))
    bspec = pl.BlockSpec((rows, W_C), lambda c: (n - 1 - c, 0))
    args = [u_tm, u_tm, bbar, cc, abar]
    specs = [fspec, bspec, _const_spec(bbar.shape), _const_spec(cc.shape), _const_spec(abar.shape)]
    if s0 is not None:
        args.append(s0)
        specs.append(_const_spec(s0.shape))
    return pl.pallas_call(
        functools.partial(_s5_scan_kernel, bp=bp, tc=tc, cw=cw, has_init=s0 is not None),
        out_shape=(jax.ShapeDtypeStruct(u_tm.shape, F32), jax.ShapeDtypeStruct(u_tm.shape, F32),
                   jax.ShapeDtypeStruct((2, bp, S5_STATE), F32)),
        grid=(n,), in_specs=specs,
        out_specs=(fspec, bspec, _const_spec((2, bp, S5_STATE))),
        scratch_shapes=[pltpu.VMEM((rows, S5_STATE), F32), pltpu.VMEM((rows, S5_STATE), F32),
                        pltpu.VMEM((2, bp, S5_STATE), F32)],
        compiler_params=_cparams(("arbitrary",)), name="s5_scan",
    )(*args)


def _s5_post_kernel(yf_ref, yb_ref, u_ref, d_ref, w_ref, b_ref, o_ref):
    ys = yf_ref[...] + yb_ref[...] + d_ref[...] * u_ref[...]
    z = _gelu_tanh(ys)
    o_ref[...] = z * _sigmoid(_dot(z.astype(BF16), w_ref[...]) + b_ref[...])


def _s5_post(yf, yb, u_tm, s5_d, glu_w, glu_b):
    n = u_tm.shape[0]
    rt = min(ROW_TILE, n)
    spec = pl.BlockSpec((rt, W_C), lambda i: (i, 0))
    return pl.pallas_call(
        _s5_post_kernel, out_shape=jax.ShapeDtypeStruct(u_tm.shape, F32), grid=(n // rt,),
        in_specs=[spec, spec, spec, _const_spec(s5_d.shape), _const_spec(glu_w.shape), _const_spec(glu_b.shape)],
        out_specs=spec, compiler_params=_cparams(("parallel",)), name="s5_post",
    )(yf, yb, u_tm, s5_d, glu_w, glu_b)


def _rwkv_prep_kernel(x_ref, xp_ref, xn_ref, vec_ref, w1_ref, w2_ref, a1_ref, a2_ref, g1_ref, g2_ref,
                      ones_ref, *out_refs, seq):
    (v_ref, kk_ref, g_ref, bon_ref, r_ref) = out_refs[:5]
    dir_refs = (out_refs[5:8], out_refs[8:11])
    x = x_ref[...]
    tm = x.shape[0]
    ext = jnp.concatenate([xp_ref[...], x, xn_ref[...]], axis=0)
    pos = _seq_pos(x.shape, tm, seq)
    dx = 0.5 * (_seq_shift(ext, 1, pos, seq, tm) + _seq_shift(ext, -1, pos, seq, tm)) - x
    w = W_D
    r, k, v, xd = (x[:, j * w:(j + 1) * w] for j in range(4))
    dr, dk_, dv, dxd = (dx[:, j * w:(j + 1) * w] for j in range(4))
    vec = vec_ref[...]
    mu = vec[0:6]
    r = r + dr * mu[0:1]
    k = k + dk_ * mu[1:2]
    v = v + dv * mu[2:3]
    xw = xd + dxd * mu[3:4]
    xa = xd + dxd * mu[4:5]
    xg = xd + dxd * mu[5:6]
    rw_kk, rw_ka, rw_rk = vec[6:7], vec[7:8], vec[8:9]
    ones = ones_ref[...]
    g_ref[...] = _bdot(_sigmoid(_bdot(xg, g1_ref[...])), g2_ref[...])
    kkr = k * rw_kk
    kk = kkr / jnp.maximum(jnp.sqrt(_bdot(kkr * kkr, ones)), 1e-12)
    v_ref[...] = v
    kk_ref[...] = kk
    r_ref[...] = r
    bonus = jnp.zeros(v.shape, F32)
    for d in range(2):
        w0, a0 = vec[9 + d:10 + d], vec[11 + d:12 + d]
        wl = -_softplus(-(w0 + _bdot(jnp.tanh(_bdot(xw, w1_ref[d])), w2_ref[d]))) - 0.5
        a = _sigmoid(a0 + _bdot(_bdot(xa, a1_ref[d]), a2_ref[d]))
        kd = k * (1.0 + (a - 1.0) * rw_ka)
        lw_ref, ka_ref, kd_ref = dir_refs[d]
        lw_ref[...] = -jnp.exp(wl)
        ka_ref[...] = kk * a
        kd_ref[...] = kd
        bonus = bonus + _bdot(r * kd * rw_rk, ones) * v
    bon_ref[...] = bonus


N_RWKV_PREP_OUT = 11


def _rwkv_prep(tiles, rkvx, vec, w1, w2, a1, a2, g1, g2, ones):
    n = rkvx.shape[0]
    prev, nxt = _halo_specs(tiles, 4 * W_D, 0, n)
    out = jax.ShapeDtypeStruct((n, W_D), F32)
    consts = [vec, w1, w2, a1, a2, g1, g2, ones]
    return pl.pallas_call(
        functools.partial(_rwkv_prep_kernel, seq=tiles.seq),
        out_shape=(out,) * N_RWKV_PREP_OUT, grid=(tiles.n_tiles,),
        in_specs=[tiles.row_spec(4 * W_D), prev, nxt] + [_const_spec(c.shape) for c in consts],
        out_specs=(tiles.row_spec(W_D),) * N_RWKV_PREP_OUT,
        compiler_params=_cparams(("parallel",)), name="rwkv_prep",
    )(rkvx, rkvx, rkvx, *consts)


RWKV_CHUNK = 64
RWKV_PAIR = 2 * HD_D


def _bmm(a, b, precision=None):
    return jnp.einsum('umk,ukn->umn', a, b, precision=precision, preferred_element_type=F32)


def _bmm_nt(a, b):
    return jnp.einsum('umk,unk->umn', a, b, preferred_element_type=F32)


def _bmm_tn(a, b):
    return jnp.stack([_dot_tn(a[u], b[u]) for u in range(a.shape[0])], axis=0)


def _bf(a):
    return a.astype(BF16)


def _rwkv_masks(t, bwd):
    n2 = 2 * t
    ri = lax.broadcasted_iota(jnp.int32, (n2, n2), 0)
    ci = lax.broadcasted_iota(jnp.int32, (n2, n2), 1)
    same_head = (ri >= t) == (ci >= t)
    tt, ss = ri & (t - 1), ci & (t - 1)
    before = (ss > tt) if bwd else (ss < tt)
    ti = lax.broadcasted_iota(jnp.int32, (t, t), 0)
    si = lax.broadcasted_iota(jnp.int32, (t, t), 1)
    return dict(
        strict=same_head & before,
        incl=same_head & (before | (ss == tt)),
        same16=(ri >> 4) == (ci >> 4),
        same32=(ri >> 5) == (ci >> 5),
        eye=(ri == ci).astype(F32),
        tri=((si >= ti) if bwd else (si <= ti)).astype(F32),
        head0=lax.broadcasted_iota(jnp.int32, (t, RWKV_PAIR), 1) < HD_D,
    )


def _split(a):
    hi = a.astype(BF16)
    return hi, (a - hi.astype(F32)).astype(BF16)


def _bmm3(a, b):
    ah, al = _split(a)
    bh, bl = _split(b)
    n = b.shape[2]
    lhs = jnp.concatenate([ah, al], axis=2)
    rhs = jnp.concatenate([jnp.concatenate([bh, bl], axis=2),
                           jnp.concatenate([bh, jnp.zeros_like(bl)], axis=2)], axis=1)
    out = _bmm(lhs, rhs)
    return out[:, :, :n] + out[:, :, n:]


def _unit_tri_inverse(a, mk):
    eye = mk['eye']
    n1 = -jnp.where(mk['same16'], a, 0.0)
    n2 = _bmm3(n1, n1)
    n4 = _bmm3(n2, n2)
    n8 = _bmm3(n4, n4)
    p = _bmm3(eye + n1, eye + n2)
    p = _bmm3(p, eye + n4)
    p = _bmm3(p, eye + n8)
    c1 = jnp.where(mk['same32'], jnp.where(mk['same16'], 0.0, a), 0.0)
    p = p - _bmm3(_bmm3(p, c1), p)
    c2 = jnp.where(mk['same32'], 0.0, a)
    return p - _bmm3(_bmm3(p, c2), p)


def _rwkv_unit(r, v, kap, lw, beta, kd, s, mk, bwd):
    u, t, _ = r.shape
    tri = jnp.broadcast_to(mk['tri'][None], (u, t, t))
    c = _bmm(tri, lw, precision=lax.Precision.HIGHEST)
    c_end = c[:, 0:1] if bwd else c[:, t - 1:t]
    e_neg = jnp.exp(-c)
    e_end = jnp.exp(c_end - c)
    head0 = mk['head0']

    def stack(x):
        return jnp.concatenate([jnp.where(head0, x, 0.0), jnp.where(head0, 0.0, x)], axis=1)

    rs = stack(r * jnp.exp(c))
    ks = stack(kap * jnp.exp(c - lw))
    bs = stack(beta * e_neg)
    kds = stack(kd * e_neg)
    vs = stack(v)
    bps = stack(beta * e_end)
    kps = stack(kd * e_end)
    n2 = 2 * t
    g = _bmm_nt(_bf(jnp.concatenate([ks, rs], axis=1)), _bf(jnp.concatenate([bs, kds], axis=1)))
    a_ab = jnp.where(mk['strict'], g[:, :n2, :n2], 0.0)
    a_ak = jnp.where(mk['strict'], g[:, :n2, n2:], 0.0)
    a_rb = jnp.where(mk['incl'], g[:, n2:, :n2], 0.0)
    a_rk = jnp.where(mk['incl'], g[:, n2:, n2:], 0.0)
    m = _unit_tri_inverse(a_ab, mk)
    vsb = _bf(vs)
    sol = _bmm3(m, jnp.concatenate([ks, _bmm(_bf(a_ak), vsb)], axis=2))
    wy = jnp.concatenate([rs, _bmm(_bf(a_rk), vsb)], axis=2) - _bmm(_bf(a_rb), _bf(sol))
    wy = wy[:, :t] + wy[:, t:]
    pg = _bmm_tn(_bf(sol), _bf(bps))
    gam = _bmm_tn(vsb, _bf(kps)) - pg[:, RWKV_PAIR:]
    y = _bmm_nt(_bf(wy[:, :, :RWKV_PAIR]), _bf(s)) + wy[:, :, RWKV_PAIR:]
    s_new = s * jnp.exp(c_end) - _bmm(_bf(s), _bf(pg[:, :RWKV_PAIR])) + gam
    return y, s_new


def _rwkv_chunk_kernel(*refs, nbt, has_init):
    del nbt
    it = iter(refs)
    ins = [tuple(next(it) for _ in range(6)) for _ in range(2)]
    s0_ref = next(it) if has_init else None
    yf_ref, yb_ref, fin_ref, st_ref = (next(it) for _ in range(4))
    c = pl.program_id(2)

    @pl.when(c == 0)
    def _():
        for d in range(2):
            st_ref[d] = s0_ref[:, d, 0] if has_init else jnp.zeros(st_ref.shape[1:], F32)

    t = yf_ref.shape[1]
    for d, y_ref in enumerate((yf_ref, yb_ref)):
        mk = _rwkv_masks(t, bwd=(d == 1))
        y, s_new = _rwkv_unit(*(ref[...] for ref in ins[d]), st_ref[d], mk, bwd=(d == 1))
        y_ref[...] = y
        st_ref[d] = s_new
        fin_ref[:, d, 0] = s_new


def _rwkv_scan(shared, fdir, bdir, batch, seq, s0=None):
    nbt = min(batch, 4)
    t = RWKV_CHUNK
    n = seq // t
    npair = W_D // RWKV_PAIR
    shp = (batch, seq, W_D)
    fspec = pl.BlockSpec((nbt, t, RWKV_PAIR), lambda g, q, c: (g, c, q))
    bspec = pl.BlockSpec((nbt, t, RWKV_PAIR), lambda g, q, c: (g, n - 1 - c, q))
    sspec = pl.BlockSpec((nbt, 2, 1, RWKV_PAIR, RWKV_PAIR), lambda g, q, c: (g, 0, q, 0, 0))
    args = [a.reshape(shp) for a in tuple(shared) + tuple(fdir) + tuple(shared) + tuple(bdir)]
    specs = [fspec] * 6 + [bspec] * 6
    if s0 is not None:
        args.append(s0)
        specs.append(sspec)
    yf, yb, fin = pl.pallas_call(
        functools.partial(_rwkv_chunk_kernel, nbt=nbt, has_init=s0 is not None),
        out_shape=(jax.ShapeDtypeStruct(shp, F32), jax.ShapeDtypeStruct(shp, F32),
                   jax.ShapeDtypeStruct((batch, 2, npair, RWKV_PAIR, RWKV_PAIR), F32)),
        grid=(batch // nbt, npair, n), in_specs=specs, out_specs=(fspec, bspec, sspec),
        scratch_shapes=[pltpu.VMEM((2, nbt, RWKV_PAIR, RWKV_PAIR), F32)],
        compiler_params=_cparams(("parallel", "parallel", "arbitrary")), name="rwkv_scan",
    )(*args)
    return yf.reshape(batch * seq, W_D), yb.reshape(batch * seq, W_D), fin


def _block_diag(w):
    n, a, b = w.shape
    return jnp.einsum('nab,nm->namb', w, jnp.eye(n, dtype=w.dtype)).reshape(n * a, n * b)


def _rope_tables(seq, dk):
    t = jnp.arange(seq)
    row = (t // GRID_W).astype(F32)
    col = (t % GRID_W).astype(F32)
    nf = dk // 4
    freqs = ROPE_BASE ** (-jnp.arange(nf, dtype=F32) / nf)
    ar, ac = row[:, None] * freqs, col[:, None] * freqs
    cos = jnp.concatenate([jnp.cos(ar), jnp.cos(ar), jnp.cos(ac), jnp.cos(ac)], axis=1)
    sin = jnp.concatenate([-jnp.sin(ar), jnp.sin(ar), -jnp.sin(ac), jnp.sin(ac)], axis=1)
    return cos, sin


def _trunk(x3, cond_base, per_batch_cond, is_latent, states, mods, p):
    batch, seq, d = x3.shape
    tiles = _Tiles(batch, seq, cond_base, per_batch_cond)
    x = x3.reshape(batch * seq, d)
    s_ret, s_lru, s_s5, s_rwkv = states
    bp = max(batch, SUBLANES)

    m0, g0 = mods[0], p['norm_g'][0]
    x = _ffn(tiles, x, m0, g0, p['f1a'][0], p['f1b'][0], first=True)
    qkvg, gx = _proj_even(tiles, x, m0, g0, p['l0_w_in'])
    rope = _rope_tables(seq, DK_A) if is_latent else None
    y_a, new_ret = _retention(qkvg, p['ret_dec'], batch, seq, s0=s_ret, rope_tabs=rope)
    af, bf, ab, bb = _lru_prep(tiles, gx, p['conv_wb'], p['lru_wg'], p['lru_bg'], p['lru_lam'])
    hf, hb, new_lru = _lru_scan(af, bf, ab, bb, batch, seq, h0=s_lru)
    x = _ffn(tiles, x, m0, g0, p['f2a'][0], p['f2b'][0], first=False, mix="even",
             mix_args=(y_a, gx, hf, hb), wo=p['l0_w_out'])

    m1, g1 = mods[1], p['norm_g'][1]
    x = _ffn(tiles, x, m1, g1, p['f1a'][1], p['f1b'][1], first=True)
    u_tm, rkvx = _proj_odd(tiles, x, m1, g1, p['l1_w_in'], bp)
    u_tm = u_tm.reshape(seq * bp, W_C)
    s5_0 = None
    if s_s5 is not None:
        s5_0 = jnp.transpose(s_s5.reshape(batch, 2, S5_STATE), (1, 0, 2))
        s5_0 = jnp.pad(s5_0, ((0, 0), (0, bp - batch), (0, 0)))
    yf5, yb5, fin5 = _s5_scan(u_tm, p['s5_bbar'], p['s5_cc'], p['s5_abar'], bp, seq, s0=s5_0)
    yc_tm = _s5_post(yf5, yb5, u_tm, p['s5_d'], p['glu_w'], p['glu_b']).reshape(seq, bp * W_C)
    new_s5 = jnp.transpose(fin5[:, :batch], (1, 0, 2)).reshape(batch, 2, 2, G_C, N_C)
    outs = _rwkv_prep(tiles, rkvx, p['rw_vec'], p['rw_w1'], p['rw_w2'], p['rw_a1'], p['rw_a2'],
                      p['rw_g1'], p['rw_g2'], p['ones_hd'])
    v, kk, gg, bonus, r = outs[:5]
    npair = H_D // 2
    eye2 = jnp.eye(2, dtype=F32)
    rw0 = None
    if s_rwkv is not None:
        rw0 = jnp.einsum('bdqjvk,ji->bdqjvik', s_rwkv.reshape(batch, 2, npair, 2, HD_D, HD_D), eye2)
        rw0 = rw0.reshape(batch, 2, npair, RWKV_PAIR, RWKV_PAIR)
    yfr, ybr, finr = _rwkv_scan((r, v, kk), outs[5:8], outs[8:11], batch, seq, s0=rw0)
    finr = finr.reshape(batch, 2, npair, 2, HD_D, 2, HD_D)
    new_rwkv = jnp.stack([finr[:, :, :, 0, :, 0, :], finr[:, :, :, 1, :, 1, :]], axis=3)
    new_rwkv = new_rwkv.reshape(batch, 2, H_D, HD_D, HD_D)
    x = _ffn(tiles, x, m1, g1, p['f2a'][1], p['f2b'][1], first=False, mix="odd",
             mix_args=(yc_tm, yfr, ybr, bonus, gg, p['avg_hd'], p['ln_wb']), wo=p['l1_w_out'],
             final_g=p['final_norm'])
    return x.reshape(batch, seq, d), (new_ret, new_lru, new_s5, new_rwkv)


def kernel(x_prompt, x_sample, state_l0_ret, state_l0_lru, state_l1_s5, state_l1_rwkv, c, c_ctx, mod_w, mod_b, norm_g, ffn1_w13, ffn1_w2, ffn2_w13, ffn2_w2, final_norm, l0_w_in, l0_w_out, l0_ret_decay, l0_conv_w, l0_conv_b, l0_lru_lam, l0_lru_wa, l0_lru_ba, l0_lru_wx, l0_lru_bx, l1_w_in, l1_w_out, l1_s5_a_re, l1_s5_a_im, l1_s5_log_dt, l1_s5_b_re, l1_s5_b_im, l1_s5_c_re, l1_s5_c_im, l1_s5_d, l1_glu_w, l1_glu_b, l1_rw_mu, l1_rw_w0, l1_rw_w1, l1_rw_w2, l1_rw_a0, l1_rw_a1, l1_rw_a2, l1_rw_g1, l1_rw_g2, l1_rw_kk, l1_rw_ka, l1_rw_rk, l1_ln_w, l1_ln_b):
    d = x_prompt.shape[-1]
    nlat = c.shape[0]
    cond8 = jnp.concatenate([c_ctx[None, :], c, jnp.zeros((SUBLANES - 1 - nlat, d), F32)], axis=0)
    mods = _modulation(cond8, mod_w, mod_b)

    gn = G_C * N_C
    eye_g = jnp.eye(G_C, dtype=F32)
    bre = jnp.einsum('dgns,gh->dgshn', l1_s5_b_re, eye_g).reshape(2, W_C, gn)
    bim = jnp.einsum('dgns,gh->dgshn', l1_s5_b_im, eye_g).reshape(2, W_C, gn)
    cre = jnp.einsum('dgsn,gh->dgnhs', l1_s5_c_re, eye_g).reshape(2, gn, W_C)
    cim = jnp.einsum('dgsn,gh->dgnhs', l1_s5_c_im, eye_g).reshape(2, gn, W_C)
    prm = jnp.stack([l1_s5_a_re.reshape(2, gn), l1_s5_a_im.reshape(2, gn),
                     jnp.repeat(l1_s5_log_dt, N_C, axis=1)], axis=1)
    s5_bbar, s5_abar = _s5_params(prm, bre, bim)
    head_id = jnp.arange(W_D) // HD_D
    ones_hd = (head_id[:, None] == head_id[None, :]).astype(BF16)
    p = dict(
        norm_g=norm_g, final_norm=final_norm.reshape(1, d),
        f1a=ffn1_w13.astype(BF16), f1b=ffn1_w2.astype(BF16),
        f2a=ffn2_w13.astype(BF16), f2b=ffn2_w2.astype(BF16),
        l0_w_in=l0_w_in.astype(BF16), l0_w_out=l0_w_out.astype(BF16),
        l1_w_in=l1_w_in.astype(BF16), l1_w_out=l1_w_out.astype(BF16),
        ret_dec=jnp.broadcast_to(l0_ret_decay.astype(F32)[:, :, None, None], (2, H_A, 1, DK_A)),
        conv_wb=jnp.concatenate([l0_conv_w, l0_conv_b[None, :]], axis=0),
        lru_wg=jnp.concatenate([_block_diag(l0_lru_wa[0]), _block_diag(l0_lru_wx[0]),
                                _block_diag(l0_lru_wa[1]), _block_diag(l0_lru_wx[1])], axis=1).astype(BF16),
        lru_bg=jnp.concatenate([l0_lru_ba[0], l0_lru_bx[0], l0_lru_ba[1], l0_lru_bx[1]])[None, :],
        lru_lam=l0_lru_lam,
        s5_bbar=s5_bbar, s5_abar=s5_abar,
        s5_cc=jnp.concatenate([cre, -cim], axis=1).astype(BF16),
        s5_d=l1_s5_d.reshape(1, W_C), glu_w=l1_glu_w.astype(BF16), glu_b=l1_glu_b.reshape(1, W_C),
        rw_vec=jnp.concatenate([l1_rw_mu, l1_rw_kk[None], l1_rw_ka[None], l1_rw_rk[None],
                                l1_rw_w0, l1_rw_a0, jnp.zeros((3, W_D), F32)], axis=0),
        rw_w1=l1_rw_w1.astype(BF16), rw_w2=l1_rw_w2.astype(BF16),
        rw_a1=l1_rw_a1.astype(BF16), rw_a2=l1_rw_a2.astype(BF16),
        rw_g1=l1_rw_g1.astype(BF16), rw_g2=l1_rw_g2.astype(BF16),
        ones_hd=ones_hd, avg_hd=(ones_hd.astype(F32) / HD_D).astype(BF16),
        ln_wb=jnp.stack([l1_ln_w, l1_ln_b], axis=0),
    )
    y_prompt, ctx_states = _trunk(x_prompt, 0, False, False, (None, None, None, None), mods, p)
    y_sample, _ = _trunk(x_sample, 1, True, True,
                         (state_l0_ret, state_l0_lru, state_l1_s5, state_l1_rwkv), mods, p)
    return (y_prompt, y_sample) + ctx_states
```

```python
import functools
import math

import jax
import jax.numpy as jnp
from jax import lax
from jax.experimental import pallas as pl
from jax.experimental.pallas import tpu as pltpu

F32 = jnp.float32
BF16 = jnp.bfloat16

N_MOD = 9
NORM_EPS = 1e-6
HEAD_NORM_EPS = 1e-5
RWKV_LN_EPS = 64e-5
ROPE_BASE = 10000.0
GRID_W = 64
H_A = 4
DK_A = 128
W_A = 512
W_B = 512
NB_B = 8
LRU_C = 8.0
W_C = 512
GS_C = 16
G_C = 32
N_C = 64
W_D = 512
HD_D = 64
H_D = 8
S5_STATE = 2 * G_C * N_C
RWKV_CHUNK = 64
RWKV_PAIR = 2 * HD_D

SUBLANES = 8
LANES = 128
VMEM_LIMIT = 56 * 1024 * 1024
ROW_TILE = 512


def _cparams(sem):
    return pltpu.CompilerParams(dimension_semantics=sem, vmem_limit_bytes=VMEM_LIMIT)


def _dot(a, b):
    return jnp.dot(a, b, preferred_element_type=F32)


def _dot_nt(a, b):
    return lax.dot_general(a, b, (((1,), (1,)), ((), ())), preferred_element_type=F32)


def _dot_tn(a, b):
    return lax.dot_general(a, b, (((0,), (0,)), ((), ())), preferred_element_type=F32)


def _bdot(a, b):
    return _dot(a.astype(BF16), b.astype(BF16))


def _sigmoid(x):
    return 1.0 / (1.0 + jnp.exp(-x))


def _silu(x):
    return x * _sigmoid(x)


def _softplus(x):
    return jnp.maximum(x, 0.0) + jnp.log1p(jnp.exp(-jnp.abs(x)))


def _log_sigmoid(x):
    return -_softplus(-x)


def _gelu_tanh(x):
    return 0.5 * x * (1.0 + jnp.tanh(math.sqrt(2.0 / math.pi) * (x + 0.044715 * (x * x * x))))


def _rms(x):
    return x * lax.rsqrt(jnp.mean(x * x, axis=-1, keepdims=True) + NORM_EPS)


def _rms_mod(x, g, scale, shift):
    return (_rms(x) * g) * (1.0 + scale) + shift


def _const_spec(shape):
    nd = len(shape)
    return pl.BlockSpec(shape, lambda *_: (0,) * nd, pipeline_mode=pl.Buffered(1))


def _mod_kernel(c_ref, w_ref, b_ref, o_ref):
    o_ref[0] = _bdot(_silu(c_ref[...]), w_ref[0]) + b_ref[0]


def _modulation(cond8, mod_w, mod_b):
    depth, d, nd = mod_w.shape
    return pl.pallas_call(
        _mod_kernel,
        out_shape=jax.ShapeDtypeStruct((depth, SUBLANES, nd), F32),
        grid=(depth, nd // d),
        in_specs=[pl.BlockSpec((SUBLANES, d), lambda l, j: (0, 0)),
                  pl.BlockSpec((1, d, d), lambda l, j: (l, 0, j)),
                  pl.BlockSpec((1, 1, d), lambda l, j: (l, 0, j))],
        out_specs=pl.BlockSpec((1, SUBLANES, d), lambda l, j: (l, 0, j)),
        compiler_params=_cparams(("arbitrary", "arbitrary")),
        name="modulation",
    )(cond8, mod_w, mod_b.reshape(depth, 1, nd)).reshape(depth, SUBLANES, N_MOD, d)


class _Tiles:
    def __init__(self, batch, seq, cond_base, per_batch_cond, row_tile=ROW_TILE):
        self.batch, self.seq = batch, seq
        self.tm = min(row_tile, batch * seq)
        if seq >= self.tm:
            self.nb, self.tl = 1, self.tm
        else:
            self.nb, self.tl = self.tm // seq, seq
        assert seq % self.tl == 0 and batch % self.nb == 0
        self.tiles_per_seq = seq // self.tl
        self.n_tiles = batch * seq // self.tm
        self.cond_base = cond_base
        self.tiles_per_cond = self.tiles_per_seq if per_batch_cond else self.n_tiles

    def cond_row(self, i):
        return self.cond_base + i // self.tiles_per_cond

    def mod_spec(self, d):
        return pl.BlockSpec((1, N_MOD, d), lambda i: (self.cond_row(i), 0, 0))

    def row_spec(self, width, col=0):
        return pl.BlockSpec((self.tm, width), lambda i: (i, col))

    def tm_spec(self, width):
        return pl.BlockSpec((self.tl, self.nb * width),
                            lambda i: (i % self.tiles_per_seq, i // self.tiles_per_seq))


def _from_time_major(blk, nb, width):
    if nb == 1:
        return blk
    return jnp.concatenate([blk[:, j * width:(j + 1) * width] for j in range(nb)], axis=0)


def _ffn_kernel(*refs, mix, first, final, nb, ff_chunks):
    it = iter(refs)
    x_ref, m_ref, g_ref, w13_ref, w2_ref = (next(it) for _ in range(5))
    x = x_ref[...]
    m = m_ref[0]
    if mix == "even":
        ya_ref, gb_ref, hf_ref, hb_ref, wo_ref = (next(it) for _ in range(5))
        ya = ya_ref[...]
        yb = _gelu_tanh(gb_ref[...]) * (hf_ref[...] + hb_ref[...])
    elif mix == "odd":
        yc_ref, yf_ref, yb_ref, bon_ref, gg_ref, avg_ref, ln_ref, wo_ref = (next(it) for _ in range(8))
        ya = _from_time_major(yc_ref[...], nb, W_C)
        ys = jnp.concatenate([yf_ref[q] + yb_ref[q] for q in range(yf_ref.shape[0])], axis=1)
        avg = avg_ref[...]
        oc = ys - _bdot(ys, avg)
        yd = oc * lax.rsqrt(_bdot(oc * oc, avg) + RWKV_LN_EPS)
        yd = yd * ln_ref[0:1] + ln_ref[1:2]
        yb = (yd + bon_ref[...]) * gg_ref[...]
    if mix is not None:
        wa = ya.shape[1]
        y = _dot(ya.astype(BF16), wo_ref[0:wa, :]) + _dot(yb.astype(BF16), wo_ref[wa:, :])
        x = x + m[5:6] * y
    fin_ref = next(it) if final else None
    o_ref = next(it)
    r0 = 0 if first else 6
    gi = 0 if first else 2
    h = _rms_mod(x, g_ref[gi:gi + 1], m[r0 + 1:r0 + 2], m[r0:r0 + 1]).astype(BF16)
    dff = w2_ref.shape[0]
    fc = dff // ff_chunks
    acc = jnp.zeros(x.shape, F32)
    for j in range(ff_chunks):
        gate = _dot(h, w13_ref[:, j * fc:(j + 1) * fc])
        up = _dot(h, w13_ref[:, dff + j * fc:dff + (j + 1) * fc])
        acc = acc + _dot((_silu(gate) * up).astype(BF16), w2_ref[j * fc:(j + 1) * fc, :])
    x = x + 0.5 * m[r0 + 2:r0 + 3] * acc
    if final:
        x = _rms(x) * fin_ref[...]
    o_ref[...] = x


def _ffn(tiles, x, mods, norm_g, w13, w2, *, first, mix=None, mix_args=(), wo=None, final_g=None):
    n, d = x.shape
    dff = w2.shape[0]
    args = [x, mods, norm_g, w13, w2]
    specs = [tiles.row_spec(d), tiles.mod_spec(d), _const_spec(norm_g.shape),
             _const_spec(w13.shape), _const_spec(w2.shape)]
    if mix == "even":
        ya, gx, hf, hb = mix_args
        args += [ya, gx, hf, hb, wo]
        specs += [tiles.row_spec(W_A), tiles.row_spec(W_B, 0), tiles.row_spec(W_B), tiles.row_spec(W_B),
                  _const_spec(wo.shape)]
    elif mix == "odd":
        yc_tm, yf, yb, bon, gg, avg, ln = mix_args
        args += [yc_tm, yf, yb, bon, gg, avg, ln, wo]
        pspec = pl.BlockSpec((yf.shape[0], tiles.tm, yf.shape[2]), lambda i: (0, i, 0))
        specs += [tiles.tm_spec(W_C), pspec, pspec] + [tiles.row_spec(W_D)] * 2 + [
            _const_spec(avg.shape), _const_spec(ln.shape), _const_spec(wo.shape)]
    if final_g is not None:
        args.append(final_g)
        specs.append(_const_spec(final_g.shape))
    kern = functools.partial(_ffn_kernel, mix=mix, first=first, final=final_g is not None,
                             nb=tiles.nb, ff_chunks=2 if dff % (2 * LANES) == 0 else 1)
    return pl.pallas_call(
        kern, out_shape=jax.ShapeDtypeStruct((n, d), F32), grid=(tiles.n_tiles,),
        in_specs=specs, out_specs=tiles.row_spec(d),
        compiler_params=_cparams(("parallel",)), name="ffn",
    )(*args)


def _proj_even_kernel(x_ref, m_ref, g_ref, w_ref, qkvg_ref, gx_ref):
    m = m_ref[0]
    h = _rms_mod(x_ref[...], g_ref[1:2], m[4:5], m[3:4]).astype(BF16)
    p = _dot(h, w_ref[...])
    nq = qkvg_ref.shape[1]
    qkvg_ref[...] = p[:, :nq]
    gx_ref[...] = p[:, nq:]


def _proj_odd_kernel(x_ref, m_ref, g_ref, w_ref, u_ref, rkvx_ref, *, nb, tl):
    m = m_ref[0]
    h = _rms_mod(x_ref[...], g_ref[1:2], m[4:5], m[3:4]).astype(BF16)
    p = _dot(h, w_ref[...])
    for j in range(nb):
        u_ref[:, j * W_C:(j + 1) * W_C] = p[j * tl:(j + 1) * tl, :W_C]
    rkvx_ref[...] = p[:, W_C:]


def _proj_even(tiles, x, mods, norm_g, w_in):
    n, d = x.shape
    return pl.pallas_call(
        _proj_even_kernel,
        out_shape=(jax.ShapeDtypeStruct((n, 4 * W_A), F32), jax.ShapeDtypeStruct((n, 2 * W_B), F32)),
        grid=(tiles.n_tiles,),
        in_specs=[tiles.row_spec(d), tiles.mod_spec(d), _const_spec(norm_g.shape), _const_spec(w_in.shape)],
        out_specs=(tiles.row_spec(4 * W_A), tiles.row_spec(2 * W_B)),
        compiler_params=_cparams(("parallel",)), name="proj_even",
    )(x, mods, norm_g, w_in)


def _proj_odd(tiles, x, mods, norm_g, w_in):
    n, d = x.shape
    kern = functools.partial(_proj_odd_kernel, nb=tiles.nb, tl=tiles.tl)
    return pl.pallas_call(
        kern,
        out_shape=(jax.ShapeDtypeStruct((tiles.seq, tiles.batch * W_C), F32),
                   jax.ShapeDtypeStruct((n, 4 * W_D), F32)),
        grid=(tiles.n_tiles,),
        in_specs=[tiles.row_spec(d), tiles.mod_spec(d), _const_spec(norm_g.shape), _const_spec(w_in.shape)],
        out_specs=(tiles.tm_spec(W_C), tiles.row_spec(4 * W_D)),
        compiler_params=_cparams(("parallel",)), name="proj_odd",
    )(x, mods, norm_g, w_in)


def _ret_kernel(*refs, seq, chunk, rope, has_init):
    it = iter(refs)
    q_ref, k_ref, v_ref, g_ref, dec_ref = (next(it) for _ in range(5))
    cos_ref, sin_ref = (next(it), next(it)) if rope else (None, None)
    s0_ref = next(it) if has_init else None
    y_ref, sf_ref, o_scr = next(it), next(it), next(it)
    n = seq // chunk
    dk = q_ref.shape[1]
    lgf = _log_sigmoid(dec_ref[0, 0])
    lgb = _log_sigmoid(dec_ref[1, 0])
    ri = lax.broadcasted_iota(jnp.int32, (chunk, dk), 0).astype(F32)
    qf_t = jnp.exp(lgf * (ri + 1.0))
    qb_t = jnp.exp(lgb * (chunk - ri))
    kf_t = jnp.exp(lgf * (chunk - 1.0 - ri))
    kb_t = jnp.exp(lgb * ri)
    reps = chunk // dk
    lgf_c = jnp.concatenate([lgf] * reps, axis=1) if reps > 1 else lgf[:, :chunk]
    lgb_c = jnp.concatenate([lgb] * reps, axis=1) if reps > 1 else lgb[:, :chunk]
    rel = (lax.broadcasted_iota(jnp.int32, (chunk, chunk), 0)
           - lax.broadcasted_iota(jnp.int32, (chunk, chunk), 1)).astype(F32)
    dmask = (jnp.where(rel >= 0, jnp.exp(lgf_c * jnp.maximum(rel, 0.0)), 0.0)
             + jnp.where(rel <= 0, jnp.exp(lgb_c * jnp.maximum(-rel, 0.0)), 0.0))
    gcf = jnp.exp(lgf * float(chunk))
    gcb = jnp.exp(lgb * float(chunk))
    lane = lax.broadcasted_iota(jnp.int32, (chunk, dk), 1)
    first_half = (lane & (dk // 2 - 1)) < dk // 4

    def load_qkv(c):
        rows = pl.ds(pl.multiple_of(c * chunk, chunk), chunk)
        q, k, v = q_ref[rows, :], k_ref[rows, :], v_ref[rows, :]
        if rope:
            cs, sn = cos_ref[rows, :], sin_ref[rows, :]

            def rot(x):
                partner = jnp.where(first_half, pltpu.roll(x, dk - dk // 4, axis=1),
                                    pltpu.roll(x, dk // 4, axis=1))
                return x * cs + partner * sn

            q, k = rot(q), rot(k)
        return rows, q * (dk ** -0.5), k, v

    def fwd(c, s):
        rows, q, k, v = load_qkv(c)
        scores = _dot_nt(q.astype(BF16), k.astype(BF16)) * dmask
        inner = _bdot(scores, v)
        o_scr[rows, :] = inner + _bdot(q * qf_t, s)
        return s * gcf + _dot_tn((k * kf_t).astype(BF16), v.astype(BF16))

    def bwd(j, s):
        rows, q, k, v = load_qkv(n - 1 - j)
        o = o_scr[rows, :] + _bdot(q * qb_t, s)
        oc = o - jnp.mean(o, axis=-1, keepdims=True)
        on = oc * lax.rsqrt(jnp.mean(oc * oc, axis=-1, keepdims=True) + HEAD_NORM_EPS)
        y_ref[rows, :] = on * _silu(g_ref[rows, :])
        return s * gcb + _dot_tn((k * kb_t).astype(BF16), v.astype(BF16))

    zero = jnp.zeros((dk, v_ref.shape[1]), F32)
    sf = lax.fori_loop(0, n, fwd, s0_ref[0, 0, 0] if has_init else zero)
    sb = lax.fori_loop(0, n, bwd, s0_ref[0, 1, 0] if has_init else zero)
    sf_ref[0, 0, 0] = sf
    sf_ref[0, 1, 0] = sb


def _retention(qkvg, dec, batch, seq, s0=None, rope_tabs=None):
    dk = DK_A
    chunk = min(seq, 256)
    args = [qkvg] * 4 + [dec]
    specs = [pl.BlockSpec((seq, dk), lambda b, h, j=j: (b, j * H_A + h)) for j in range(4)]
    specs.append(pl.BlockSpec((2, 1, 1, dk), lambda b, h: (0, h, 0, 0)))
    if rope_tabs is not None:
        args += list(rope_tabs)
        specs += [_const_spec((seq, dk))] * 2
    if s0 is not None:
        args.append(s0)
        specs.append(pl.BlockSpec((1, 2, 1, dk, dk), lambda b, h: (b, 0, h, 0, 0)))
    kern = functools.partial(_ret_kernel, seq=seq, chunk=chunk, rope=rope_tabs is not None,
                             has_init=s0 is not None)
    return pl.pallas_call(
        kern,
        out_shape=(jax.ShapeDtypeStruct((batch * seq, W_A), F32),
                   jax.ShapeDtypeStruct((batch, 2, H_A, dk, dk), F32)),
        grid=(batch, H_A),
        in_specs=specs,
        out_specs=(pl.BlockSpec((seq, dk), lambda b, h: (b, h)),
                   pl.BlockSpec((1, 2, 1, dk, dk), lambda b, h: (b, 0, h, 0, 0))),
        scratch_shapes=[pltpu.VMEM((seq, dk), F32)],
        compiler_params=_cparams(("parallel", "arbitrary")), name="retention",
    )(*args)


def _halo_specs(tiles, width, col, n_rows):
    tm = tiles.tm
    nblk = n_rows // SUBLANES
    per = tm // SUBLANES
    prev = pl.BlockSpec((SUBLANES, width), lambda i: (jnp.maximum(i * per - 1, 0), col))
    nxt = pl.BlockSpec((SUBLANES, width), lambda i: (jnp.minimum((i + 1) * per, nblk - 1), col))
    return prev, nxt


def _seq_shift(ext, s, pos, seq, tm):
    sh = pltpu.roll(ext, s % ext.shape[0], axis=0)[SUBLANES:SUBLANES + tm]
    ok = (pos >= s) if s > 0 else (pos < seq + s)
    return jnp.where(ok, sh, 0.0)


def _seq_pos(shape, tm, seq):
    return (pl.program_id(0) * tm + lax.broadcasted_iota(jnp.int32, shape, 0)) & (seq - 1)


def _lru_prep_kernel(x_ref, xp_ref, xn_ref, cw_ref, wg_ref, bg_ref, lam_ref,
                     af_ref, bf_ref, ab_ref, bb_ref, *, seq):
    x = x_ref[...]
    tm = x.shape[0]
    ext = jnp.concatenate([xp_ref[...], x, xn_ref[...]], axis=0)
    pos = _seq_pos(x.shape, tm, seq)
    xc = (cw_ref[4:5] + cw_ref[0:1] * _seq_shift(ext, 2, pos, seq, tm)
          + cw_ref[1:2] * _seq_shift(ext, 1, pos, seq, tm) + cw_ref[2:3] * x
          + cw_ref[3:4] * _seq_shift(ext, -1, pos, seq, tm))
    gates = _dot(xc.astype(BF16), wg_ref[...]) + bg_ref[...]
    w = x.shape[1]
    for d, (a_ref, b_ref) in enumerate(((af_ref, bf_ref), (ab_ref, bb_ref))):
        r = _sigmoid(gates[:, (2 * d) * w:(2 * d + 1) * w])
        i = _sigmoid(gates[:, (2 * d + 1) * w:(2 * d + 2) * w])
        log_a = LRU_C * r * _log_sigmoid(lam_ref[d:d + 1])
        a = jnp.exp(log_a)
        a_ref[...] = a
        b_ref[...] = jnp.sqrt(1.0 - a * a) * (i * xc)


def _lru_prep(tiles, gx, conv_wb, wg, bg, lam):
    n = gx.shape[0]
    prev, nxt = _halo_specs(tiles, W_B, 1, n)
    out = jax.ShapeDtypeStruct((n, W_B), F32)
    return pl.pallas_call(
        functools.partial(_lru_prep_kernel, seq=tiles.seq),
        out_shape=(out,) * 4, grid=(tiles.n_tiles,),
        in_specs=[tiles.row_spec(W_B, 1), prev, nxt, _const_spec(conv_wb.shape), _const_spec(wg.shape),
                  _const_spec(bg.shape), _const_spec(lam.shape)],
        out_specs=(tiles.row_spec(W_B),) * 4,
        compiler_params=_cparams(("parallel",)), name="lru_prep",
    )(gx, gx, gx, conv_wb, wg, bg, lam)


def _lru_scan_kernel(*refs, nbt, tc, has_init):
    it = iter(refs)
    af_ref, bf_ref, ab_ref, bb_ref = (next(it) for _ in range(4))
    h0_ref = next(it) if has_init else None
    hf_ref, hb_ref, fin_ref, st_ref = (next(it) for _ in range(4))
    c = pl.program_id(1)

    @pl.when(c == 0)
    def _():
        st_ref[...] = h0_ref[...] if has_init else jnp.zeros(st_ref.shape, F32)

    def body(t, carry):
        tb = tc - 1 - t
        new = []
        for j in range(nbt):
            hf, hb = carry[2 * j], carry[2 * j + 1]
            hf = af_ref[j, pl.ds(t, 1), :] * hf + bf_ref[j, pl.ds(t, 1), :]
            hb = ab_ref[j, pl.ds(tb, 1), :] * hb + bb_ref[j, pl.ds(tb, 1), :]
            hf_ref[j, pl.ds(t, 1), :] = hf
            hb_ref[j, pl.ds(tb, 1), :] = hb
            new += [hf, hb]
        return tuple(new)

    init = tuple(st_ref[j, d:d + 1, :] for j in range(nbt) for d in range(2))
    out = lax.fori_loop(0, tc, body, init)
    for j in range(nbt):
        for d in range(2):
            st_ref[j, d:d + 1, :] = out[2 * j + d]
    fin_ref[...] = st_ref[...]


def _lru_scan(af, bf, ab, bb, batch, seq, h0=None):
    nbt = min(batch, 4)
    tc = min(seq, 256)
    n = seq // tc
    shp = (batch, seq, W_B)
    fspec = pl.BlockSpec((nbt, tc, W_B), lambda g, c: (g, c, 0))
    bspec = pl.BlockSpec((nbt, tc, W_B), lambda g, c: (g, n - 1 - c, 0))
    sspec = pl.BlockSpec((nbt, 2, W_B), lambda g, c: (g, 0, 0))
    args = [af.reshape(shp), bf.reshape(shp), ab.reshape(shp), bb.reshape(shp)]
    specs = [fspec, fspec, bspec, bspec]
    if h0 is not None:
        args.append(h0)
        specs.append(sspec)
    hf, hb, fin = pl.pallas_call(
        functools.partial(_lru_scan_kernel, nbt=nbt, tc=tc, has_init=h0 is not None),
        out_shape=(jax.ShapeDtypeStruct(shp, F32), jax.ShapeDtypeStruct(shp, F32),
                   jax.ShapeDtypeStruct((batch, 2, W_B), F32)),
        grid=(batch // nbt, n), in_specs=specs, out_specs=(fspec, bspec, sspec),
        scratch_shapes=[pltpu.VMEM((nbt, 2, W_B), F32)],
        compiler_params=_cparams(("parallel", "arbitrary")), name="lru_scan",
    )(*args)
    return hf.reshape(batch * seq, W_B), hb.reshape(batch * seq, W_B), fin


S5_BLOCKS = 2


def _s5_param_kernel(prm_ref, bre_ref, bim_ref, bbar_ref, abar_ref):
    a_re, a_im, ldt = prm_ref[0, 0:1], prm_ref[0, 1:2], prm_ref[0, 2:3]
    dt = jnp.exp(ldt)
    mag = jnp.exp(a_re * dt)
    abr = mag * jnp.cos(a_im * dt)
    abi = mag * jnp.sin(a_im * dt)
    den = a_re * a_re + a_im * a_im
    fr = ((abr - 1.0) * a_re + abi * a_im) / den
    fi = (abi * a_re - (abr - 1.0) * a_im) / den
    bre, bim = bre_ref[0, 0], bim_ref[0, 0]
    ns = bre.shape[1]
    bbar_ref[0, 0, :, 0:ns] = (fr * bre - fi * bim).astype(BF16)
    bbar_ref[0, 0, :, ns:] = (fr * bim + fi * bre).astype(BF16)
    abar_ref[0, 0:1] = abr
    abar_ref[0, 1:2] = abi


def _s5_params(prm, bre, bim):
    gn = prm.shape[2]
    _, nblk, cb, ns = bre.shape
    return pl.pallas_call(
        _s5_param_kernel,
        out_shape=(jax.ShapeDtypeStruct((2, nblk, cb, 2 * ns), BF16), jax.ShapeDtypeStruct((2, 2, gn), F32)),
        grid=(2, nblk),
        in_specs=[pl.BlockSpec((1, 3, ns), lambda d, j: (d, 0, j)),
                  pl.BlockSpec((1, 1, cb, ns), lambda d, j: (d, j, 0, 0)),
                  pl.BlockSpec((1, 1, cb, ns), lambda d, j: (d, j, 0, 0))],
        out_specs=(pl.BlockSpec((1, 1, cb, 2 * ns), lambda d, j: (d, j, 0, 0)),
                   pl.BlockSpec((1, 2, ns), lambda d, j: (d, 0, j))),
        compiler_params=_cparams(("arbitrary", "arbitrary")), name="s5_params",
    )(prm, bre, bim)


def _s5_scan_kernel(*refs, bp, tc, cw, has_init):
    it = iter(refs)
    uf_ref, ub_ref, bbar_ref, cc_ref, abar_ref = (next(it) for _ in range(5))
    s0_ref = next(it) if has_init else None
    yf_ref, yb_ref, fin_ref, buf_f, buf_b, st_ref = (next(it) for _ in range(6))
    c = pl.program_id(0)
    gn = abar_ref.shape[2]
    nblk, cb = bbar_ref.shape[1], bbar_ref.shape[2]
    ns = gn // nblk
    tr = st_ref.shape[1]
    sub = tr // bp
    n_it = tc // sub

    @pl.when(c == 0)
    def _():
        st_ref[...] = s0_ref[...] if has_init else jnp.zeros(st_ref.shape, F32)

    for d, (u_ref, buf) in enumerate(((uf_ref, buf_f), (ub_ref, buf_b))):
        u = u_ref[...].astype(BF16)
        for j in range(nblk):
            bu = _dot(u[:, j * cb:(j + 1) * cb], bbar_ref[d, j])
            buf[:, j * ns:(j + 1) * ns] = bu[:, :ns]
            buf[:, gn + j * ns:gn + (j + 1) * ns] = bu[:, ns:]

    first_f = lax.broadcasted_iota(jnp.int32, (tr, cw), 0) < bp

    def cmul_add(ar, ai, hr, hi, xr, xi):
        return ar * hr - ai * hi + xr, ar * hi + ai * hr + xi

    for kb in range(gn // cw):
        re = slice(kb * cw, (kb + 1) * cw)
        im = slice(gn + kb * cw, gn + (kb + 1) * cw)
        coef = []
        for d in range(2):
            coef.append((jnp.broadcast_to(abar_ref[d, 0:1, re], (tr, cw)),
                         jnp.broadcast_to(abar_ref[d, 1:2, re], (tr, cw))))

        def body(i, carry, re=re, im=im, coef=coef):
            out = []
            for d, buf in enumerate((buf_f, buf_b)):
                hr, hi = carry[2 * d], carry[2 * d + 1]
                ar, ai = coef[d]
                tile = i if d == 0 else n_it - 1 - i
                rows = pl.ds(pl.multiple_of(tile * tr, tr), tr)
                xr, xi = buf[rows, re], buf[rows, im]
                r1, i1 = cmul_add(ar, ai, hr, hi, xr, xi)
                if sub == 1:
                    outr, outi, nr, ni = r1, i1, r1, i1
                else:
                    first = first_f if d == 0 else jnp.logical_not(first_f)
                    r2, i2 = cmul_add(ar, ai, pltpu.roll(r1, bp, axis=0), pltpu.roll(i1, bp, axis=0), xr, xi)
                    outr, outi = jnp.where(first, r1, r2), jnp.where(first, i1, i2)
                    nr = jnp.where(first, pltpu.roll(r2, bp, axis=0), r2)
                    ni = jnp.where(first, pltpu.roll(i2, bp, axis=0), i2)
                buf[rows, re] = outr
                buf[rows, im] = outi
                out += [nr, ni]
            return tuple(out)

        init = (st_ref[0, :, re], st_ref[0, :, im], st_ref[1, :, re], st_ref[1, :, im])
        fr_, fi_, br_, bi_ = lax.fori_loop(0, n_it, body, init)
        st_ref[0, :, re] = fr_
        st_ref[0, :, im] = fi_
        st_ref[1, :, re] = br_
        st_ref[1, :, im] = bi_

    for d, (y_ref, buf) in enumerate(((yf_ref, buf_f), (yb_ref, buf_b))):
        for j in range(nblk):
            hre = buf[:, j * ns:(j + 1) * ns].astype(BF16)
            him = buf[:, gn + j * ns:gn + (j + 1) * ns].astype(BF16)
            y_ref[:, j * cb:(j + 1) * cb] = _dot(hre, cc_ref[d, j, 0:ns, :]) + _dot(him, cc_ref[d, j, ns:, :])
    fin_ref[...] = st_ref[...]


def _s5_scan(u_tm, bbar, cc, abar, bp, seq, s0=None):
    assert bp % SUBLANES == 0 or 2 * bp == SUBLANES
    tr = max(bp, SUBLANES)
    rows = min(512, seq * bp)
    tc = rows // bp
    n = seq // tc
    cw = 128 if bp > SUBLANES else 256
    fspec = pl.BlockSpec((rows, W_C), lambda c: (c, 0))
    bspec = pl.BlockSpec((rows, W_C), lambda c: (n - 1 - c, 0))
    args = [u_tm, u_tm, bbar, cc, abar]
    specs = [fspec, bspec, _const_spec(bbar.shape), _const_spec(cc.shape), _const_spec(abar.shape)]
    if s0 is not None:
        args.append(s0)
        specs.append(_const_spec(s0.shape))
    return pl.pallas_call(
        functools.partial(_s5_scan_kernel, bp=bp, tc=tc, cw=cw, has_init=s0 is not None),
        out_shape=(jax.ShapeDtypeStruct(u_tm.shape, F32), jax.ShapeDtypeStruct(u_tm.shape, F32),
                   jax.ShapeDtypeStruct((2, tr, S5_STATE), F32)),
        grid=(n,), in_specs=specs,
        out_specs=(fspec, bspec, _const_spec((2, tr, S5_STATE))),
        scratch_shapes=[pltpu.VMEM((rows, S5_STATE), F32), pltpu.VMEM((rows, S5_STATE), F32),
                        pltpu.VMEM((2, tr, S5_STATE), F32)],
        compiler_params=_cparams(("arbitrary",)), name="s5_scan",
    )(*args)


def _s5_post_kernel(yf_ref, yb_ref, u_ref, d_ref, w_ref, b_ref, o_ref):
    ys = yf_ref[...] + yb_ref[...] + d_ref[...] * u_ref[...]
    z = _gelu_tanh(ys)
    o_ref[...] = z * _sigmoid(_dot(z.astype(BF16), w_ref[...]) + b_ref[...])


def _s5_post(yf, yb, u_tm, s5_d, glu_w, glu_b):
    n = u_tm.shape[0]
    rt = min(ROW_TILE, n)
    spec = pl.BlockSpec((rt, W_C), lambda i: (i, 0))
    return pl.pallas_call(
        _s5_post_kernel, out_shape=jax.ShapeDtypeStruct(u_tm.shape, F32), grid=(n // rt,),
        in_specs=[spec, spec, spec, _const_spec(s5_d.shape), _const_spec(glu_w.shape), _const_spec(glu_b.shape)],
        out_specs=spec, compiler_params=_cparams(("parallel",)), name="s5_post",
    )(yf, yb, u_tm, s5_d, glu_w, glu_b)


def _rwkv_prep_kernel(x_ref, xp_ref, xn_ref, vec_ref, w1_ref, w2_ref, a1_ref, a2_ref, g1_ref, g2_ref,
                      ones_ref, *out_refs, seq):
    (g_ref, bon_ref, v_ref, kk_ref, r_ref) = out_refs[:5]
    dir_refs = (out_refs[5:8], out_refs[8:11])

    def put_pairs(ref, val):
        for q in range(ref.shape[0]):
            ref[q] = val[:, q * RWKV_PAIR:(q + 1) * RWKV_PAIR]

    x = x_ref[...]
    tm = x.shape[0]
    ext = jnp.concatenate([xp_ref[...], x, xn_ref[...]], axis=0)
    pos = _seq_pos(x.shape, tm, seq)
    dx = 0.5 * (_seq_shift(ext, 1, pos, seq, tm) + _seq_shift(ext, -1, pos, seq, tm)) - x
    w = W_D
    r, k, v, xd = (x[:, j * w:(j + 1) * w] for j in range(4))
    dr, dk_, dv, dxd = (dx[:, j * w:(j + 1) * w] for j in range(4))
    vec = vec_ref[...]
    mu = vec[0:6]
    r = r + dr * mu[0:1]
    k = k + dk_ * mu[1:2]
    v = v + dv * mu[2:3]
    xw = xd + dxd * mu[3:4]
    xa = xd + dxd * mu[4:5]
    xg = xd + dxd * mu[5:6]
    rw_kk, rw_ka, rw_rk = vec[6:7], vec[7:8], vec[8:9]
    ones = ones_ref[...]
    g_ref[...] = _bdot(_sigmoid(_bdot(xg, g1_ref[...])), g2_ref[...])
    kkr = k * rw_kk
    kk = kkr / jnp.maximum(jnp.sqrt(_bdot(kkr * kkr, ones)), 1e-12)
    put_pairs(v_ref, v)
    put_pairs(kk_ref, kk)
    put_pairs(r_ref, r)
    bonus = jnp.zeros(v.shape, F32)
    for d in range(2):
        w0, a0 = vec[9 + d:10 + d], vec[11 + d:12 + d]
        wl = -_softplus(-(w0 + _bdot(jnp.tanh(_bdot(xw, w1_ref[d])), w2_ref[d]))) - 0.5
        a = _sigmoid(a0 + _bdot(_bdot(xa, a1_ref[d]), a2_ref[d]))
        kd = k * (1.0 + (a - 1.0) * rw_ka)
        lw_ref, ka_ref, kd_ref = dir_refs[d]
        put_pairs(lw_ref, -jnp.exp(wl))
        put_pairs(ka_ref, kk * a)
        put_pairs(kd_ref, kd)
        bonus = bonus + _bdot(r * kd * rw_rk, ones) * v
    bon_ref[...] = bonus


N_RWKV_SEQS = 9


def _rwkv_prep(tiles, rkvx, vec, w1, w2, a1, a2, g1, g2, ones):
    n = rkvx.shape[0]
    npair = W_D // RWKV_PAIR
    prev, nxt = _halo_specs(tiles, 4 * W_D, 0, n)
    flat = jax.ShapeDtypeStruct((n, W_D), F32)
    paired = jax.ShapeDtypeStruct((npair, n, RWKV_PAIR), F32)
    pspec = pl.BlockSpec((npair, tiles.tm, RWKV_PAIR), lambda i: (0, i, 0))
    consts = [vec, w1, w2, a1, a2, g1, g2, ones]
    return pl.pallas_call(
        functools.partial(_rwkv_prep_kernel, seq=tiles.seq),
        out_shape=(flat,) * 2 + (paired,) * N_RWKV_SEQS, grid=(tiles.n_tiles,),
        in_specs=[tiles.row_spec(4 * W_D), prev, nxt] + [_const_spec(c.shape) for c in consts],
        out_specs=(tiles.row_spec(W_D),) * 2 + (pspec,) * N_RWKV_SEQS,
        compiler_params=_cparams(("parallel",)), name="rwkv_prep",
    )(rkvx, rkvx, rkvx, *consts)


def _bmm(a, b, precision=None):
    return jnp.einsum('umk,ukn->umn', a, b, precision=precision, preferred_element_type=F32)


def _bmm_nt(a, b):
    return jnp.einsum('umk,unk->umn', a, b, preferred_element_type=F32)


def _bmm_tn(a, b):
    return jnp.stack([_dot_tn(a[u], b[u]) for u in range(a.shape[0])], axis=0)


def _bf(a):
    return a.astype(BF16)


def _rwkv_masks(t, bwd):
    n2 = 2 * t
    ri = lax.broadcasted_iota(jnp.int32, (n2, n2), 0)
    ci = lax.broadcasted_iota(jnp.int32, (n2, n2), 1)
    same_head = (ri >= t) == (ci >= t)
    tt, ss = ri & (t - 1), ci & (t - 1)
    before = (ss > tt) if bwd else (ss < tt)
    ti = lax.broadcasted_iota(jnp.int32, (t, t), 0)
    si = lax.broadcasted_iota(jnp.int32, (t, t), 1)
    return dict(
        strict=same_head & before,
        incl=same_head & (before | (ss == tt)),
        same16=(ri >> 4) == (ci >> 4),
        same32=(ri >> 5) == (ci >> 5),
        eye=(ri == ci).astype(F32),
        tri=((si >= ti) if bwd else (si <= ti)).astype(F32),
        head0=lax.broadcasted_iota(jnp.int32, (t, RWKV_PAIR), 1) < HD_D,
    )


def _split(a):
    hi = a.astype(BF16)
    return hi, (a - hi.astype(F32)).astype(BF16)


def _bmm3(a, b):
    ah, al = _split(a)
    bh, bl = _split(b)
    n = b.shape[2]
    lhs = jnp.concatenate([ah, al], axis=2)
    rhs = jnp.concatenate([jnp.concatenate([bh, bl], axis=2),
                           jnp.concatenate([bh, jnp.zeros_like(bl)], axis=2)], axis=1)
    out = _bmm(lhs, rhs)
    return out[:, :, :n] + out[:, :, n:]


def _unit_tri_inverse(a, mk):
    eye = mk['eye']
    n1 = -jnp.where(mk['same16'], a, 0.0)
    n2 = _bmm3(n1, n1)
    n4 = _bmm3(n2, n2)
    n8 = _bmm3(n4, n4)
    p = _bmm3(eye + n1, eye + n2)
    p = _bmm3(p, eye + n4)
    p = _bmm3(p, eye + n8)
    c1 = jnp.where(mk['same32'], jnp.where(mk['same16'], 0.0, a), 0.0)
    p = p - _bmm3(_bmm3(p, c1), p)
    c2 = jnp.where(mk['same32'], 0.0, a)
    return p - _bmm3(_bmm3(p, c2), p)


def _rwkv_unit(r, v, kap, lw, beta, kd, s, mk, bwd):
    u, t, _ = r.shape
    tri = jnp.broadcast_to(mk['tri'][None], (u, t, t))
    c = _bmm(tri, lw, precision=lax.Precision.HIGHEST)
    c_end = c[:, 0:1] if bwd else c[:, t - 1:t]
    e_neg = jnp.exp(-c)
    e_end = jnp.exp(c_end - c)
    head0 = mk['head0']

    def stack(x):
        return jnp.concatenate([jnp.where(head0, x, 0.0), jnp.where(head0, 0.0, x)], axis=1)

    rs = stack(r * jnp.exp(c))
    ks = stack(kap * jnp.exp(c - lw))
    bs = stack(beta * e_neg)
    kds = stack(kd * e_neg)
    vs = stack(v)
    bps = stack(beta * e_end)
    kps = stack(kd * e_end)
    n2 = 2 * t
    g = _bmm_nt(_bf(jnp.concatenate([ks, rs], axis=1)), _bf(jnp.concatenate([bs, kds], axis=1)))
    a_ab = jnp.where(mk['strict'], g[:, :n2, :n2], 0.0)
    a_ak = jnp.where(mk['strict'], g[:, :n2, n2:], 0.0)
    a_rb = jnp.where(mk['incl'], g[:, n2:, :n2], 0.0)
    a_rk = jnp.where(mk['incl'], g[:, n2:, n2:], 0.0)
    m = _unit_tri_inverse(a_ab, mk)
    vsb = _bf(vs)
    sol = _bmm3(m, jnp.concatenate([ks, _bmm(_bf(a_ak), vsb)], axis=2))
    wy = jnp.concatenate([rs, _bmm(_bf(a_rk), vsb)], axis=2) - _bmm(_bf(a_rb), _bf(sol))
    wy = wy[:, :t] + wy[:, t:]
    pg = _bmm_tn(_bf(sol), _bf(bps))
    gam = _bmm_tn(vsb, _bf(kps)) - pg[:, RWKV_PAIR:]
    y = _bmm_nt(_bf(wy[:, :, :RWKV_PAIR]), _bf(s)) + wy[:, :, RWKV_PAIR:]
    s_new = s * jnp.exp(c_end) - _bmm(_bf(s), _bf(pg[:, :RWKV_PAIR])) + gam
    return y, s_new


def _rwkv_chunk_kernel(*refs, nbt, has_init):
    del nbt
    it = iter(refs)
    ins = [tuple(next(it) for _ in range(6)) for _ in range(2)]
    s0_ref = next(it) if has_init else None
    yf_ref, yb_ref, fin_ref, st_ref = (next(it) for _ in range(4))
    c = pl.program_id(2)

    @pl.when(c == 0)
    def _():
        for d in range(2):
            st_ref[d] = s0_ref[:, d, 0] if has_init else jnp.zeros(st_ref.shape[1:], F32)

    t = RWKV_CHUNK
    nsub = yf_ref.shape[2] // t
    masks = [_rwkv_masks(t, bwd=False), _rwkv_masks(t, bwd=True)]

    def chunk(j, carry):
        for d, y_ref in enumerate((yf_ref, yb_ref)):
            rows = pl.ds(pl.multiple_of((j if d == 0 else nsub - 1 - j) * t, t), t)
            y, s_new = _rwkv_unit(*(ref[0, :, rows, :] for ref in ins[d]), st_ref[d], masks[d], bwd=(d == 1))
            y_ref[0, :, rows, :] = y
            st_ref[d] = s_new
        return carry

    lax.fori_loop(0, nsub, chunk, 0)
    for d in range(2):
        fin_ref[:, d, 0] = st_ref[d]


def _rwkv_scan(shared, fdir, bdir, batch, seq, s0=None):
    nbt = min(batch, 4)
    tb = min(seq, 4 * RWKV_CHUNK)
    n = seq // tb
    npair = W_D // RWKV_PAIR
    shp = (npair, batch, seq, RWKV_PAIR)
    fspec = pl.BlockSpec((1, nbt, tb, RWKV_PAIR), lambda g, q, c: (q, g, c, 0))
    bspec = pl.BlockSpec((1, nbt, tb, RWKV_PAIR), lambda g, q, c: (q, g, n - 1 - c, 0))
    sspec = pl.BlockSpec((nbt, 2, 1, RWKV_PAIR, RWKV_PAIR), lambda g, q, c: (g, 0, q, 0, 0))
    args = [a.reshape(shp) for a in tuple(shared) + tuple(fdir) + tuple(shared) + tuple(bdir)]
    specs = [fspec] * 6 + [bspec] * 6
    if s0 is not None:
        args.append(s0)
        specs.append(sspec)
    yf, yb, fin = pl.pallas_call(
        functools.partial(_rwkv_chunk_kernel, nbt=nbt, has_init=s0 is not None),
        out_shape=(jax.ShapeDtypeStruct(shp, F32), jax.ShapeDtypeStruct(shp, F32),
                   jax.ShapeDtypeStruct((batch, 2, npair, RWKV_PAIR, RWKV_PAIR), F32)),
        grid=(batch // nbt, npair, n), in_specs=specs, out_specs=(fspec, bspec, sspec),
        scratch_shapes=[pltpu.VMEM((2, nbt, RWKV_PAIR, RWKV_PAIR), F32)],
        compiler_params=_cparams(("parallel", "parallel", "arbitrary")), name="rwkv_scan",
    )(*args)
    pshape = (npair, batch * seq, RWKV_PAIR)
    return yf.reshape(pshape), yb.reshape(pshape), fin


def _block_diag(w):
    n, a, b = w.shape
    return jnp.einsum('nab,nm->namb', w, jnp.eye(n, dtype=w.dtype)).reshape(n * a, n * b)


def _rope_tables(seq, dk):
    t = jnp.arange(seq)
    row = (t // GRID_W).astype(F32)
    col = (t % GRID_W).astype(F32)
    nf = dk // 4
    freqs = ROPE_BASE ** (-jnp.arange(nf, dtype=F32) / nf)
    ar, ac = row[:, None] * freqs, col[:, None] * freqs
    cos = jnp.concatenate([jnp.cos(ar), jnp.cos(ar), jnp.cos(ac), jnp.cos(ac)], axis=1)
    sin = jnp.concatenate([-jnp.sin(ar), jnp.sin(ar), -jnp.sin(ac), jnp.sin(ac)], axis=1)
    return cos, sin


def _trunk(x3, cond_base, per_batch_cond, is_latent, states, mods, p):
    batch, seq, d = x3.shape
    tiles = _Tiles(batch, seq, cond_base, per_batch_cond)
    x = x3.reshape(batch * seq, d)
    s_ret, s_lru, s_s5, s_rwkv = states

    m0, g0 = mods[0], p['norm_g'][0]
    x = _ffn(tiles, x, m0, g0, p['f1a'][0], p['f1b'][0], first=True)
    qkvg, gx = _proj_even(tiles, x, m0, g0, p['l0_w_in'])
    rope = _rope_tables(seq, DK_A) if is_latent else None
    y_a, new_ret = _retention(qkvg, p['ret_dec'], batch, seq, s0=s_ret, rope_tabs=rope)
    af, bf, ab, bb = _lru_prep(tiles, gx, p['conv_wb'], p['lru_wg'], p['lru_bg'], p['lru_lam'])
    hf, hb, new_lru = _lru_scan(af, bf, ab, bb, batch, seq, h0=s_lru)
    x = _ffn(tiles, x, m0, g0, p['f2a'][0], p['f2b'][0], first=False, mix="even",
             mix_args=(y_a, gx, hf, hb), wo=p['l0_w_out'])

    m1, g1 = mods[1], p['norm_g'][1]
    x = _ffn(tiles, x, m1, g1, p['f1a'][1], p['f1b'][1], first=True)
    u_tm, rkvx = _proj_odd(tiles, x, m1, g1, p['l1_w_in'])
    bp = batch
    u_tm = u_tm.reshape(seq * bp, W_C)
    s5_0 = None
    if s_s5 is not None:
        s5_0 = jnp.transpose(s_s5.reshape(batch, 2, S5_STATE), (1, 0, 2))
        s5_0 = jnp.concatenate([s5_0] * (max(bp, SUBLANES) // bp), axis=1)
    yf5, yb5, fin5 = _s5_scan(u_tm, p['s5_bbar'], p['s5_cc'], p['s5_abar'], bp, seq, s0=s5_0)
    yc_tm = _s5_post(yf5, yb5, u_tm, p['s5_d'], p['glu_w'], p['glu_b']).reshape(seq, bp * W_C)
    new_s5 = jnp.transpose(fin5[:, :batch], (1, 0, 2)).reshape(batch, 2, 2, G_C, N_C)
    outs = _rwkv_prep(tiles, rkvx, p['rw_vec'], p['rw_w1'], p['rw_w2'], p['rw_a1'], p['rw_a2'],
                      p['rw_g1'], p['rw_g2'], p['ones_hd'])
    gg, bonus, v, kk, r = outs[:5]
    npair = H_D // 2
    eye2 = jnp.eye(2, dtype=F32)
    rw0 = None
    if s_rwkv is not None:
        rw0 = jnp.einsum('bdqjvk,ji->bdqjvik', s_rwkv.reshape(batch, 2, npair, 2, HD_D, HD_D), eye2)
        rw0 = rw0.reshape(batch, 2, npair, RWKV_PAIR, RWKV_PAIR)
    yfr, ybr, finr = _rwkv_scan((r, v, kk), outs[5:8], outs[8:11], batch, seq, s0=rw0)
    finr = finr.reshape(batch, 2, npair, 2, HD_D, 2, HD_D)
    new_rwkv = jnp.stack([finr[:, :, :, 0, :, 0, :], finr[:, :, :, 1, :, 1, :]], axis=3)
    new_rwkv = new_rwkv.reshape(batch, 2, H_D, HD_D, HD_D)
    x = _ffn(tiles, x, m1, g1, p['f2a'][1], p['f2b'][1], first=False, mix="odd",
             mix_args=(yc_tm, yfr, ybr, bonus, gg, p['avg_hd'], p['ln_wb']), wo=p['l1_w_out'],
             final_g=p['final_norm'])
    return x.reshape(batch, seq, d), (new_ret, new_lru, new_s5, new_rwkv)


def kernel(x_prompt, x_sample, state_l0_ret, state_l0_lru, state_l1_s5, state_l1_rwkv, c, c_ctx, mod_w, mod_b, norm_g, ffn1_w13, ffn1_w2, ffn2_w13, ffn2_w2, final_norm, l0_w_in, l0_w_out, l0_ret_decay, l0_conv_w, l0_conv_b, l0_lru_lam, l0_lru_wa, l0_lru_ba, l0_lru_wx, l0_lru_bx, l1_w_in, l1_w_out, l1_s5_a_re, l1_s5_a_im, l1_s5_log_dt, l1_s5_b_re, l1_s5_b_im, l1_s5_c_re, l1_s5_c_im, l1_s5_d, l1_glu_w, l1_glu_b, l1_rw_mu, l1_rw_w0, l1_rw_w1, l1_rw_w2, l1_rw_a0, l1_rw_a1, l1_rw_a2, l1_rw_g1, l1_rw_g2, l1_rw_kk, l1_rw_ka, l1_rw_rk, l1_ln_w, l1_ln_b):
    d = x_prompt.shape[-1]
    nlat = c.shape[0]
    cond8 = jnp.concatenate([c_ctx[None, :], c, jnp.zeros((SUBLANES - 1 - nlat, d), F32)], axis=0)
    mods = _modulation(cond8, mod_w, mod_b)

    gn = G_C * N_C
    gb = G_C // S5_BLOCKS
    eye_g = jnp.eye(gb, dtype=F32)

    def embed(w, spec):
        w = w.reshape((2, S5_BLOCKS, gb) + w.shape[2:])
        return jnp.einsum(spec, w, eye_g)

    bre = embed(l1_s5_b_re, 'djgns,gh->djgshn').reshape(2, S5_BLOCKS, gb * GS_C, gb * N_C)
    bim = embed(l1_s5_b_im, 'djgns,gh->djgshn').reshape(2, S5_BLOCKS, gb * GS_C, gb * N_C)
    cre = embed(l1_s5_c_re, 'djgsn,gh->djgnhs').reshape(2, S5_BLOCKS, gb * N_C, gb * GS_C)
    cim = embed(l1_s5_c_im, 'djgsn,gh->djgnhs').reshape(2, S5_BLOCKS, gb * N_C, gb * GS_C)
    prm = jnp.stack([l1_s5_a_re.reshape(2, gn), l1_s5_a_im.reshape(2, gn),
                     jnp.repeat(l1_s5_log_dt, N_C, axis=1)], axis=1)
    s5_bbar, s5_abar = _s5_params(prm, bre, bim)
    head_id = jnp.arange(W_D) // HD_D
    ones_hd = (head_id[:, None] == head_id[None, :]).astype(BF16)
    p = dict(
        norm_g=norm_g, final_norm=final_norm.reshape(1, d),
        f1a=ffn1_w13.astype(BF16), f1b=ffn1_w2.astype(BF16),
        f2a=ffn2_w13.astype(BF16), f2b=ffn2_w2.astype(BF16),
        l0_w_in=l0_w_in.astype(BF16), l0_w_out=l0_w_out.astype(BF16),
        l1_w_in=l1_w_in.astype(BF16), l1_w_out=l1_w_out.astype(BF16),
        ret_dec=jnp.broadcast_to(l0_ret_decay.astype(F32)[:, :, None, None], (2, H_A, 1, DK_A)),
        conv_wb=jnp.concatenate([l0_conv_w, l0_conv_b[None, :]], axis=0),
        lru_wg=jnp.concatenate([_block_diag(l0_lru_wa[0]), _block_diag(l0_lru_wx[0]),
                                _block_diag(l0_lru_wa[1]), _block_diag(l0_lru_wx[1])], axis=1).astype(BF16),
        lru_bg=jnp.concatenate([l0_lru_ba[0], l0_lru_bx[0], l0_lru_ba[1], l0_lru_bx[1]])[None, :],
        lru_lam=l0_lru_lam,
        s5_bbar=s5_bbar, s5_abar=s5_abar,
        s5_cc=jnp.concatenate([cre, -cim], axis=2).astype(BF16),
        s5_d=l1_s5_d.reshape(1, W_C), glu_w=l1_glu_w.astype(BF16), glu_b=l1_glu_b.reshape(1, W_C),
        rw_vec=jnp.concatenate([l1_rw_mu, l1_rw_kk[None], l1_rw_ka[None], l1_rw_rk[None],
                                l1_rw_w0, l1_rw_a0, jnp.zeros((3, W_D), F32)], axis=0),
        rw_w1=l1_rw_w1.astype(BF16), rw_w2=l1_rw_w2.astype(BF16),
        rw_a1=l1_rw_a1.astype(BF16), rw_a2=l1_rw_a2.astype(BF16),
        rw_g1=l1_rw_g1.astype(BF16), rw_g2=l1_rw_g2.astype(BF16),
        ones_hd=ones_hd, avg_hd=(ones_hd.astype(F32) / HD_D).astype(BF16),
        ln_wb=jnp.stack([l1_ln_w, l1_ln_b], axis=0),
    )
    y_prompt, ctx_states = _trunk(x_prompt, 0, False, False, (None, None, None, None), mods, p)
    y_sample, _ = _trunk(x_sample, 1, True, True,
                         (state_l0_ret, state_l0_lru, state_l1_s5, state_l1_rwkv), mods, p)
    return (y_prompt, y_sample) + ctx_states
```

```python
import functools
import math

import jax
import jax.numpy as jnp
from jax import lax
from jax.experimental import pallas as pl
from jax.experimental.pallas import tpu as pltpu

F32 = jnp.float32
BF16 = jnp.bfloat16

N_MOD = 9
NORM_EPS = 1e-6
HEAD_NORM_EPS = 1e-5
RWKV_LN_EPS = 64e-5
ROPE_BASE = 10000.0
GRID_W = 64
H_A = 4
DK_A = 128
W_A = 512
W_B = 512
NB_B = 8
LRU_C = 8.0
W_C = 512
GS_C = 16
G_C = 32
N_C = 64
W_D = 512
HD_D = 64
H_D = 8
S5_STATE = 2 * G_C * N_C
RWKV_CHUNK = 64
RWKV_PAIR = 2 * HD_D

SUBLANES = 8
LANES = 128
VMEM_LIMIT = 56 * 1024 * 1024
ROW_TILE = 512


def _cparams(sem):
    return pltpu.CompilerParams(dimension_semantics=sem, vmem_limit_bytes=VMEM_LIMIT)


def _dot(a, b):
    return jnp.dot(a, b, preferred_element_type=F32)


def _dot_nt(a, b):
    return lax.dot_general(a, b, (((1,), (1,)), ((), ())), preferred_element_type=F32)


def _dot_tn(a, b):
    return lax.dot_general(a, b, (((0,), (0,)), ((), ())), preferred_element_type=F32)


def _bdot(a, b):
    return _dot(a.astype(BF16), b.astype(BF16))


def _sigmoid(x):
    return 1.0 / (1.0 + jnp.exp(-x))


def _silu(x):
    return x * _sigmoid(x)


def _softplus(x):
    return jnp.maximum(x, 0.0) + jnp.log1p(jnp.exp(-jnp.abs(x)))


def _log_sigmoid(x):
    return -_softplus(-x)


def _gelu_tanh(x):
    return 0.5 * x * (1.0 + jnp.tanh(math.sqrt(2.0 / math.pi) * (x + 0.044715 * (x * x * x))))


def _rms(x):
    return x * lax.rsqrt(jnp.mean(x * x, axis=-1, keepdims=True) + NORM_EPS)


def _rms_mod(x, g, scale, shift):
    return (_rms(x) * g) * (1.0 + scale) + shift


def _const_spec(shape):
    nd = len(shape)
    return pl.BlockSpec(shape, lambda *_: (0,) * nd, pipeline_mode=pl.Buffered(1))


def _mod_kernel(c_ref, w_ref, b_ref, o_ref):
    o_ref[0] = _bdot(_silu(c_ref[...]), w_ref[0]) + b_ref[0]


def _modulation(cond8, mod_w, mod_b):
    depth, d, nd = mod_w.shape
    return pl.pallas_call(
        _mod_kernel,
        out_shape=jax.ShapeDtypeStruct((depth, SUBLANES, nd), F32),
        grid=(depth, nd // d),
        in_specs=[pl.BlockSpec((SUBLANES, d), lambda l, j: (0, 0)),
                  pl.BlockSpec((1, d, d), lambda l, j: (l, 0, j)),
                  pl.BlockSpec((1, 1, d), lambda l, j: (l, 0, j))],
        out_specs=pl.BlockSpec((1, SUBLANES, d), lambda l, j: (l, 0, j)),
        compiler_params=_cparams(("arbitrary", "arbitrary")),
        name="modulation",
    )(cond8, mod_w, mod_b.reshape(depth, 1, nd)).reshape(depth, SUBLANES, N_MOD, d)


class _Tiles:
    def __init__(self, batch, seq, cond_base, per_batch_cond, row_tile=ROW_TILE):
        self.batch, self.seq = batch, seq
        self.tm = min(row_tile, batch * seq)
        if seq >= self.tm:
            self.nb, self.tl = 1, self.tm
        else:
            self.nb, self.tl = self.tm // seq, seq
        assert seq % self.tl == 0 and batch % self.nb == 0
        self.tiles_per_seq = seq // self.tl
        self.n_tiles = batch * seq // self.tm
        self.cond_base = cond_base
        self.tiles_per_cond = self.tiles_per_seq if per_batch_cond else self.n_tiles

    def cond_row(self, i):
        return self.cond_base + i // self.tiles_per_cond

    def mod_spec(self, d):
        return pl.BlockSpec((1, N_MOD, d), lambda i: (self.cond_row(i), 0, 0))

    def row_spec(self, width, col=0):
        return pl.BlockSpec((self.tm, width), lambda i: (i, col))


def _ffn_kernel(*refs, mix, first, final, ff_chunks):
    it = iter(refs)
    x_ref, m_ref, g_ref, w13_ref, w2_ref = (next(it) for _ in range(5))
    x = x_ref[...]
    m = m_ref[0]
    if mix == "even":
        ya_ref, gb_ref, hf_ref, hb_ref, wo_ref = (next(it) for _ in range(5))
        ya = ya_ref[...]
        yb = _gelu_tanh(gb_ref[...]) * (hf_ref[...] + hb_ref[...])
    elif mix == "odd":
        yc_ref, yf_ref, yb_ref, bon_ref, gg_ref, avg_ref, ln_ref, wo_ref = (next(it) for _ in range(8))
        ya = yc_ref[...]
        ys = jnp.concatenate([yf_ref[q] + yb_ref[q] for q in range(yf_ref.shape[0])], axis=1)
        avg = avg_ref[...]
        oc = ys - _bdot(ys, avg)
        yd = oc * lax.rsqrt(_bdot(oc * oc, avg) + RWKV_LN_EPS)
        yd = yd * ln_ref[0:1] + ln_ref[1:2]
        yb = (yd + bon_ref[...]) * gg_ref[...]
    if mix is not None:
        wa = ya.shape[1]
        y = _dot(ya.astype(BF16), wo_ref[0:wa, :]) + _dot(yb.astype(BF16), wo_ref[wa:, :])
        x = x + m[5:6] * y
    fin_ref = next(it) if final else None
    o_ref = next(it)
    r0 = 0 if first else 6
    gi = 0 if first else 2
    h = _rms_mod(x, g_ref[gi:gi + 1], m[r0 + 1:r0 + 2], m[r0:r0 + 1]).astype(BF16)
    dff = w2_ref.shape[0]
    fc = dff // ff_chunks
    acc = jnp.zeros(x.shape, F32)
    for j in range(ff_chunks):
        gate = _dot(h, w13_ref[:, j * fc:(j + 1) * fc])
        up = _dot(h, w13_ref[:, dff + j * fc:dff + (j + 1) * fc])
        acc = acc + _dot((_silu(gate) * up).astype(BF16), w2_ref[j * fc:(j + 1) * fc, :])
    x = x + 0.5 * m[r0 + 2:r0 + 3] * acc
    if final:
        x = _rms(x) * fin_ref[...]
    o_ref[...] = x


def _ffn(tiles, x, mods, norm_g, w13, w2, *, first, mix=None, mix_args=(), wo=None, final_g=None):
    n, d = x.shape
    dff = w2.shape[0]
    args = [x, mods, norm_g, w13, w2]
    specs = [tiles.row_spec(d), tiles.mod_spec(d), _const_spec(norm_g.shape),
             _const_spec(w13.shape), _const_spec(w2.shape)]
    if mix == "even":
        ya, gx, hf, hb = mix_args
        args += [ya, gx, hf, hb, wo]
        specs += [tiles.row_spec(W_A), tiles.row_spec(W_B, 0), tiles.row_spec(W_B), tiles.row_spec(W_B),
                  _const_spec(wo.shape)]
    elif mix == "odd":
        yc_tm, yf, yb, bon, gg, avg, ln = mix_args
        args += [yc_tm, yf, yb, bon, gg, avg, ln, wo]
        pspec = pl.BlockSpec((yf.shape[0], tiles.tm, yf.shape[2]), lambda i: (0, i, 0))
        specs += [tiles.row_spec(W_C), pspec, pspec] + [tiles.row_spec(W_D)] * 2 + [
            _const_spec(avg.shape), _const_spec(ln.shape), _const_spec(wo.shape)]
    if final_g is not None:
        args.append(final_g)
        specs.append(_const_spec(final_g.shape))
    kern = functools.partial(_ffn_kernel, mix=mix, first=first, final=final_g is not None,
                             ff_chunks=2 if dff % (2 * LANES) == 0 else 1)
    return pl.pallas_call(
        kern, out_shape=jax.ShapeDtypeStruct((n, d), F32), grid=(tiles.n_tiles,),
        in_specs=specs, out_specs=tiles.row_spec(d),
        compiler_params=_cparams(("parallel",)), name="ffn",
    )(*args)


def _proj_even_kernel(x_ref, m_ref, g_ref, w_ref, qkvg_ref, gx_ref):
    m = m_ref[0]
    h = _rms_mod(x_ref[...], g_ref[1:2], m[4:5], m[3:4]).astype(BF16)
    p = _dot(h, w_ref[...])
    nq = qkvg_ref.shape[1]
    qkvg_ref[...] = p[:, :nq]
    gx_ref[...] = p[:, nq:]


def _proj_odd_kernel(x_ref, m_ref, g_ref, w_ref, u_ref, rkvx_ref):
    nb, tl, _ = x_ref.shape
    hs = []
    for b in range(nb):
        m = m_ref[b if m_ref.shape[0] > 1 else 0]
        hs.append(_rms_mod(x_ref[b], g_ref[1:2], m[4:5], m[3:4]).astype(BF16))
    p = _dot(jnp.concatenate(hs, axis=0), w_ref[...])
    for b in range(nb):
        pb = p[b * tl:(b + 1) * tl]
        for j in range(u_ref.shape[0]):
            u_ref[j, pl.ds(b, tl, stride=nb), :] = pb[:, j * LANES:(j + 1) * LANES]
        rkvx_ref[b] = pb[:, W_C:]


def _proj_even(tiles, x, mods, norm_g, w_in):
    n, d = x.shape
    return pl.pallas_call(
        _proj_even_kernel,
        out_shape=(jax.ShapeDtypeStruct((n, 4 * W_A), F32), jax.ShapeDtypeStruct((n, 2 * W_B), F32)),
        grid=(tiles.n_tiles,),
        in_specs=[tiles.row_spec(d), tiles.mod_spec(d), _const_spec(norm_g.shape), _const_spec(w_in.shape)],
        out_specs=(tiles.row_spec(4 * W_A), tiles.row_spec(2 * W_B)),
        compiler_params=_cparams(("parallel",)), name="proj_even",
    )(x, mods, norm_g, w_in)


def _time_tile(batch, seq):
    tl = max(min(ROW_TILE, batch * seq) // batch, SUBLANES)
    assert seq % tl == 0 and tl % SUBLANES == 0
    return tl


def _proj_odd(tiles, x, mods, norm_g, w_in):
    n, d = x.shape
    batch, seq = tiles.batch, tiles.seq
    tl = _time_tile(batch, seq)
    if tiles.tiles_per_cond == tiles.n_tiles:
        mspec = pl.BlockSpec((1, N_MOD, d), lambda i: (tiles.cond_base, 0, 0))
    else:
        assert tiles.cond_base % batch == 0
        mspec = pl.BlockSpec((batch, N_MOD, d), lambda i: (tiles.cond_base // batch, 0, 0))
    u_tm, rkvx = pl.pallas_call(
        _proj_odd_kernel,
        out_shape=(jax.ShapeDtypeStruct((W_C // LANES, seq * batch, LANES), F32),
                   jax.ShapeDtypeStruct((batch, seq, 4 * W_D), F32)),
        grid=(seq // tl,),
        in_specs=[pl.BlockSpec((batch, tl, d), lambda i: (0, i, 0)), mspec,
                  _const_spec(norm_g.shape), _const_spec(w_in.shape)],
        out_specs=(pl.BlockSpec((W_C // LANES, tl * batch, LANES), lambda i: (0, i, 0)),
                   pl.BlockSpec((batch, tl, 4 * W_D), lambda i: (0, i, 0))),
        compiler_params=_cparams(("parallel",)), name="proj_odd",
    )(x.reshape(batch, seq, d), mods, norm_g, w_in)
    return u_tm, rkvx.reshape(n, 4 * W_D)


def _ret_kernel(*refs, seq, chunk, rope, has_init):
    it = iter(refs)
    q_ref, k_ref, v_ref, g_ref, dec_ref = (next(it) for _ in range(5))
    cos_ref, sin_ref = (next(it), next(it)) if rope else (None, None)
    s0_ref = next(it) if has_init else None
    y_ref, sf_ref, o_scr = next(it), next(it), next(it)
    n = seq // chunk
    dk = q_ref.shape[1]
    lgf = _log_sigmoid(dec_ref[0, 0])
    lgb = _log_sigmoid(dec_ref[1, 0])
    ri = lax.broadcasted_iota(jnp.int32, (chunk, dk), 0).astype(F32)
    qf_t = jnp.exp(lgf * (ri + 1.0))
    qb_t = jnp.exp(lgb * (chunk - ri))
    kf_t = jnp.exp(lgf * (chunk - 1.0 - ri))
    kb_t = jnp.exp(lgb * ri)
    reps = chunk // dk
    lgf_c = jnp.concatenate([lgf] * reps, axis=1) if reps > 1 else lgf[:, :chunk]
    lgb_c = jnp.concatenate([lgb] * reps, axis=1) if reps > 1 else lgb[:, :chunk]
    rel = (lax.broadcasted_iota(jnp.int32, (chunk, chunk), 0)
           - lax.broadcasted_iota(jnp.int32, (chunk, chunk), 1)).astype(F32)
    dmask = (jnp.where(rel >= 0, jnp.exp(lgf_c * jnp.maximum(rel, 0.0)), 0.0)
             + jnp.where(rel <= 0, jnp.exp(lgb_c * jnp.maximum(-rel, 0.0)), 0.0))
    gcf = jnp.exp(lgf * float(chunk))
    gcb = jnp.exp(lgb * float(chunk))
    lane = lax.broadcasted_iota(jnp.int32, (chunk, dk), 1)
    first_half = (lane & (dk // 2 - 1)) < dk // 4

    def load_qkv(c):
        rows = pl.ds(pl.multiple_of(c * chunk, chunk), chunk)
        q, k, v = q_ref[rows, :], k_ref[rows, :], v_ref[rows, :]
        if rope:
            cs, sn = cos_ref[rows, :], sin_ref[rows, :]

            def rot(x):
                partner = jnp.where(first_half, pltpu.roll(x, dk - dk // 4, axis=1),
                                    pltpu.roll(x, dk // 4, axis=1))
                return x * cs + partner * sn

            q, k = rot(q), rot(k)
        return rows, q * (dk ** -0.5), k, v

    def fwd(c, s):
        rows, q, k, v = load_qkv(c)
        scores = _dot_nt(q.astype(BF16), k.astype(BF16)) * dmask
        inner = _bdot(scores, v)
        o_scr[rows, :] = inner + _bdot(q * qf_t, s)
        return s * gcf + _dot_tn((k * kf_t).astype(BF16), v.astype(BF16))

    def bwd(j, s):
        rows, q, k, v = load_qkv(n - 1 - j)
        o = o_scr[rows, :] + _bdot(q * qb_t, s)
        oc = o - jnp.mean(o, axis=-1, keepdims=True)
        on = oc * lax.rsqrt(jnp.mean(oc * oc, axis=-1, keepdims=True) + HEAD_NORM_EPS)
        y_ref[rows, :] = on * _silu(g_ref[rows, :])
        return s * gcb + _dot_tn((k * kb_t).astype(BF16), v.astype(BF16))

    zero = jnp.zeros((dk, v_ref.shape[1]), F32)
    unroll = 2 if n % 2 == 0 else 1
    sf = lax.fori_loop(0, n, fwd, s0_ref[0, 0, 0] if has_init else zero, unroll=unroll)
    sb = lax.fori_loop(0, n, bwd, s0_ref[0, 1, 0] if has_init else zero, unroll=unroll)
    sf_ref[0, 0, 0] = sf
    sf_ref[0, 1, 0] = sb


def _retention(qkvg, dec, batch, seq, s0=None, rope_tabs=None):
    dk = DK_A
    chunk = min(seq, 256)
    args = [qkvg] * 4 + [dec]
    specs = [pl.BlockSpec((seq, dk), lambda b, h, j=j: (b, j * H_A + h)) for j in range(4)]
    specs.append(pl.BlockSpec((2, 1, 1, dk), lambda b, h: (0, h, 0, 0)))
    if rope_tabs is not None:
        args += list(rope_tabs)
        specs += [_const_spec((seq, dk))] * 2
    if s0 is not None:
        args.append(s0)
        specs.append(pl.BlockSpec((1, 2, 1, dk, dk), lambda b, h: (b, 0, h, 0, 0)))
    kern = functools.partial(_ret_kernel, seq=seq, chunk=chunk, rope=rope_tabs is not None,
                             has_init=s0 is not None)
    return pl.pallas_call(
        kern,
        out_shape=(jax.ShapeDtypeStruct((batch * seq, W_A), F32),
                   jax.ShapeDtypeStruct((batch, 2, H_A, dk, dk), F32)),
        grid=(batch, H_A),
        in_specs=specs,
        out_specs=(pl.BlockSpec((seq, dk), lambda b, h: (b, h)),
                   pl.BlockSpec((1, 2, 1, dk, dk), lambda b, h: (b, 0, h, 0, 0))),
        scratch_shapes=[pltpu.VMEM((seq, dk), F32)],
        compiler_params=_cparams(("parallel", "arbitrary")), name="retention",
    )(*args)


def _halo_specs(tiles, width, col, n_rows):
    tm = tiles.tm
    nblk = n_rows // SUBLANES
    per = tm // SUBLANES
    prev = pl.BlockSpec((SUBLANES, width), lambda i: (jnp.maximum(i * per - 1, 0), col))
    nxt = pl.BlockSpec((SUBLANES, width), lambda i: (jnp.minimum((i + 1) * per, nblk - 1), col))
    return prev, nxt


def _seq_shift(ext, s, pos, seq, tm):
    sh = pltpu.roll(ext, s % ext.shape[0], axis=0)[SUBLANES:SUBLANES + tm]
    ok = (pos >= s) if s > 0 else (pos < seq + s)
    return jnp.where(ok, sh, 0.0)


def _seq_pos(shape, tm, seq):
    return (pl.program_id(0) * tm + lax.broadcasted_iota(jnp.int32, shape, 0)) & (seq - 1)


def _lru_prep_kernel(x_ref, xp_ref, xn_ref, cw_ref, wg_ref, bg_ref, lam_ref,
                     af_ref, bf_ref, ab_ref, bb_ref, *, seq):
    x = x_ref[...]
    tm = x.shape[0]
    ext = jnp.concatenate([xp_ref[...], x, xn_ref[...]], axis=0)
    pos = _seq_pos(x.shape, tm, seq)
    xc = (cw_ref[4:5] + cw_ref[0:1] * _seq_shift(ext, 2, pos, seq, tm)
          + cw_ref[1:2] * _seq_shift(ext, 1, pos, seq, tm) + cw_ref[2:3] * x
          + cw_ref[3:4] * _seq_shift(ext, -1, pos, seq, tm))
    gates = _dot(xc.astype(BF16), wg_ref[...]) + bg_ref[...]
    w = x.shape[1]
    for d, (a_ref, b_ref) in enumerate(((af_ref, bf_ref), (ab_ref, bb_ref))):
        r = _sigmoid(gates[:, (2 * d) * w:(2 * d + 1) * w])
        i = _sigmoid(gates[:, (2 * d + 1) * w:(2 * d + 2) * w])
        log_a = LRU_C * r * _log_sigmoid(lam_ref[d:d + 1])
        a = jnp.exp(log_a)
        a_ref[...] = a
        b_ref[...] = jnp.sqrt(1.0 - a * a) * (i * xc)


def _lru_prep(tiles, gx, conv_wb, wg, bg, lam):
    n = gx.shape[0]
    prev, nxt = _halo_specs(tiles, W_B, 1, n)
    out = jax.ShapeDtypeStruct((n, W_B), F32)
    return pl.pallas_call(
        functools.partial(_lru_prep_kernel, seq=tiles.seq),
        out_shape=(out,) * 4, grid=(tiles.n_tiles,),
        in_specs=[tiles.row_spec(W_B, 1), prev, nxt, _const_spec(conv_wb.shape), _const_spec(wg.shape),
                  _const_spec(bg.shape), _const_spec(lam.shape)],
        out_specs=(tiles.row_spec(W_B),) * 4,
        compiler_params=_cparams(("parallel",)), name="lru_prep",
    )(gx, gx, gx, conv_wb, wg, bg, lam)


def _lru_scan_kernel(*refs, nbt, tc, has_init):
    it = iter(refs)
    af_ref, bf_ref, ab_ref, bb_ref = (next(it) for _ in range(4))
    h0_ref = next(it) if has_init else None
    hf_ref, hb_ref, fin_ref, st_ref = (next(it) for _ in range(4))
    c = pl.program_id(1)

    @pl.when(c == 0)
    def _():
        st_ref[...] = h0_ref[...] if has_init else jnp.zeros(st_ref.shape, F32)

    def body(t, carry):
        tb = tc - 1 - t
        new = []
        for j in range(nbt):
            hf, hb = carry[2 * j], carry[2 * j + 1]
            hf = af_ref[j, pl.ds(t, 1), :] * hf + bf_ref[j, pl.ds(t, 1), :]
            hb = ab_ref[j, pl.ds(tb, 1), :] * hb + bb_ref[j, pl.ds(tb, 1), :]
            hf_ref[j, pl.ds(t, 1), :] = hf
            hb_ref[j, pl.ds(tb, 1), :] = hb
            new += [hf, hb]
        return tuple(new)

    init = tuple(st_ref[j, d:d + 1, :] for j in range(nbt) for d in range(2))
    out = lax.fori_loop(0, tc, body, init)
    for j in range(nbt):
        for d in range(2):
            st_ref[j, d:d + 1, :] = out[2 * j + d]
    fin_ref[...] = st_ref[...]


def _lru_scan(af, bf, ab, bb, batch, seq, h0=None):
    nbt = min(batch, 4)
    tc = min(seq, 256)
    n = seq // tc
    shp = (batch, seq, W_B)
    fspec = pl.BlockSpec((nbt, tc, W_B), lambda g, c: (g, c, 0))
    bspec = pl.BlockSpec((nbt, tc, W_B), lambda g, c: (g, n - 1 - c, 0))
    sspec = pl.BlockSpec((nbt, 2, W_B), lambda g, c: (g, 0, 0))
    args = [af.reshape(shp), bf.reshape(shp), ab.reshape(shp), bb.reshape(shp)]
    specs = [fspec, fspec, bspec, bspec]
    if h0 is not None:
        args.append(h0)
        specs.append(sspec)
    hf, hb, fin = pl.pallas_call(
        functools.partial(_lru_scan_kernel, nbt=nbt, tc=tc, has_init=h0 is not None),
        out_shape=(jax.ShapeDtypeStruct(shp, F32), jax.ShapeDtypeStruct(shp, F32),
                   jax.ShapeDtypeStruct((batch, 2, W_B), F32)),
        grid=(batch // nbt, n), in_specs=specs, out_specs=(fspec, bspec, sspec),
        scratch_shapes=[pltpu.VMEM((nbt, 2, W_B), F32)],
        compiler_params=_cparams(("parallel", "arbitrary")), name="lru_scan",
    )(*args)
    return hf.reshape(batch * seq, W_B), hb.reshape(batch * seq, W_B), fin


S5_BLOCKS = 2


def _s5_param_kernel(prm_ref, bre_ref, bim_ref, bbar_ref, abar_ref):
    a_re, a_im, ldt = prm_ref[0, 0:1], prm_ref[0, 1:2], prm_ref[0, 2:3]
    dt = jnp.exp(ldt)
    mag = jnp.exp(a_re * dt)
    abr = mag * jnp.cos(a_im * dt)
    abi = mag * jnp.sin(a_im * dt)
    den = a_re * a_re + a_im * a_im
    fr = ((abr - 1.0) * a_re + abi * a_im) / den
    fi = (abi * a_re - (abr - 1.0) * a_im) / den
    bre, bim = bre_ref[0, 0], bim_ref[0, 0]
    ns = bre.shape[1]
    bbar_ref[0, 0, :, 0:ns] = (fr * bre - fi * bim).astype(BF16)
    bbar_ref[0, 0, :, ns:] = (fr * bim + fi * bre).astype(BF16)
    abar_ref[0, 0:1] = abr
    abar_ref[0, 1:2] = abi


def _s5_params(prm, bre, bim):
    gn = prm.shape[2]
    _, nblk, cb, ns = bre.shape
    return pl.pallas_call(
        _s5_param_kernel,
        out_shape=(jax.ShapeDtypeStruct((2, nblk, cb, 2 * ns), BF16), jax.ShapeDtypeStruct((2, 2, gn), F32)),
        grid=(2, nblk),
        in_specs=[pl.BlockSpec((1, 3, ns), lambda d, j: (d, 0, j)),
                  pl.BlockSpec((1, 1, cb, ns), lambda d, j: (d, j, 0, 0)),
                  pl.BlockSpec((1, 1, cb, ns), lambda d, j: (d, j, 0, 0))],
        out_specs=(pl.BlockSpec((1, 1, cb, 2 * ns), lambda d, j: (d, j, 0, 0)),
                   pl.BlockSpec((1, 2, ns), lambda d, j: (d, 0, j))),
        compiler_params=_cparams(("arbitrary", "arbitrary")), name="s5_params",
    )(prm, bre, bim)


def _s5_scan_kernel(*refs, bp, tc, cw, has_init):
    it = iter(refs)
    uf_ref, ub_ref, bbar_ref, cc_ref, abar_ref = (next(it) for _ in range(5))
    s0_ref = next(it) if has_init else None
    yf_ref, yb_ref, fin_ref, buf_f, buf_b, st_ref = (next(it) for _ in range(6))
    c = pl.program_id(0)
    gn = abar_ref.shape[2]
    nblk, cb = bbar_ref.shape[1], bbar_ref.shape[2]
    ns = gn // nblk
    tr = st_ref.shape[1]
    sub = tr // bp
    n_it = tc // sub

    @pl.when(c == 0)
    def _():
        st_ref[...] = s0_ref[...] if has_init else jnp.zeros(st_ref.shape, F32)

    for d, (u_ref, buf) in enumerate(((uf_ref, buf_f), (ub_ref, buf_b))):
        u = jnp.concatenate([u_ref[j] for j in range(u_ref.shape[0])], axis=1).astype(BF16)
        for j in range(nblk):
            bu = _dot(u[:, j * cb:(j + 1) * cb], bbar_ref[d, j])
            buf[:, j * ns:(j + 1) * ns] = bu[:, :ns]
            buf[:, gn + j * ns:gn + (j + 1) * ns] = bu[:, ns:]

    first_f = lax.broadcasted_iota(jnp.int32, (tr, cw), 0) < bp

    def cmul_add(ar, ai, hr, hi, xr, xi):
        return ar * hr - ai * hi + xr, ar * hi + ai * hr + xi

    for kb in range(gn // cw):
        re = slice(kb * cw, (kb + 1) * cw)
        im = slice(gn + kb * cw, gn + (kb + 1) * cw)
        coef = []
        for d in range(2):
            coef.append((jnp.broadcast_to(abar_ref[d, 0:1, re], (tr, cw)),
                         jnp.broadcast_to(abar_ref[d, 1:2, re], (tr, cw))))

        def body(i, carry, re=re, im=im, coef=coef):
            out = []
            for d, buf in enumerate((buf_f, buf_b)):
                hr, hi = carry[2 * d], carry[2 * d + 1]
                ar, ai = coef[d]
                tile = i if d == 0 else n_it - 1 - i
                rows = pl.ds(pl.multiple_of(tile * tr, tr), tr)
                xr, xi = buf[rows, re], buf[rows, im]
                r1, i1 = cmul_add(ar, ai, hr, hi, xr, xi)
                if sub == 1:
                    outr, outi, nr, ni = r1, i1, r1, i1
                else:
                    first = first_f if d == 0 else jnp.logical_not(first_f)
                    r2, i2 = cmul_add(ar, ai, pltpu.roll(r1, bp, axis=0), pltpu.roll(i1, bp, axis=0), xr, xi)
                    outr, outi = jnp.where(first, r1, r2), jnp.where(first, i1, i2)
                    nr = jnp.where(first, pltpu.roll(r2, bp, axis=0), r2)
                    ni = jnp.where(first, pltpu.roll(i2, bp, axis=0), i2)
                buf[rows, re] = outr
                buf[rows, im] = outi
                out += [nr, ni]
            return tuple(out)

        init = (st_ref[0, :, re], st_ref[0, :, im], st_ref[1, :, re], st_ref[1, :, im])
        fr_, fi_, br_, bi_ = lax.fori_loop(0, n_it, body, init)
        st_ref[0, :, re] = fr_
        st_ref[0, :, im] = fi_
        st_ref[1, :, re] = br_
        st_ref[1, :, im] = bi_

    for d, (y_ref, buf) in enumerate(((yf_ref, buf_f), (yb_ref, buf_b))):
        for j in range(nblk):
            hre = buf[:, j * ns:(j + 1) * ns].astype(BF16)
            him = buf[:, gn + j * ns:gn + (j + 1) * ns].astype(BF16)
            yj = _dot(hre, cc_ref[d, j, 0:ns, :]) + _dot(him, cc_ref[d, j, ns:, :])
            for k in range(cb // LANES):
                y_ref[j * (cb // LANES) + k] = yj[:, k * LANES:(k + 1) * LANES]
    fin_ref[...] = st_ref[...]


def _s5_scan(u_tm, bbar, cc, abar, bp, seq, s0=None):
    assert bp % SUBLANES == 0 or 2 * bp == SUBLANES
    tr = max(bp, SUBLANES)
    rows = min(512, seq * bp)
    tc = rows // bp
    n = seq // tc
    cw = 128 if bp > SUBLANES else 256
    ng = W_C // LANES
    fspec = pl.BlockSpec((ng, rows, LANES), lambda c: (0, c, 0))
    bspec = pl.BlockSpec((ng, rows, LANES), lambda c: (0, n - 1 - c, 0))
    args = [u_tm, u_tm, bbar, cc, abar]
    specs = [fspec, bspec, _const_spec(bbar.shape), _const_spec(cc.shape), _const_spec(abar.shape)]
    if s0 is not None:
        args.append(s0)
        specs.append(_const_spec(s0.shape))
    return pl.pallas_call(
        functools.partial(_s5_scan_kernel, bp=bp, tc=tc, cw=cw, has_init=s0 is not None),
        out_shape=(jax.ShapeDtypeStruct(u_tm.shape, F32), jax.ShapeDtypeStruct(u_tm.shape, F32),
                   jax.ShapeDtypeStruct((2, tr, S5_STATE), F32)),
        grid=(n,), in_specs=specs,
        out_specs=(fspec, bspec, _const_spec((2, tr, S5_STATE))),
        scratch_shapes=[pltpu.VMEM((rows, S5_STATE), F32), pltpu.VMEM((rows, S5_STATE), F32),
                        pltpu.VMEM((2, tr, S5_STATE), F32)],
        compiler_params=_cparams(("arbitrary",)), name="s5_scan",
    )(*args)


def _s5_post_kernel(yf_ref, yb_ref, u_ref, d_ref, w_ref, b_ref, o_ref):
    nb, tl, _ = o_ref.shape
    zs = []
    for b in range(nb):
        rows = pl.ds(b, tl, stride=nb)
        yf, yb, u = (jnp.concatenate([ref[j, rows, :] for j in range(ref.shape[0])], axis=1)
                     for ref in (yf_ref, yb_ref, u_ref))
        zs.append(_gelu_tanh(yf + yb + d_ref[...] * u))
    z = jnp.concatenate(zs, axis=0)
    y = z * _sigmoid(_dot(z.astype(BF16), w_ref[...]) + b_ref[...])
    for b in range(nb):
        o_ref[b] = y[b * tl:(b + 1) * tl]


def _s5_post(yf, yb, u_tm, s5_d, glu_w, glu_b, batch, seq):
    tl = _time_tile(batch, seq)
    spec = pl.BlockSpec((W_C // LANES, tl * batch, LANES), lambda i: (0, i, 0))
    return pl.pallas_call(
        _s5_post_kernel, out_shape=jax.ShapeDtypeStruct((batch, seq, W_C), F32), grid=(seq // tl,),
        in_specs=[spec, spec, spec, _const_spec(s5_d.shape), _const_spec(glu_w.shape), _const_spec(glu_b.shape)],
        out_specs=pl.BlockSpec((batch, tl, W_C), lambda i: (0, i, 0)),
        compiler_params=_cparams(("parallel",)), name="s5_post",
    )(yf, yb, u_tm, s5_d, glu_w, glu_b).reshape(batch * seq, W_C)


def _rwkv_prep_kernel(x_ref, xp_ref, xn_ref, vec_ref, w1_ref, w2_ref, a1_ref, a2_ref, g1_ref, g2_ref,
                      ones_ref, *out_refs, seq):
    (g_ref, bon_ref, v_ref, kk_ref, r_ref) = out_refs[:5]
    dir_refs = (out_refs[5:8], out_refs[8:11])

    def put_pairs(ref, val):
        for q in range(ref.shape[0]):
            ref[q] = val[:, q * RWKV_PAIR:(q + 1) * RWKV_PAIR]

    x = x_ref[...]
    tm = x.shape[0]
    ext = jnp.concatenate([xp_ref[...], x, xn_ref[...]], axis=0)
    pos = _seq_pos(x.shape, tm, seq)
    dx = 0.5 * (_seq_shift(ext, 1, pos, seq, tm) + _seq_shift(ext, -1, pos, seq, tm)) - x
    w = W_D
    r, k, v, xd = (x[:, j * w:(j + 1) * w] for j in range(4))
    dr, dk_, dv, dxd = (dx[:, j * w:(j + 1) * w] for j in range(4))
    vec = vec_ref[...]
    mu = vec[0:6]
    r = r + dr * mu[0:1]
    k = k + dk_ * mu[1:2]
    v = v + dv * mu[2:3]
    xw = xd + dxd * mu[3:4]
    xa = xd + dxd * mu[4:5]
    xg = xd + dxd * mu[5:6]
    rw_kk, rw_ka, rw_rk = vec[6:7], vec[7:8], vec[8:9]
    ones = ones_ref[...]
    g_ref[...] = _bdot(_sigmoid(_bdot(xg, g1_ref[...])), g2_ref[...])
    kkr = k * rw_kk
    kk = kkr / jnp.maximum(jnp.sqrt(_bdot(kkr * kkr, ones)), 1e-12)
    put_pairs(v_ref, v)
    put_pairs(kk_ref, kk)
    put_pairs(r_ref, r)
    bonus = jnp.zeros(v.shape, F32)
    for d in range(2):
        w0, a0 = vec[9 + d:10 + d], vec[11 + d:12 + d]
        wl = -_softplus(-(w0 + _bdot(jnp.tanh(_bdot(xw, w1_ref[d])), w2_ref[d]))) - 0.5
        a = _sigmoid(a0 + _bdot(_bdot(xa, a1_ref[d]), a2_ref[d]))
        kd = k * (1.0 + (a - 1.0) * rw_ka)
        lw_ref, ka_ref, kd_ref = dir_refs[d]
        put_pairs(lw_ref, -jnp.exp(wl))
        put_pairs(ka_ref, kk * a)
        put_pairs(kd_ref, kd)
        bonus = bonus + _bdot(r * kd * rw_rk, ones) * v
    bon_ref[...] = bonus


N_RWKV_SEQS = 9


def _rwkv_prep(tiles, rkvx, vec, w1, w2, a1, a2, g1, g2, ones):
    n = rkvx.shape[0]
    npair = W_D // RWKV_PAIR
    prev, nxt = _halo_specs(tiles, 4 * W_D, 0, n)
    flat = jax.ShapeDtypeStruct((n, W_D), F32)
    paired = jax.ShapeDtypeStruct((npair, n, RWKV_PAIR), F32)
    pspec = pl.BlockSpec((npair, tiles.tm, RWKV_PAIR), lambda i: (0, i, 0))
    consts = [vec, w1, w2, a1, a2, g1, g2, ones]
    return pl.pallas_call(
        functools.partial(_rwkv_prep_kernel, seq=tiles.seq),
        out_shape=(flat,) * 2 + (paired,) * N_RWKV_SEQS, grid=(tiles.n_tiles,),
        in_specs=[tiles.row_spec(4 * W_D), prev, nxt] + [_const_spec(c.shape) for c in consts],
        out_specs=(tiles.row_spec(W_D),) * 2 + (pspec,) * N_RWKV_SEQS,
        compiler_params=_cparams(("parallel",)), name="rwkv_prep",
    )(rkvx, rkvx, rkvx, *consts)


def _bmm(a, b, precision=None):
    return jnp.einsum('umk,ukn->umn', a, b, precision=precision, preferred_element_type=F32)


def _bmm_nt(a, b):
    return jnp.einsum('umk,unk->umn', a, b, preferred_element_type=F32)


def _bmm_tn(a, b):
    return jnp.stack([_dot_tn(a[u], b[u]) for u in range(a.shape[0])], axis=0)


def _bf(a):
    return a.astype(BF16)


def _rwkv_masks(t, nbt):
    n2 = 2 * t
    ri = lax.broadcasted_iota(jnp.int32, (n2, n2), 0)
    ci = lax.broadcasted_iota(jnp.int32, (n2, n2), 1)
    same_head = (ri >= t) == (ci >= t)
    tt, ss = ri & (t - 1), ci & (t - 1)
    ti = lax.broadcasted_iota(jnp.int32, (t, t), 0)
    si = lax.broadcasted_iota(jnp.int32, (t, t), 1)

    def per_unit(fwd, bwd, dtype):
        return jnp.concatenate([jnp.broadcast_to(m.astype(dtype)[None], (nbt,) + m.shape) for m in (fwd, bwd)],
                               axis=0)

    return dict(
        strict=per_unit(same_head & (ss < tt), same_head & (ss > tt), F32),
        incl=per_unit(same_head & (ss <= tt), same_head & (ss >= tt), F32),
        tri=per_unit(si <= ti, si >= ti, BF16),
        same16=(ri >> 4) == (ci >> 4),
        off16=((ri >> 5) == (ci >> 5)) & ((ri >> 4) != (ci >> 4)),
        off32=(ri >> 5) != (ci >> 5),
        eye=(ri == ci).astype(F32),
        head0=lax.broadcasted_iota(jnp.int32, (t, RWKV_PAIR), 1) < HD_D,
    )


def _split(a):
    hi = a.astype(BF16)
    return hi, (a - hi.astype(F32)).astype(BF16)


def _bmm3(a, b):
    (ah, al), (bh, bl) = a, b
    n = bh.shape[2]
    lhs = jnp.concatenate([ah, al], axis=2)
    rhs = jnp.concatenate([jnp.concatenate([bh, bl], axis=2),
                           jnp.concatenate([bh, jnp.zeros_like(bl)], axis=2)], axis=1)
    out = _bmm(lhs, rhs)
    return out[:, :, :n] + out[:, :, n:]


def _unit_tri_inverse(a, mk):
    n1f = -jnp.where(mk['same16'], a, 0.0)
    n1 = _split(n1f)
    n2 = _split(_bmm3(n1, n1))
    n4 = _split(_bmm3(n2, n2))
    n8 = _split(_bmm3(n4, n4))
    p = mk['eye'] + n1f
    p = p + _bmm3(_split(p), n2)
    p = p + _bmm3(_split(p), n4)
    p = p + _bmm3(_split(p), n8)
    for off in ('off16', 'off32'):
        ps = _split(p)
        p = p - _bmm3(_split(_bmm3(ps, _split(jnp.where(mk[off], a, 0.0)))), ps)
    return _split(p)


def _rwkv_unit(r, v, kap, lw, beta, kd, s, mk):
    u, t, _ = r.shape
    lw_hi = lw.astype(BF16)
    lw_rest = lw - lw_hi.astype(F32)
    lw_mid = lw_rest.astype(BF16)
    lw_lo = (lw_rest - lw_mid.astype(F32)).astype(BF16)
    c3 = _bmm(mk['tri'], jnp.concatenate([lw_hi, lw_mid, lw_lo], axis=2))
    w = lw.shape[2]
    c = c3[:, :, :w] + c3[:, :, w:2 * w] + c3[:, :, 2 * w:]
    c_end = jnp.concatenate([c[:u // 2, t - 1:t], c[u // 2:, 0:1]], axis=0)
    e_neg = jnp.exp(-c)
    e_end = jnp.exp(c_end - c)
    head0 = mk['head0']

    def stack(x):
        return jnp.concatenate([jnp.where(head0, x, 0.0), jnp.where(head0, 0.0, x)], axis=1)

    rs = stack(r * jnp.exp(c))
    ks = stack(kap * jnp.exp(c - lw))
    bs = stack(beta * e_neg)
    kds = stack(kd * e_neg)
    vs = stack(v)
    bps = stack(beta * e_end)
    kps = stack(kd * e_end)
    n2 = 2 * t
    g = _bmm_nt(_bf(jnp.concatenate([ks, rs], axis=1)), _bf(jnp.concatenate([bs, kds], axis=1)))
    a_ab = g[:, :n2, :n2] * mk['strict']
    a_ak = g[:, :n2, n2:] * mk['strict']
    a_rb = g[:, n2:, :n2] * mk['incl']
    a_rk = g[:, n2:, n2:] * mk['incl']
    m = _unit_tri_inverse(a_ab, mk)
    vsb = _bf(vs)
    sol = _bmm3(m, _split(jnp.concatenate([ks, _bmm(_bf(a_ak), vsb)], axis=2)))
    wy = jnp.concatenate([rs, _bmm(_bf(a_rk), vsb)], axis=2) - _bmm(_bf(a_rb), _bf(sol))
    wy = wy[:, :t] + wy[:, t:]
    pg = _bmm_tn(_bf(sol), _bf(bps))
    gam = _bmm_tn(vsb, _bf(kps)) - pg[:, RWKV_PAIR:]
    y = _bmm_nt(_bf(wy[:, :, :RWKV_PAIR]), _bf(s)) + wy[:, :, RWKV_PAIR:]
    s_new = s * jnp.exp(c_end) - _bmm(_bf(s), _bf(pg[:, :RWKV_PAIR])) + gam
    return y, s_new


def _rwkv_chunk_kernel(*refs, nbt, has_init):
    it = iter(refs)
    ins = [tuple(next(it) for _ in range(6)) for _ in range(2)]
    s0_ref = next(it) if has_init else None
    yf_ref, yb_ref, fin_ref, st_ref = (next(it) for _ in range(4))
    c = pl.program_id(2)

    @pl.when(c == 0)
    def _():
        for d in range(2):
            st_ref[d * nbt:(d + 1) * nbt] = (s0_ref[:, d, 0] if has_init
                                             else jnp.zeros((nbt,) + st_ref.shape[1:], F32))

    t = RWKV_CHUNK
    nsub = yf_ref.shape[2] // t
    mk = _rwkv_masks(t, nbt)

    def chunk(j, carry):
        rows_f = pl.ds(pl.multiple_of(j * t, t), t)
        rows_b = pl.ds(pl.multiple_of((nsub - 1 - j) * t, t), t)
        ops = [jnp.concatenate([f[0, :, rows_f, :], b[0, :, rows_b, :]], axis=0) for f, b in zip(*ins)]
        y, s_new = _rwkv_unit(*ops, st_ref[...], mk)
        yf_ref[0, :, rows_f, :] = y[:nbt]
        yb_ref[0, :, rows_b, :] = y[nbt:]
        st_ref[...] = s_new
        return carry

    lax.fori_loop(0, nsub, chunk, 0)
    for d in range(2):
        fin_ref[:, d, 0] = st_ref[d * nbt:(d + 1) * nbt]


def _rwkv_scan(shared, fdir, bdir, batch, seq, s0=None):
    nbt = min(batch, 4)
    tb = min(seq, 4 * RWKV_CHUNK)
    n = seq // tb
    npair = W_D // RWKV_PAIR
    shp = (npair, batch, seq, RWKV_PAIR)
    fspec = pl.BlockSpec((1, nbt, tb, RWKV_PAIR), lambda g, q, c: (q, g, c, 0))
    bspec = pl.BlockSpec((1, nbt, tb, RWKV_PAIR), lambda g, q, c: (q, g, n - 1 - c, 0))
    sspec = pl.BlockSpec((nbt, 2, 1, RWKV_PAIR, RWKV_PAIR), lambda g, q, c: (g, 0, q, 0, 0))
    args = [a.reshape(shp) for a in tuple(shared) + tuple(fdir) + tuple(shared) + tuple(bdir)]
    specs = [fspec] * 6 + [bspec] * 6
    if s0 is not None:
        args.append(s0)
        specs.append(sspec)
    yf, yb, fin = pl.pallas_call(
        functools.partial(_rwkv_chunk_kernel, nbt=nbt, has_init=s0 is not None),
        out_shape=(jax.ShapeDtypeStruct(shp, F32), jax.ShapeDtypeStruct(shp, F32),
                   jax.ShapeDtypeStruct((batch, 2, npair, RWKV_PAIR, RWKV_PAIR), F32)),
        grid=(batch // nbt, npair, n), in_specs=specs, out_specs=(fspec, bspec, sspec),
        scratch_shapes=[pltpu.VMEM((2 * nbt, RWKV_PAIR, RWKV_PAIR), F32)],
        compiler_params=_cparams(("parallel", "parallel", "arbitrary")), name="rwkv_scan",
    )(*args)
    pshape = (npair, batch * seq, RWKV_PAIR)
    return yf.reshape(pshape), yb.reshape(pshape), fin


def _block_diag(w):
    n, a, b = w.shape
    return jnp.einsum('nab,nm->namb', w, jnp.eye(n, dtype=w.dtype)).reshape(n * a, n * b)


def _rope_tables(seq, dk):
    t = jnp.arange(seq)
    row = (t // GRID_W).astype(F32)
    col = (t % GRID_W).astype(F32)
    nf = dk // 4
    freqs = ROPE_BASE ** (-jnp.arange(nf, dtype=F32) / nf)
    ar, ac = row[:, None] * freqs, col[:, None] * freqs
    cos = jnp.concatenate([jnp.cos(ar), jnp.cos(ar), jnp.cos(ac), jnp.cos(ac)], axis=1)
    sin = jnp.concatenate([-jnp.sin(ar), jnp.sin(ar), -jnp.sin(ac), jnp.sin(ac)], axis=1)
    return cos, sin


def _trunk(x3, cond_base, per_batch_cond, is_latent, states, mods, p):
    batch, seq, d = x3.shape
    tiles = _Tiles(batch, seq, cond_base, per_batch_cond)
    x = x3.reshape(batch * seq, d)
    s_ret, s_lru, s_s5, s_rwkv = states

    m0, g0 = mods[0], p['norm_g'][0]
    x = _ffn(tiles, x, m0, g0, p['f1a'][0], p['f1b'][0], first=True)
    qkvg, gx = _proj_even(tiles, x, m0, g0, p['l0_w_in'])
    rope = _rope_tables(seq, DK_A) if is_latent else None
    y_a, new_ret = _retention(qkvg, p['ret_dec'], batch, seq, s0=s_ret, rope_tabs=rope)
    af, bf, ab, bb = _lru_prep(tiles, gx, p['conv_wb'], p['lru_wg'], p['lru_bg'], p['lru_lam'])
    hf, hb, new_lru = _lru_scan(af, bf, ab, bb, batch, seq, h0=s_lru)
    x = _ffn(tiles, x, m0, g0, p['f2a'][0], p['f2b'][0], first=False, mix="even",
             mix_args=(y_a, gx, hf, hb), wo=p['l0_w_out'])

    m1, g1 = mods[1], p['norm_g'][1]
    x = _ffn(tiles, x, m1, g1, p['f1a'][1], p['f1b'][1], first=True)
    u_tm, rkvx = _proj_odd(tiles, x, m1, g1, p['l1_w_in'])
    bp = batch
    s5_0 = None
    if s_s5 is not None:
        s5_0 = jnp.transpose(s_s5.reshape(batch, 2, S5_STATE), (1, 0, 2))
        s5_0 = jnp.concatenate([s5_0] * (max(bp, SUBLANES) // bp), axis=1)
    yf5, yb5, fin5 = _s5_scan(u_tm, p['s5_bbar'], p['s5_cc'], p['s5_abar'], bp, seq, s0=s5_0)
    yc = _s5_post(yf5, yb5, u_tm, p['s5_d'], p['glu_w'], p['glu_b'], batch, seq)
    new_s5 = jnp.transpose(fin5[:, :batch], (1, 0, 2)).reshape(batch, 2, 2, G_C, N_C)
    outs = _rwkv_prep(tiles, rkvx, p['rw_vec'], p['rw_w1'], p['rw_w2'], p['rw_a1'], p['rw_a2'],
                      p['rw_g1'], p['rw_g2'], p['ones_hd'])
    gg, bonus, v, kk, r = outs[:5]
    npair = H_D // 2
    eye2 = jnp.eye(2, dtype=F32)
    rw0 = None
    if s_rwkv is not None:
        rw0 = jnp.einsum('bdqjvk,ji->bdqjvik', s_rwkv.reshape(batch, 2, npair, 2, HD_D, HD_D), eye2)
        rw0 = rw0.reshape(batch, 2, npair, RWKV_PAIR, RWKV_PAIR)
    yfr, ybr, finr = _rwkv_scan((r, v, kk), outs[5:8], outs[8:11], batch, seq, s0=rw0)
    finr = finr.reshape(batch, 2, npair, 2, HD_D, 2, HD_D)
    new_rwkv = jnp.stack([finr[:, :, :, 0, :, 0, :], finr[:, :, :, 1, :, 1, :]], axis=3)
    new_rwkv = new_rwkv.reshape(batch, 2, H_D, HD_D, HD_D)
    x = _ffn(tiles, x, m1, g1, p['f2a'][1], p['f2b'][1], first=False, mix="odd",
             mix_args=(yc, yfr, ybr, bonus, gg, p['avg_hd'], p['ln_wb']), wo=p['l1_w_out'],
             final_g=p['final_norm'])
    return x.reshape(batch, seq, d), (new_ret, new_lru, new_s5, new_rwkv)


def kernel(x_prompt, x_sample, state_l0_ret, state_l0_lru, state_l1_s5, state_l1_rwkv, c, c_ctx, mod_w, mod_b, norm_g, ffn1_w13, ffn1_w2, ffn2_w13, ffn2_w2, final_norm, l0_w_in, l0_w_out, l0_ret_decay, l0_conv_w, l0_conv_b, l0_lru_lam, l0_lru_wa, l0_lru_ba, l0_lru_wx, l0_lru_bx, l1_w_in, l1_w_out, l1_s5_a_re, l1_s5_a_im, l1_s5_log_dt, l1_s5_b_re, l1_s5_b_im, l1_s5_c_re, l1_s5_c_im, l1_s5_d, l1_glu_w, l1_glu_b, l1_rw_mu, l1_rw_w0, l1_rw_w1, l1_rw_w2, l1_rw_a0, l1_rw_a1, l1_rw_a2, l1_rw_g1, l1_rw_g2, l1_rw_kk, l1_rw_ka, l1_rw_rk, l1_ln_w, l1_ln_b):
    d = x_prompt.shape[-1]
    nlat = c.shape[0]
    cond8 = jnp.concatenate([c_ctx[None, :], jnp.zeros((SUBLANES - 1 - nlat, d), F32), c], axis=0)
    mods = _modulation(cond8, mod_w, mod_b)

    gn = G_C * N_C
    gb = G_C // S5_BLOCKS
    eye_g = jnp.eye(gb, dtype=F32)

    def embed(w, spec):
        w = w.reshape((2, S5_BLOCKS, gb) + w.shape[2:])
        return jnp.einsum(spec, w, eye_g)

    bre = embed(l1_s5_b_re, 'djgns,gh->djgshn').reshape(2, S5_BLOCKS, gb * GS_C, gb * N_C)
    bim = embed(l1_s5_b_im, 'djgns,gh->djgshn').reshape(2, S5_BLOCKS, gb * GS_C, gb * N_C)
    cre = embed(l1_s5_c_re, 'djgsn,gh->djgnhs').reshape(2, S5_BLOCKS, gb * N_C, gb * GS_C)
    cim = embed(l1_s5_c_im, 'djgsn,gh->djgnhs').reshape(2, S5_BLOCKS, gb * N_C, gb * GS_C)
    prm = jnp.stack([l1_s5_a_re.reshape(2, gn), l1_s5_a_im.reshape(2, gn),
                     jnp.repeat(l1_s5_log_dt, N_C, axis=1)], axis=1)
    s5_bbar, s5_abar = _s5_params(prm, bre, bim)
    head_id = jnp.arange(W_D) // HD_D
    ones_hd = (head_id[:, None] == head_id[None, :]).astype(BF16)
    p = dict(
        norm_g=norm_g, final_norm=final_norm.reshape(1, d),
        f1a=ffn1_w13.astype(BF16), f1b=ffn1_w2.astype(BF16),
        f2a=ffn2_w13.astype(BF16), f2b=ffn2_w2.astype(BF16),
        l0_w_in=l0_w_in.astype(BF16), l0_w_out=l0_w_out.astype(BF16),
        l1_w_in=l1_w_in.astype(BF16), l1_w_out=l1_w_out.astype(BF16),
        ret_dec=jnp.broadcast_to(l0_ret_decay.astype(F32)[:, :, None, None], (2, H_A, 1, DK_A)),
        conv_wb=jnp.concatenate([l0_conv_w, l0_conv_b[None, :]], axis=0),
        lru_wg=jnp.concatenate([_block_diag(l0_lru_wa[0]), _block_diag(l0_lru_wx[0]),
                                _block_diag(l0_lru_wa[1]), _block_diag(l0_lru_wx[1])], axis=1).astype(BF16),
        lru_bg=jnp.concatenate([l0_lru_ba[0], l0_lru_bx[0], l0_lru_ba[1], l0_lru_bx[1]])[None, :],
        lru_lam=l0_lru_lam,
        s5_bbar=s5_bbar, s5_abar=s5_abar,
        s5_cc=jnp.concatenate([cre, -cim], axis=2).astype(BF16),
        s5_d=l1_s5_d.reshape(1, W_C), glu_w=l1_glu_w.astype(BF16), glu_b=l1_glu_b.reshape(1, W_C),
        rw_vec=jnp.concatenate([l1_rw_mu, l1_rw_kk[None], l1_rw_ka[None], l1_rw_rk[None],
                                l1_rw_w0, l1_rw_a0, jnp.zeros((3, W_D), F32)], axis=0),
        rw_w1=l1_rw_w1.astype(BF16), rw_w2=l1_rw_w2.astype(BF16),
        rw_a1=l1_rw_a1.astype(BF16), rw_a2=l1_rw_a2.astype(BF16),
        rw_g1=l1_rw_g1.astype(BF16), rw_g2=l1_rw_g2.astype(BF16),
        ones_hd=ones_hd, avg_hd=(ones_hd.astype(F32) / HD_D).astype(BF16),
        ln_wb=jnp.stack([l1_ln_w, l1_ln_b], axis=0),
    )
    y_prompt, ctx_states = _trunk(x_prompt, 0, False, False, (None, None, None, None), mods, p)
    y_sample, _ = _trunk(x_sample, SUBLANES - nlat, True, True,
                         (state_l0_ret, state_l0_lru, state_l1_s5, state_l1_rwkv), mods, p)
    return (y_prompt, y_sample) + ctx_states
```

```python
import functools
import math

import jax
import jax.numpy as jnp
from jax import lax
from jax.experimental import pallas as pl
from jax.experimental.pallas import tpu as pltpu

F32 = jnp.float32
BF16 = jnp.bfloat16

N_MOD = 9
NORM_EPS = 1e-6
HEAD_NORM_EPS = 1e-5
RWKV_LN_EPS = 64e-5
ROPE_BASE = 10000.0
GRID_W = 64
H_A = 4
DK_A = 128
W_A = 512
W_B = 512
NB_B = 8
LRU_C = 8.0
W_C = 512
GS_C = 16
G_C = 32
N_C = 64
W_D = 512
HD_D = 64
H_D = 8
S5_STATE = 2 * G_C * N_C
RWKV_CHUNK = 64
RWKV_PAIR = 2 * HD_D

SUBLANES = 8
LANES = 128
VMEM_LIMIT = 56 * 1024 * 1024
ROW_TILE = 512


def _cparams(sem):
    return pltpu.CompilerParams(dimension_semantics=sem, vmem_limit_bytes=VMEM_LIMIT)


def _dot(a, b):
    return jnp.dot(a, b, preferred_element_type=F32)


def _dot_nt(a, b):
    return lax.dot_general(a, b, (((1,), (1,)), ((), ())), preferred_element_type=F32)


def _dot_tn(a, b):
    return lax.dot_general(a, b, (((0,), (0,)), ((), ())), preferred_element_type=F32)


def _bdot(a, b):
    return _dot(a.astype(BF16), b.astype(BF16))


def _sigmoid(x):
    return 1.0 / (1.0 + jnp.exp(-x))


def _silu(x):
    return x * _sigmoid(x)


def _softplus(x):
    return jnp.maximum(x, 0.0) + jnp.log1p(jnp.exp(-jnp.abs(x)))


def _log_sigmoid(x):
    return -_softplus(-x)


def _gelu_tanh(x):
    return 0.5 * x * (1.0 + jnp.tanh(math.sqrt(2.0 / math.pi) * (x + 0.044715 * (x * x * x))))


def _rms(x):
    return x * lax.rsqrt(jnp.mean(x * x, axis=-1, keepdims=True) + NORM_EPS)


def _rms_mod(x, g, scale, shift):
    return (_rms(x) * g) * (1.0 + scale) + shift


def _const_spec(shape):
    nd = len(shape)
    return pl.BlockSpec(shape, lambda *_: (0,) * nd, pipeline_mode=pl.Buffered(1))


def _mod_kernel(c_ref, w_ref, b_ref, o_ref):
    o_ref[0] = _bdot(_silu(c_ref[...]), w_ref[0]) + b_ref[0]


def _modulation(cond8, mod_w, mod_b):
    depth, d, nd = mod_w.shape
    return pl.pallas_call(
        _mod_kernel,
        out_shape=jax.ShapeDtypeStruct((depth, SUBLANES, nd), F32),
        grid=(depth, nd // d),
        in_specs=[pl.BlockSpec((SUBLANES, d), lambda l, j: (0, 0)),
                  pl.BlockSpec((1, d, d), lambda l, j: (l, 0, j)),
                  pl.BlockSpec((1, 1, d), lambda l, j: (l, 0, j))],
        out_specs=pl.BlockSpec((1, SUBLANES, d), lambda l, j: (l, 0, j)),
        compiler_params=_cparams(("arbitrary", "arbitrary")),
        name="modulation",
    )(cond8, mod_w, mod_b.reshape(depth, 1, nd)).reshape(depth, SUBLANES, N_MOD, d)


class _Tiles:
    def __init__(self, batch, seq, cond_base, per_batch_cond, row_tile=ROW_TILE):
        self.batch, self.seq = batch, seq
        self.tm = min(row_tile, batch * seq)
        if seq >= self.tm:
            self.nb, self.tl = 1, self.tm
        else:
            self.nb, self.tl = self.tm // seq, seq
        assert seq % self.tl == 0 and batch % self.nb == 0
        self.tiles_per_seq = seq // self.tl
        self.n_tiles = batch * seq // self.tm
        self.cond_base = cond_base
        self.tiles_per_cond = self.tiles_per_seq if per_batch_cond else self.n_tiles

    def cond_row(self, i):
        return self.cond_base + i // self.tiles_per_cond

    def mod_spec(self, d):
        return pl.BlockSpec((1, N_MOD, d), lambda i: (self.cond_row(i), 0, 0))

    def row_spec(self, width, col=0):
        return pl.BlockSpec((self.tm, width), lambda i: (i, col))


def _ffn_kernel(*refs, mix, first, final, ff_chunks):
    it = iter(refs)
    x_ref, m_ref, g_ref, w13_ref, w2_ref = (next(it) for _ in range(5))
    x = x_ref[...]
    m = m_ref[0]
    if mix == "even":
        ya_ref, gb_ref, hf_ref, hb_ref, wo_ref = (next(it) for _ in range(5))
        ya = ya_ref[...]
        yb = _gelu_tanh(gb_ref[...]) * (hf_ref[...] + hb_ref[...])
    elif mix == "odd":
        yc_ref, yf_ref, yb_ref, bon_ref, gg_ref, avg_ref, ln_ref, wo_ref = (next(it) for _ in range(8))
        ya = yc_ref[...]
        ys = jnp.concatenate([yf_ref[q] + yb_ref[q] for q in range(yf_ref.shape[0])], axis=1)
        avg = avg_ref[...]
        oc = ys - _bdot(ys, avg)
        yd = oc * lax.rsqrt(_bdot(oc * oc, avg) + RWKV_LN_EPS)
        yd = yd * ln_ref[0:1] + ln_ref[1:2]
        yb = (yd + bon_ref[...]) * gg_ref[...]
    if mix is not None:
        wa = ya.shape[1]
        y = _dot(ya.astype(BF16), wo_ref[0:wa, :]) + _dot(yb.astype(BF16), wo_ref[wa:, :])
        x = x + m[5:6] * y
    fin_ref = next(it) if final else None
    o_ref = next(it)
    r0 = 0 if first else 6
    gi = 0 if first else 2
    h = _rms_mod(x, g_ref[gi:gi + 1], m[r0 + 1:r0 + 2], m[r0:r0 + 1]).astype(BF16)
    dff = w2_ref.shape[0]
    fc = dff // ff_chunks
    acc = jnp.zeros(x.shape, F32)
    for j in range(ff_chunks):
        gate = _dot(h, w13_ref[:, j * fc:(j + 1) * fc])
        up = _dot(h, w13_ref[:, dff + j * fc:dff + (j + 1) * fc])
        acc = acc + _dot((_silu(gate) * up).astype(BF16), w2_ref[j * fc:(j + 1) * fc, :])
    x = x + 0.5 * m[r0 + 2:r0 + 3] * acc
    if final:
        x = _rms(x) * fin_ref[...]
    o_ref[...] = x


def _ffn(tiles, x, mods, norm_g, w13, w2, *, first, mix=None, mix_args=(), wo=None, final_g=None):
    n, d = x.shape
    dff = w2.shape[0]
    args = [x, mods, norm_g, w13, w2]
    specs = [tiles.row_spec(d), tiles.mod_spec(d), _const_spec(norm_g.shape),
             _const_spec(w13.shape), _const_spec(w2.shape)]
    if mix == "even":
        ya, gx, hf, hb = mix_args
        args += [ya, gx, hf, hb, wo]
        specs += [tiles.row_spec(W_A), tiles.row_spec(W_B, 0), tiles.row_spec(W_B), tiles.row_spec(W_B),
                  _const_spec(wo.shape)]
    elif mix == "odd":
        yc_tm, yf, yb, bon, gg, avg, ln = mix_args
        args += [yc_tm, yf, yb, bon, gg, avg, ln, wo]
        pspec = pl.BlockSpec((yf.shape[0], tiles.tm, yf.shape[2]), lambda i: (0, i, 0))
        specs += [tiles.row_spec(W_C), pspec, pspec] + [tiles.row_spec(W_D)] * 2 + [
            _const_spec(avg.shape), _const_spec(ln.shape), _const_spec(wo.shape)]
    if final_g is not None:
        args.append(final_g)
        specs.append(_const_spec(final_g.shape))
    kern = functools.partial(_ffn_kernel, mix=mix, first=first, final=final_g is not None,
                             ff_chunks=2 if dff % (2 * LANES) == 0 else 1)
    return pl.pallas_call(
        kern, out_shape=jax.ShapeDtypeStruct((n, d), F32), grid=(tiles.n_tiles,),
        in_specs=specs, out_specs=tiles.row_spec(d),
        compiler_params=_cparams(("parallel",)), name="ffn",
    )(*args)


def _proj_even_kernel(*refs, rope):
    it = iter(refs)
    x_ref, m_ref, g_ref, w_ref = (next(it) for _ in range(4))
    cos_ref, sin_ref = (next(it), next(it)) if rope else (None, None)
    qkvg_ref, gx_ref = next(it), next(it)
    m = m_ref[0]
    h = _rms_mod(x_ref[...], g_ref[1:2], m[4:5], m[3:4]).astype(BF16)
    p = _dot(h, w_ref[...])
    nq = qkvg_ref.shape[1]
    if rope:
        qk = p[:, :2 * W_A]
        reps = 2 * W_A // DK_A
        cs = jnp.concatenate([cos_ref[...]] * reps, axis=1)
        sn = jnp.concatenate([sin_ref[...]] * reps, axis=1)
        lane = lax.broadcasted_iota(jnp.int32, qk.shape, 1)
        first_half = (lane & (DK_A // 2 - 1)) < DK_A // 4
        quarter = DK_A // 4
        partner = jnp.where(first_half, pltpu.roll(qk, 2 * W_A - quarter, axis=1), pltpu.roll(qk, quarter, axis=1))
        qkvg_ref[:, :2 * W_A] = qk * cs + partner * sn
        qkvg_ref[:, 2 * W_A:] = p[:, 2 * W_A:nq]
    else:
        qkvg_ref[...] = p[:, :nq]
    gx_ref[...] = p[:, nq:]


def _proj_odd_kernel(x_ref, m_ref, g_ref, w_ref, u_ref, rkvx_ref):
    nb, tl, _ = x_ref.shape
    hs = []
    for b in range(nb):
        m = m_ref[b if m_ref.shape[0] > 1 else 0]
        hs.append(_rms_mod(x_ref[b], g_ref[1:2], m[4:5], m[3:4]).astype(BF16))
    p = _dot(jnp.concatenate(hs, axis=0), w_ref[...])
    for b in range(nb):
        pb = p[b * tl:(b + 1) * tl]
        for j in range(u_ref.shape[0]):
            u_ref[j, pl.ds(b, tl, stride=nb), :] = pb[:, j * LANES:(j + 1) * LANES]
        rkvx_ref[b] = pb[:, W_C:]


def _proj_even(tiles, x, mods, norm_g, w_in, rope_tabs=None):
    n, d = x.shape
    args = [x, mods, norm_g, w_in]
    specs = [tiles.row_spec(d), tiles.mod_spec(d), _const_spec(norm_g.shape), _const_spec(w_in.shape)]
    if rope_tabs is not None:
        assert tiles.nb == 1
        args += list(rope_tabs)
        specs += [pl.BlockSpec((tiles.tm, DK_A), lambda i: (i % tiles.tiles_per_seq, 0))] * 2
    return pl.pallas_call(
        functools.partial(_proj_even_kernel, rope=rope_tabs is not None),
        out_shape=(jax.ShapeDtypeStruct((n, 4 * W_A), F32), jax.ShapeDtypeStruct((n, 2 * W_B), F32)),
        grid=(tiles.n_tiles,),
        in_specs=specs,
        out_specs=(tiles.row_spec(4 * W_A), tiles.row_spec(2 * W_B)),
        compiler_params=_cparams(("parallel",)), name="proj_even",
    )(*args)


def _time_tile(batch, seq):
    tl = max(min(ROW_TILE, batch * seq) // batch, SUBLANES)
    assert seq % tl == 0 and tl % SUBLANES == 0
    return tl


def _proj_odd(tiles, x, mods, norm_g, w_in):
    n, d = x.shape
    batch, seq = tiles.batch, tiles.seq
    tl = _time_tile(batch, seq)
    if tiles.tiles_per_cond == tiles.n_tiles:
        mspec = pl.BlockSpec((1, N_MOD, d), lambda i: (tiles.cond_base, 0, 0))
    else:
        assert tiles.cond_base % batch == 0
        mspec = pl.BlockSpec((batch, N_MOD, d), lambda i: (tiles.cond_base // batch, 0, 0))
    u_tm, rkvx = pl.pallas_call(
        _proj_odd_kernel,
        out_shape=(jax.ShapeDtypeStruct((W_C // LANES, seq * batch, LANES), F32),
                   jax.ShapeDtypeStruct((batch, seq, 4 * W_D), F32)),
        grid=(seq // tl,),
        in_specs=[pl.BlockSpec((batch, tl, d), lambda i: (0, i, 0)), mspec,
                  _const_spec(norm_g.shape), _const_spec(w_in.shape)],
        out_specs=(pl.BlockSpec((W_C // LANES, tl * batch, LANES), lambda i: (0, i, 0)),
                   pl.BlockSpec((batch, tl, 4 * W_D), lambda i: (0, i, 0))),
        compiler_params=_cparams(("parallel",)), name="proj_odd",
    )(x.reshape(batch, seq, d), mods, norm_g, w_in)
    return u_tm, rkvx.reshape(n, 4 * W_D)


def _ret_kernel(*refs, seq, chunk, hp, has_init):
    it = iter(refs)
    q_ref, k_ref, v_ref, g_ref, dec_ref = (next(it) for _ in range(5))
    s0_ref = next(it) if has_init else None
    y_ref, sf_ref, o_scr = next(it), next(it), next(it)
    n = seq // chunk
    dk = DK_A
    ri = lax.broadcasted_iota(jnp.int32, (chunk, dk), 0).astype(F32)
    rel = (lax.broadcasted_iota(jnp.int32, (chunk, chunk), 0)
           - lax.broadcasted_iota(jnp.int32, (chunk, chunk), 1)).astype(F32)
    reps = chunk // dk
    tabs = []
    for h in range(hp):
        lgf = _log_sigmoid(dec_ref[0, h])
        lgb = _log_sigmoid(dec_ref[1, h])
        lgf_c = jnp.concatenate([lgf] * reps, axis=1) if reps > 1 else lgf[:, :chunk]
        lgb_c = jnp.concatenate([lgb] * reps, axis=1) if reps > 1 else lgb[:, :chunk]
        tabs.append(dict(
            qf=jnp.exp(lgf * (ri + 1.0)), qb=jnp.exp(lgb * (chunk - ri)),
            kf=jnp.exp(lgf * (chunk - 1.0 - ri)), kb=jnp.exp(lgb * ri),
            dmask=(jnp.where(rel >= 0, jnp.exp(lgf_c * jnp.maximum(rel, 0.0)), 0.0)
                   + jnp.where(rel <= 0, jnp.exp(lgb_c * jnp.maximum(-rel, 0.0)), 0.0)),
            gcf=jnp.exp(lgf * float(chunk)), gcb=jnp.exp(lgb * float(chunk))))

    def load_qkv(c, h):
        rows = pl.ds(pl.multiple_of(c * chunk, chunk), chunk)
        cols = slice(h * dk, (h + 1) * dk)
        return rows, cols, q_ref[rows, cols] * (dk ** -0.5), k_ref[rows, cols], v_ref[rows, cols]

    def fwd(c, states):
        out = []
        for h, (s, t) in enumerate(zip(states, tabs)):
            rows, cols, q, k, v = load_qkv(c, h)
            scores = _dot_nt(q.astype(BF16), k.astype(BF16)) * t['dmask']
            o_scr[rows, cols] = _bdot(scores, v) + _bdot(q * t['qf'], s)
            out.append(s * t['gcf'] + _dot_tn((k * t['kf']).astype(BF16), v.astype(BF16)))
        return tuple(out)

    def bwd(j, states):
        out = []
        for h, (s, t) in enumerate(zip(states, tabs)):
            rows, cols, q, k, v = load_qkv(n - 1 - j, h)
            o = o_scr[rows, cols] + _bdot(q * t['qb'], s)
            oc = o - jnp.mean(o, axis=-1, keepdims=True)
            on = oc * lax.rsqrt(jnp.mean(oc * oc, axis=-1, keepdims=True) + HEAD_NORM_EPS)
            y_ref[rows, cols] = on * _silu(g_ref[rows, cols])
            out.append(s * t['gcb'] + _dot_tn((k * t['kb']).astype(BF16), v.astype(BF16)))
        return tuple(out)

    zero = jnp.zeros((dk, dk), F32)
    unroll = 2 if n % 2 == 0 else 1
    sf = lax.fori_loop(0, n, fwd, tuple(s0_ref[0, 0, h] if has_init else zero for h in range(hp)), unroll=unroll)
    sb = lax.fori_loop(0, n, bwd, tuple(s0_ref[0, 1, h] if has_init else zero for h in range(hp)), unroll=unroll)
    for h in range(hp):
        sf_ref[0, 0, h] = sf[h]
        sf_ref[0, 1, h] = sb[h]


def _retention(qkvg, dec, batch, seq, s0=None):
    dk = DK_A
    chunk = min(seq, 256)
    hp = H_A if seq <= 512 else 2
    ng = H_A // hp
    args = [qkvg] * 4 + [dec]
    specs = [pl.BlockSpec((seq, hp * dk), lambda b, hh, j=j: (b, j * ng + hh)) for j in range(4)]
    specs.append(pl.BlockSpec((2, hp, 1, dk), lambda b, hh: (0, hh, 0, 0)))
    sspec = pl.BlockSpec((1, 2, hp, dk, dk), lambda b, hh: (b, 0, hh, 0, 0))
    if s0 is not None:
        args.append(s0)
        specs.append(sspec)
    kern = functools.partial(_ret_kernel, seq=seq, chunk=chunk, hp=hp, has_init=s0 is not None)
    return pl.pallas_call(
        kern,
        out_shape=(jax.ShapeDtypeStruct((batch * seq, W_A), F32),
                   jax.ShapeDtypeStruct((batch, 2, H_A, dk, dk), F32)),
        grid=(batch, ng),
        in_specs=specs,
        out_specs=(pl.BlockSpec((seq, hp * dk), lambda b, hh: (b, hh)), sspec),
        scratch_shapes=[pltpu.VMEM((seq, hp * dk), F32)],
        compiler_params=_cparams(("parallel", "arbitrary")), name="retention",
    )(*args)


def _halo_specs(tiles, width, col, n_rows):
    tm = tiles.tm
    nblk = n_rows // SUBLANES
    per = tm // SUBLANES
    prev = pl.BlockSpec((SUBLANES, width), lambda i: (jnp.maximum(i * per - 1, 0), col))
    nxt = pl.BlockSpec((SUBLANES, width), lambda i: (jnp.minimum((i + 1) * per, nblk - 1), col))
    return prev, nxt


def _seq_shift(ext, s, pos, seq, tm):
    sh = pltpu.roll(ext, s % ext.shape[0], axis=0)[SUBLANES:SUBLANES + tm]
    ok = (pos >= s) if s > 0 else (pos < seq + s)
    return jnp.where(ok, sh, 0.0)


def _seq_pos(shape, tm, seq):
    return (pl.program_id(0) * tm + lax.broadcasted_iota(jnp.int32, shape, 0)) & (seq - 1)


def _lru_prep_kernel(x_ref, xp_ref, xn_ref, cw_ref, wg_ref, bg_ref, lam_ref,
                     af_ref, bf_ref, ab_ref, bb_ref, *, seq):
    x = x_ref[...]
    tm = x.shape[0]
    ext = jnp.concatenate([xp_ref[...], x, xn_ref[...]], axis=0)
    pos = _seq_pos(x.shape, tm, seq)
    xc = (cw_ref[4:5] + cw_ref[0:1] * _seq_shift(ext, 2, pos, seq, tm)
          + cw_ref[1:2] * _seq_shift(ext, 1, pos, seq, tm) + cw_ref[2:3] * x
          + cw_ref[3:4] * _seq_shift(ext, -1, pos, seq, tm))
    gates = _dot(xc.astype(BF16), wg_ref[...]) + bg_ref[...]
    w = x.shape[1]
    for d, (a_ref, b_ref) in enumerate(((af_ref, bf_ref), (ab_ref, bb_ref))):
        r = _sigmoid(gates[:, (2 * d) * w:(2 * d + 1) * w])
        i = _sigmoid(gates[:, (2 * d + 1) * w:(2 * d + 2) * w])
        log_a = LRU_C * r * _log_sigmoid(lam_ref[d:d + 1])
        a = jnp.exp(log_a)
        a_ref[...] = a
        b_ref[...] = jnp.sqrt(1.0 - a * a) * (i * xc)


def _lru_prep(tiles, gx, conv_wb, wg, bg, lam):
    n = gx.shape[0]
    prev, nxt = _halo_specs(tiles, W_B, 1, n)
    out = jax.ShapeDtypeStruct((n, W_B), F32)
    return pl.pallas_call(
        functools.partial(_lru_prep_kernel, seq=tiles.seq),
        out_shape=(out,) * 4, grid=(tiles.n_tiles,),
        in_specs=[tiles.row_spec(W_B, 1), prev, nxt, _const_spec(conv_wb.shape), _const_spec(wg.shape),
                  _const_spec(bg.shape), _const_spec(lam.shape)],
        out_specs=(tiles.row_spec(W_B),) * 4,
        compiler_params=_cparams(("parallel",)), name="lru_prep",
    )(gx, gx, gx, conv_wb, wg, bg, lam)


def _lru_scan_kernel(*refs, nbt, tc, has_init):
    it = iter(refs)
    af_ref, bf_ref, ab_ref, bb_ref = (next(it) for _ in range(4))
    h0_ref = next(it) if has_init else None
    hf_ref, hb_ref, fin_ref, st_ref = (next(it) for _ in range(4))
    c = pl.program_id(1)

    @pl.when(c == 0)
    def _():
        st_ref[...] = h0_ref[...] if has_init else jnp.zeros(st_ref.shape, F32)

    def body(t, carry):
        tb = tc - 1 - t
        new = []
        for j in range(nbt):
            hf, hb = carry[2 * j], carry[2 * j + 1]
            hf = af_ref[j, pl.ds(t, 1), :] * hf + bf_ref[j, pl.ds(t, 1), :]
            hb = ab_ref[j, pl.ds(tb, 1), :] * hb + bb_ref[j, pl.ds(tb, 1), :]
            hf_ref[j, pl.ds(t, 1), :] = hf
            hb_ref[j, pl.ds(tb, 1), :] = hb
            new += [hf, hb]
        return tuple(new)

    init = tuple(st_ref[j, d:d + 1, :] for j in range(nbt) for d in range(2))
    out = lax.fori_loop(0, tc, body, init)
    for j in range(nbt):
        for d in range(2):
            st_ref[j, d:d + 1, :] = out[2 * j + d]
    fin_ref[...] = st_ref[...]


def _lru_scan(af, bf, ab, bb, batch, seq, h0=None):
    nbt = min(batch, 4)
    tc = min(seq, 256)
    n = seq // tc
    shp = (batch, seq, W_B)
    fspec = pl.BlockSpec((nbt, tc, W_B), lambda g, c: (g, c, 0))
    bspec = pl.BlockSpec((nbt, tc, W_B), lambda g, c: (g, n - 1 - c, 0))
    sspec = pl.BlockSpec((nbt, 2, W_B), lambda g, c: (g, 0, 0))
    args = [af.reshape(shp), bf.reshape(shp), ab.reshape(shp), bb.reshape(shp)]
    specs = [fspec, fspec, bspec, bspec]
    if h0 is not None:
        args.append(h0)
        specs.append(sspec)
    hf, hb, fin = pl.pallas_call(
        functools.partial(_lru_scan_kernel, nbt=nbt, tc=tc, has_init=h0 is not None),
        out_shape=(jax.ShapeDtypeStruct(shp, F32), jax.ShapeDtypeStruct(shp, F32),
                   jax.ShapeDtypeStruct((batch, 2, W_B), F32)),
        grid=(batch // nbt, n), in_specs=specs, out_specs=(fspec, bspec, sspec),
        scratch_shapes=[pltpu.VMEM((nbt, 2, W_B), F32)],
        compiler_params=_cparams(("parallel", "arbitrary")), name="lru_scan",
    )(*args)
    return hf.reshape(batch * seq, W_B), hb.reshape(batch * seq, W_B), fin


S5_BLOCKS = 2


def _s5_param_kernel(prm_ref, bre_ref, bim_ref, bbar_ref, abar_ref):
    a_re, a_im, ldt = prm_ref[0, 0:1], prm_ref[0, 1:2], prm_ref[0, 2:3]
    dt = jnp.exp(ldt)
    mag = jnp.exp(a_re * dt)
    abr = mag * jnp.cos(a_im * dt)
    abi = mag * jnp.sin(a_im * dt)
    den = a_re * a_re + a_im * a_im
    fr = ((abr - 1.0) * a_re + abi * a_im) / den
    fi = (abi * a_re - (abr - 1.0) * a_im) / den
    bre, bim = bre_ref[0, 0], bim_ref[0, 0]
    ns = bre.shape[1]
    bbar_ref[0, 0, :, 0:ns] = (fr * bre - fi * bim).astype(BF16)
    bbar_ref[0, 0, :, ns:] = (fr * bim + fi * bre).astype(BF16)
    abar_ref[0, 0:1] = abr
    abar_ref[0, 1:2] = abi


def _s5_params(prm, bre, bim):
    gn = prm.shape[2]
    _, nblk, cb, ns = bre.shape
    return pl.pallas_call(
        _s5_param_kernel,
        out_shape=(jax.ShapeDtypeStruct((2, nblk, cb, 2 * ns), BF16), jax.ShapeDtypeStruct((2, 2, gn), F32)),
        grid=(2, nblk),
        in_specs=[pl.BlockSpec((1, 3, ns), lambda d, j: (d, 0, j)),
                  pl.BlockSpec((1, 1, cb, ns), lambda d, j: (d, j, 0, 0)),
                  pl.BlockSpec((1, 1, cb, ns), lambda d, j: (d, j, 0, 0))],
        out_specs=(pl.BlockSpec((1, 1, cb, 2 * ns), lambda d, j: (d, j, 0, 0)),
                   pl.BlockSpec((1, 2, ns), lambda d, j: (d, 0, j))),
        compiler_params=_cparams(("arbitrary", "arbitrary")), name="s5_params",
    )(prm, bre, bim)


def _s5_scan_kernel(*refs, bp, tc, cw, has_init):
    it = iter(refs)
    uf_ref, ub_ref, bbar_ref, cc_ref, abar_ref = (next(it) for _ in range(5))
    s0_ref = next(it) if has_init else None
    yf_ref, yb_ref, fin_ref, buf_f, buf_b, st_ref = (next(it) for _ in range(6))
    c = pl.program_id(0)
    gn = abar_ref.shape[2]
    nblk, cb = bbar_ref.shape[1], bbar_ref.shape[2]
    ns = gn // nblk
    tr = st_ref.shape[1]
    sub = tr // bp
    n_it = tc // sub

    @pl.when(c == 0)
    def _():
        st_ref[...] = s0_ref[...] if has_init else jnp.zeros(st_ref.shape, F32)

    for d, (u_ref, buf) in enumerate(((uf_ref, buf_f), (ub_ref, buf_b))):
        u = jnp.concatenate([u_ref[j] for j in range(u_ref.shape[0])], axis=1).astype(BF16)
        for j in range(nblk):
            bu = _dot(u[:, j * cb:(j + 1) * cb], bbar_ref[d, j])
            buf[:, j * ns:(j + 1) * ns] = bu[:, :ns]
            buf[:, gn + j * ns:gn + (j + 1) * ns] = bu[:, ns:]

    first_f = lax.broadcasted_iota(jnp.int32, (tr, cw), 0) < bp

    def cmul_add(ar, ai, hr, hi, xr, xi):
        return ar * hr - ai * hi + xr, ar * hi + ai * hr + xi

    for kb in range(gn // cw):
        re = slice(kb * cw, (kb + 1) * cw)
        im = slice(gn + kb * cw, gn + (kb + 1) * cw)
        coef = []
        for d in range(2):
            coef.append((jnp.broadcast_to(abar_ref[d, 0:1, re], (tr, cw)),
                         jnp.broadcast_to(abar_ref[d, 1:2, re], (tr, cw))))

        def body(i, carry, re=re, im=im, coef=coef):
            out = []
            for d, buf in enumerate((buf_f, buf_b)):
                hr, hi = carry[2 * d], carry[2 * d + 1]
                ar, ai = coef[d]
                tile = i if d == 0 else n_it - 1 - i
                rows = pl.ds(pl.multiple_of(tile * tr, tr), tr)
                xr, xi = buf[rows, re], buf[rows, im]
                r1, i1 = cmul_add(ar, ai, hr, hi, xr, xi)
                if sub == 1:
                    outr, outi, nr, ni = r1, i1, r1, i1
                else:
                    first = first_f if d == 0 else jnp.logical_not(first_f)
                    r2, i2 = cmul_add(ar, ai, pltpu.roll(r1, bp, axis=0), pltpu.roll(i1, bp, axis=0), xr, xi)
                    outr, outi = jnp.where(first, r1, r2), jnp.where(first, i1, i2)
                    nr = jnp.where(first, pltpu.roll(r2, bp, axis=0), r2)
                    ni = jnp.where(first, pltpu.roll(i2, bp, axis=0), i2)
                buf[rows, re] = outr
                buf[rows, im] = outi
                out += [nr, ni]
            return tuple(out)

        init = (st_ref[0, :, re], st_ref[0, :, im], st_ref[1, :, re], st_ref[1, :, im])
        fr_, fi_, br_, bi_ = lax.fori_loop(0, n_it, body, init)
        st_ref[0, :, re] = fr_
        st_ref[0, :, im] = fi_
        st_ref[1, :, re] = br_
        st_ref[1, :, im] = bi_

    for d, (y_ref, buf) in enumerate(((yf_ref, buf_f), (yb_ref, buf_b))):
        for j in range(nblk):
            hre = buf[:, j * ns:(j + 1) * ns].astype(BF16)
            him = buf[:, gn + j * ns:gn + (j + 1) * ns].astype(BF16)
            yj = _dot(hre, cc_ref[d, j, 0:ns, :]) + _dot(him, cc_ref[d, j, ns:, :])
            for k in range(cb // LANES):
                y_ref[j * (cb // LANES) + k] = yj[:, k * LANES:(k + 1) * LANES]
    fin_ref[...] = st_ref[...]


def _s5_scan(u_tm, bbar, cc, abar, bp, seq, s0=None):
    assert bp % SUBLANES == 0 or 2 * bp == SUBLANES
    tr = max(bp, SUBLANES)
    rows = min(512, seq * bp)
    tc = rows // bp
    n = seq // tc
    cw = 128 if bp > SUBLANES else 256
    ng = W_C // LANES
    fspec = pl.BlockSpec((ng, rows, LANES), lambda c: (0, c, 0))
    bspec = pl.BlockSpec((ng, rows, LANES), lambda c: (0, n - 1 - c, 0))
    args = [u_tm, u_tm, bbar, cc, abar]
    specs = [fspec, bspec, _const_spec(bbar.shape), _const_spec(cc.shape), _const_spec(abar.shape)]
    if s0 is not None:
        args.append(s0)
        specs.append(_const_spec(s0.shape))
    return pl.pallas_call(
        functools.partial(_s5_scan_kernel, bp=bp, tc=tc, cw=cw, has_init=s0 is not None),
        out_shape=(jax.ShapeDtypeStruct(u_tm.shape, F32), jax.ShapeDtypeStruct(u_tm.shape, F32),
                   jax.ShapeDtypeStruct((2, tr, S5_STATE), F32)),
        grid=(n,), in_specs=specs,
        out_specs=(fspec, bspec, _const_spec((2, tr, S5_STATE))),
        scratch_shapes=[pltpu.VMEM((rows, S5_STATE), F32), pltpu.VMEM((rows, S5_STATE), F32),
                        pltpu.VMEM((2, tr, S5_STATE), F32)],
        compiler_params=_cparams(("arbitrary",)), name="s5_scan",
    )(*args)


def _s5_post_kernel(yf_ref, yb_ref, u_ref, d_ref, w_ref, b_ref, o_ref):
    nb, tl, _ = o_ref.shape
    zs = []
    for b in range(nb):
        rows = pl.ds(b, tl, stride=nb)
        yf, yb, u = (jnp.concatenate([ref[j, rows, :] for j in range(ref.shape[0])], axis=1)
                     for ref in (yf_ref, yb_ref, u_ref))
        zs.append(_gelu_tanh(yf + yb + d_ref[...] * u))
    z = jnp.concatenate(zs, axis=0)
    y = z * _sigmoid(_dot(z.astype(BF16), w_ref[...]) + b_ref[...])
    for b in range(nb):
        o_ref[b] = y[b * tl:(b + 1) * tl]


def _s5_post(yf, yb, u_tm, s5_d, glu_w, glu_b, batch, seq):
    tl = _time_tile(batch, seq)
    spec = pl.BlockSpec((W_C // LANES, tl * batch, LANES), lambda i: (0, i, 0))
    return pl.pallas_call(
        _s5_post_kernel, out_shape=jax.ShapeDtypeStruct((batch, seq, W_C), F32), grid=(seq // tl,),
        in_specs=[spec, spec, spec, _const_spec(s5_d.shape), _const_spec(glu_w.shape), _const_spec(glu_b.shape)],
        out_specs=pl.BlockSpec((batch, tl, W_C), lambda i: (0, i, 0)),
        compiler_params=_cparams(("parallel",)), name="s5_post",
    )(yf, yb, u_tm, s5_d, glu_w, glu_b).reshape(batch * seq, W_C)


def _rwkv_prep_kernel(x_ref, xp_ref, xn_ref, vec_ref, w1_ref, w2_ref, a1_ref, a2_ref, g1_ref, g2_ref,
                      ones_ref, *out_refs, seq):
    (g_ref, bon_ref, v_ref, kk_ref, r_ref) = out_refs[:5]
    dir_refs = (out_refs[5:8], out_refs[8:11])

    def put_pairs(ref, val):
        for q in range(ref.shape[0]):
            ref[q] = val[:, q * RWKV_PAIR:(q + 1) * RWKV_PAIR]

    x = x_ref[...]
    tm = x.shape[0]
    ext = jnp.concatenate([xp_ref[...], x, xn_ref[...]], axis=0)
    pos = _seq_pos(x.shape, tm, seq)
    dx = 0.5 * (_seq_shift(ext, 1, pos, seq, tm) + _seq_shift(ext, -1, pos, seq, tm)) - x
    w = W_D
    r, k, v, xd = (x[:, j * w:(j + 1) * w] for j in range(4))
    dr, dk_, dv, dxd = (dx[:, j * w:(j + 1) * w] for j in range(4))
    vec = vec_ref[...]
    mu = vec[0:6]
    r = r + dr * mu[0:1]
    k = k + dk_ * mu[1:2]
    v = v + dv * mu[2:3]
    xw = xd + dxd * mu[3:4]
    xa = xd + dxd * mu[4:5]
    xg = xd + dxd * mu[5:6]
    rw_kk, rw_ka, rw_rk = vec[6:7], vec[7:8], vec[8:9]
    ones = ones_ref[...]
    g_ref[...] = _bdot(_sigmoid(_bdot(xg, g1_ref[...])), g2_ref[...])
    kkr = k * rw_kk
    kk = kkr / jnp.maximum(jnp.sqrt(_bdot(kkr * kkr, ones)), 1e-12)
    put_pairs(v_ref, v)
    put_pairs(kk_ref, kk)
    put_pairs(r_ref, r)
    bonus = jnp.zeros(v.shape, F32)
    for d in range(2):
        w0, a0 = vec[9 + d:10 + d], vec[11 + d:12 + d]
        wl = -_softplus(-(w0 + _bdot(jnp.tanh(_bdot(xw, w1_ref[d])), w2_ref[d]))) - 0.5
        a = _sigmoid(a0 + _bdot(_bdot(xa, a1_ref[d]), a2_ref[d]))
        kd = k * (1.0 + (a - 1.0) * rw_ka)
        lw_ref, ka_ref, kd_ref = dir_refs[d]
        put_pairs(lw_ref, -jnp.exp(wl))
        put_pairs(ka_ref, kk * a)
        put_pairs(kd_ref, kd)
        bonus = bonus + _bdot(r * kd * rw_rk, ones) * v
    bon_ref[...] = bonus


N_RWKV_SEQS = 9


def _rwkv_prep(tiles, rkvx, vec, w1, w2, a1, a2, g1, g2, ones):
    n = rkvx.shape[0]
    npair = W_D // RWKV_PAIR
    prev, nxt = _halo_specs(tiles, 4 * W_D, 0, n)
    flat = jax.ShapeDtypeStruct((n, W_D), F32)
    paired = jax.ShapeDtypeStruct((npair, n, RWKV_PAIR), F32)
    pspec = pl.BlockSpec((npair, tiles.tm, RWKV_PAIR), lambda i: (0, i, 0))
    consts = [vec, w1, w2, a1, a2, g1, g2, ones]
    return pl.pallas_call(
        functools.partial(_rwkv_prep_kernel, seq=tiles.seq),
        out_shape=(flat,) * 2 + (paired,) * N_RWKV_SEQS, grid=(tiles.n_tiles,),
        in_specs=[tiles.row_spec(4 * W_D), prev, nxt] + [_const_spec(c.shape) for c in consts],
        out_specs=(tiles.row_spec(W_D),) * 2 + (pspec,) * N_RWKV_SEQS,
        compiler_params=_cparams(("parallel",)), name="rwkv_prep",
    )(rkvx, rkvx, rkvx, *consts)


def _bmm(a, b, precision=None):
    return jnp.einsum('umk,ukn->umn', a, b, precision=precision, preferred_element_type=F32)


def _bmm_nt(a, b):
    return jnp.einsum('umk,unk->umn', a, b, preferred_element_type=F32)


def _bmm_tn(a, b):
    return jnp.stack([_dot_tn(a[u], b[u]) for u in range(a.shape[0])], axis=0)


def _bf(a):
    return a.astype(BF16)


def _rwkv_masks(t, nbt):
    n2 = 2 * t
    ri = lax.broadcasted_iota(jnp.int32, (n2, n2), 0)
    ci = lax.broadcasted_iota(jnp.int32, (n2, n2), 1)
    same_head = (ri >= t) == (ci >= t)
    tt, ss = ri & (t - 1), ci & (t - 1)
    ti = lax.broadcasted_iota(jnp.int32, (t, t), 0)
    si = lax.broadcasted_iota(jnp.int32, (t, t), 1)

    def per_unit(fwd, bwd, dtype):
        return jnp.concatenate([jnp.broadcast_to(m.astype(dtype)[None], (nbt,) + m.shape) for m in (fwd, bwd)],
                               axis=0)

    return dict(
        strict=per_unit(same_head & (ss < tt), same_head & (ss > tt), F32),
        incl=per_unit(same_head & (ss <= tt), same_head & (ss >= tt), F32),
        tri=per_unit(si <= ti, si >= ti, BF16),
        same16=(ri >> 4) == (ci >> 4),
        off16=((ri >> 5) == (ci >> 5)) & ((ri >> 4) != (ci >> 4)),
        off32=(ri >> 5) != (ci >> 5),
        eye=(ri == ci).astype(F32),
        head0=lax.broadcasted_iota(jnp.int32, (t, RWKV_PAIR), 1) < HD_D,
    )


def _split(a):
    hi = a.astype(BF16)
    return hi, (a - hi.astype(F32)).astype(BF16)


def _bmm3(a, b):
    (ah, al), (bh, bl) = a, b
    n = bh.shape[2]
    lhs = jnp.concatenate([ah, al], axis=2)
    rhs = jnp.concatenate([jnp.concatenate([bh, bl], axis=2),
                           jnp.concatenate([bh, jnp.zeros_like(bl)], axis=2)], axis=1)
    out = _bmm(lhs, rhs)
    return out[:, :, :n] + out[:, :, n:]


def _unit_tri_inverse(a, mk):
    n1f = -jnp.where(mk['same16'], a, 0.0)
    n1 = _split(n1f)
    n2 = _split(_bmm3(n1, n1))
    n4 = _split(_bmm3(n2, n2))
    n8 = _split(_bmm3(n4, n4))
    p = mk['eye'] + n1f
    p = p + _bmm3(_split(p), n2)
    p = p + _bmm3(_split(p), n4)
    p = p + _bmm3(_split(p), n8)
    for off in ('off16', 'off32'):
        ps = _split(p)
        p = p - _bmm3(_split(_bmm3(ps, _split(jnp.where(mk[off], a, 0.0)))), ps)
    return _split(p)


def _rwkv_unit(r, v, kap, lw, beta, kd, s, mk):
    u, t, _ = r.shape
    lw_hi = lw.astype(BF16)
    lw_rest = lw - lw_hi.astype(F32)
    lw_mid = lw_rest.astype(BF16)
    lw_lo = (lw_rest - lw_mid.astype(F32)).astype(BF16)
    c3 = _bmm(mk['tri'], jnp.concatenate([lw_hi, lw_mid, lw_lo], axis=2))
    w = lw.shape[2]
    c = c3[:, :, :w] + c3[:, :, w:2 * w] + c3[:, :, 2 * w:]
    c_end = jnp.concatenate([c[:u // 2, t - 1:t], c[u // 2:, 0:1]], axis=0)
    e_neg = jnp.exp(-c)
    e_end = jnp.exp(c_end - c)
    head0 = mk['head0']

    def stack(x):
        return jnp.concatenate([jnp.where(head0, x, 0.0), jnp.where(head0, 0.0, x)], axis=1)

    rs = stack(r * jnp.exp(c))
    ks = stack(kap * jnp.exp(c - lw))
    bs = stack(beta * e_neg)
    kds = stack(kd * e_neg)
    vs = stack(v)
    bps = stack(beta * e_end)
    kps = stack(kd * e_end)
    n2 = 2 * t
    g = _bmm_nt(_bf(jnp.concatenate([ks, rs], axis=1)), _bf(jnp.concatenate([bs, kds], axis=1)))
    a_ab = g[:, :n2, :n2] * mk['strict']
    a_ak = g[:, :n2, n2:] * mk['strict']
    a_rb = g[:, n2:, :n2] * mk['incl']
    a_rk = g[:, n2:, n2:] * mk['incl']
    m = _unit_tri_inverse(a_ab, mk)
    vsb = _bf(vs)
    sol = _bmm3(m, _split(jnp.concatenate([ks, _bmm(_bf(a_ak), vsb)], axis=2)))
    wy = jnp.concatenate([rs, _bmm(_bf(a_rk), vsb)], axis=2) - _bmm(_bf(a_rb), _bf(sol))
    wy = wy[:, :t] + wy[:, t:]
    pg = _bmm_tn(_bf(sol), _bf(bps))
    gam = _bmm_tn(vsb, _bf(kps)) - pg[:, RWKV_PAIR:]
    y = _bmm_nt(_bf(wy[:, :, :RWKV_PAIR]), _bf(s)) + wy[:, :, RWKV_PAIR:]
    s_new = s * jnp.exp(c_end) - _bmm(_bf(s), _bf(pg[:, :RWKV_PAIR])) + gam
    return y, s_new


def _rwkv_chunk_kernel(*refs, nbt, has_init):
    it = iter(refs)
    ins = [tuple(next(it) for _ in range(6)) for _ in range(2)]
    s0_ref = next(it) if has_init else None
    yf_ref, yb_ref, fin_ref, st_ref = (next(it) for _ in range(4))
    c = pl.program_id(2)

    @pl.when(c == 0)
    def _():
        for d in range(2):
            st_ref[d * nbt:(d + 1) * nbt] = (s0_ref[:, d, 0] if has_init
                                             else jnp.zeros((nbt,) + st_ref.shape[1:], F32))

    t = RWKV_CHUNK
    nsub = yf_ref.shape[2] // t
    mk = _rwkv_masks(t, nbt)

    def chunk(j, carry):
        rows_f = pl.ds(pl.multiple_of(j * t, t), t)
        rows_b = pl.ds(pl.multiple_of((nsub - 1 - j) * t, t), t)
        ops = [jnp.concatenate([f[0, :, rows_f, :], b[0, :, rows_b, :]], axis=0) for f, b in zip(*ins)]
        y, s_new = _rwkv_unit(*ops, st_ref[...], mk)
        yf_ref[0, :, rows_f, :] = y[:nbt]
        yb_ref[0, :, rows_b, :] = y[nbt:]
        st_ref[...] = s_new
        return carry

    lax.fori_loop(0, nsub, chunk, 0)
    for d in range(2):
        for h in range(2):
            fin_ref[:, d, h] = st_ref[d * nbt:(d + 1) * nbt, h * HD_D:(h + 1) * HD_D, h * HD_D:(h + 1) * HD_D]


def _rwkv_scan(shared, fdir, bdir, batch, seq, s0=None):
    nbt = min(batch, 4)
    tb = min(seq, 4 * RWKV_CHUNK)
    n = seq // tb
    npair = W_D // RWKV_PAIR
    shp = (npair, batch, seq, RWKV_PAIR)
    fspec = pl.BlockSpec((1, nbt, tb, RWKV_PAIR), lambda g, q, c: (q, g, c, 0))
    bspec = pl.BlockSpec((1, nbt, tb, RWKV_PAIR), lambda g, q, c: (q, g, n - 1 - c, 0))
    sspec = pl.BlockSpec((nbt, 2, 1, RWKV_PAIR, RWKV_PAIR), lambda g, q, c: (g, 0, q, 0, 0))
    args = [a.reshape(shp) for a in tuple(shared) + tuple(fdir) + tuple(shared) + tuple(bdir)]
    specs = [fspec] * 6 + [bspec] * 6
    if s0 is not None:
        args.append(s0)
        specs.append(sspec)
    yf, yb, fin = pl.pallas_call(
        functools.partial(_rwkv_chunk_kernel, nbt=nbt, has_init=s0 is not None),
        out_shape=(jax.ShapeDtypeStruct(shp, F32), jax.ShapeDtypeStruct(shp, F32),
                   jax.ShapeDtypeStruct((batch, 2, H_D, HD_D, HD_D), F32)),
        grid=(batch // nbt, npair, n), in_specs=specs,
        out_specs=(fspec, bspec, pl.BlockSpec((nbt, 2, 2, HD_D, HD_D), lambda g, q, c: (g, 0, q, 0, 0))),
        scratch_shapes=[pltpu.VMEM((2 * nbt, RWKV_PAIR, RWKV_PAIR), F32)],
        compiler_params=_cparams(("parallel", "parallel", "arbitrary")), name="rwkv_scan",
    )(*args)
    pshape = (npair, batch * seq, RWKV_PAIR)
    return yf.reshape(pshape), yb.reshape(pshape), fin


def _block_diag(w):
    n, a, b = w.shape
    return jnp.einsum('nab,nm->namb', w, jnp.eye(n, dtype=w.dtype)).reshape(n * a, n * b)


def _rope_tables(seq, dk):
    t = jnp.arange(seq)
    row = (t // GRID_W).astype(F32)
    col = (t % GRID_W).astype(F32)
    nf = dk // 4
    freqs = ROPE_BASE ** (-jnp.arange(nf, dtype=F32) / nf)
    ar, ac = row[:, None] * freqs, col[:, None] * freqs
    cos = jnp.concatenate([jnp.cos(ar), jnp.cos(ar), jnp.cos(ac), jnp.cos(ac)], axis=1)
    sin = jnp.concatenate([-jnp.sin(ar), jnp.sin(ar), -jnp.sin(ac), jnp.sin(ac)], axis=1)
    return cos, sin


def _trunk(x3, cond_base, per_batch_cond, is_latent, states, mods, p):
    batch, seq, d = x3.shape
    tiles = _Tiles(batch, seq, cond_base, per_batch_cond)
    x = x3.reshape(batch * seq, d)
    s_ret, s_lru, s_s5, s_rwkv = states

    m0, g0 = mods[0], p['norm_g'][0]
    x = _ffn(tiles, x, m0, g0, p['f1a'][0], p['f1b'][0], first=True)
    rope = _rope_tables(seq, DK_A) if is_latent else None
    qkvg, gx = _proj_even(tiles, x, m0, g0, p['l0_w_in'], rope_tabs=rope)
    y_a, new_ret = _retention(qkvg, p['ret_dec'], batch, seq, s0=s_ret)
    af, bf, ab, bb = _lru_prep(tiles, gx, p['conv_wb'], p['lru_wg'], p['lru_bg'], p['lru_lam'])
    hf, hb, new_lru = _lru_scan(af, bf, ab, bb, batch, seq, h0=s_lru)
    x = _ffn(tiles, x, m0, g0, p['f2a'][0], p['f2b'][0], first=False, mix="even",
             mix_args=(y_a, gx, hf, hb), wo=p['l0_w_out'])

    m1, g1 = mods[1], p['norm_g'][1]
    x = _ffn(tiles, x, m1, g1, p['f1a'][1], p['f1b'][1], first=True)
    u_tm, rkvx = _proj_odd(tiles, x, m1, g1, p['l1_w_in'])
    bp = batch
    s5_0 = None
    if s_s5 is not None:
        s5_0 = jnp.transpose(s_s5.reshape(batch, 2, S5_STATE), (1, 0, 2))
        s5_0 = jnp.concatenate([s5_0] * (max(bp, SUBLANES) // bp), axis=1)
    yf5, yb5, fin5 = _s5_scan(u_tm, p['s5_bbar'], p['s5_cc'], p['s5_abar'], bp, seq, s0=s5_0)
    yc = _s5_post(yf5, yb5, u_tm, p['s5_d'], p['glu_w'], p['glu_b'], batch, seq)
    new_s5 = jnp.transpose(fin5[:, :batch], (1, 0, 2)).reshape(batch, 2, 2, G_C, N_C)
    outs = _rwkv_prep(tiles, rkvx, p['rw_vec'], p['rw_w1'], p['rw_w2'], p['rw_a1'], p['rw_a2'],
                      p['rw_g1'], p['rw_g2'], p['ones_hd'])
    gg, bonus, v, kk, r = outs[:5]
    npair = H_D // 2
    eye2 = jnp.eye(2, dtype=F32)
    rw0 = None
    if s_rwkv is not None:
        rw0 = jnp.einsum('bdqjvk,ji->bdqjvik', s_rwkv.reshape(batch, 2, npair, 2, HD_D, HD_D), eye2)
        rw0 = rw0.reshape(batch, 2, npair, RWKV_PAIR, RWKV_PAIR)
    yfr, ybr, new_rwkv = _rwkv_scan((r, v, kk), outs[5:8], outs[8:11], batch, seq, s0=rw0)
    x = _ffn(tiles, x, m1, g1, p['f2a'][1], p['f2b'][1], first=False, mix="odd",
             mix_args=(yc, yfr, ybr, bonus, gg, p['avg_hd'], p['ln_wb']), wo=p['l1_w_out'],
             final_g=p['final_norm'])
    return x.reshape(batch, seq, d), (new_ret, new_lru, new_s5, new_rwkv)


def kernel(x_prompt, x_sample, state_l0_ret, state_l0_lru, state_l1_s5, state_l1_rwkv, c, c_ctx, mod_w, mod_b, norm_g, ffn1_w13, ffn1_w2, ffn2_w13, ffn2_w2, final_norm, l0_w_in, l0_w_out, l0_ret_decay, l0_conv_w, l0_conv_b, l0_lru_lam, l0_lru_wa, l0_lru_ba, l0_lru_wx, l0_lru_bx, l1_w_in, l1_w_out, l1_s5_a_re, l1_s5_a_im, l1_s5_log_dt, l1_s5_b_re, l1_s5_b_im, l1_s5_c_re, l1_s5_c_im, l1_s5_d, l1_glu_w, l1_glu_b, l1_rw_mu, l1_rw_w0, l1_rw_w1, l1_rw_w2, l1_rw_a0, l1_rw_a1, l1_rw_a2, l1_rw_g1, l1_rw_g2, l1_rw_kk, l1_rw_ka, l1_rw_rk, l1_ln_w, l1_ln_b):
    d = x_prompt.shape[-1]
    nlat = c.shape[0]
    cond8 = jnp.concatenate([c_ctx[None, :], jnp.zeros((SUBLANES - 1 - nlat, d), F32), c], axis=0)
    mods = _modulation(cond8, mod_w, mod_b)

    gn = G_C * N_C
    gb = G_C // S5_BLOCKS
    eye_g = jnp.eye(gb, dtype=F32)

    def embed(w, spec):
        w = w.reshape((2, S5_BLOCKS, gb) + w.shape[2:])
        return jnp.einsum(spec, w, eye_g)

    bre = embed(l1_s5_b_re, 'djgns,gh->djgshn').reshape(2, S5_BLOCKS, gb * GS_C, gb * N_C)
    bim = embed(l1_s5_b_im, 'djgns,gh->djgshn').reshape(2, S5_BLOCKS, gb * GS_C, gb * N_C)
    cre = embed(l1_s5_c_re, 'djgsn,gh->djgnhs').reshape(2, S5_BLOCKS, gb * N_C, gb * GS_C)
    cim = embed(l1_s5_c_im, 'djgsn,gh->djgnhs').reshape(2, S5_BLOCKS, gb * N_C, gb * GS_C)
    prm = jnp.stack([l1_s5_a_re.reshape(2, gn), l1_s5_a_im.reshape(2, gn),
                     jnp.repeat(l1_s5_log_dt, N_C, axis=1)], axis=1)
    s5_bbar, s5_abar = _s5_params(prm, bre, bim)
    head_id = jnp.arange(W_D) // HD_D
    ones_hd = (head_id[:, None] == head_id[None, :]).astype(BF16)
    p = dict(
        norm_g=norm_g, final_norm=final_norm.reshape(1, d),
        f1a=ffn1_w13.astype(BF16), f1b=ffn1_w2.astype(BF16),
        f2a=ffn2_w13.astype(BF16), f2b=ffn2_w2.astype(BF16),
        l0_w_in=l0_w_in.astype(BF16), l0_w_out=l0_w_out.astype(BF16),
        l1_w_in=l1_w_in.astype(BF16), l1_w_out=l1_w_out.astype(BF16),
        ret_dec=jnp.broadcast_to(l0_ret_decay.astype(F32)[:, :, None, None], (2, H_A, 1, DK_A)),
        conv_wb=jnp.concatenate([l0_conv_w, l0_conv_b[None, :]], axis=0),
        lru_wg=jnp.concatenate([_block_diag(l0_lru_wa[0]), _block_diag(l0_lru_wx[0]),
                                _block_diag(l0_lru_wa[1]), _block_diag(l0_lru_wx[1])], axis=1).astype(BF16),
        lru_bg=jnp.concatenate([l0_lru_ba[0], l0_lru_bx[0], l0_lru_ba[1], l0_lru_bx[1]])[None, :],
        lru_lam=l0_lru_lam,
        s5_bbar=s5_bbar, s5_abar=s5_abar,
        s5_cc=jnp.concatenate([cre, -cim], axis=2).astype(BF16),
        s5_d=l1_s5_d.reshape(1, W_C), glu_w=l1_glu_w.astype(BF16), glu_b=l1_glu_b.reshape(1, W_C),
        rw_vec=jnp.concatenate([l1_rw_mu, l1_rw_kk[None], l1_rw_ka[None], l1_rw_rk[None],
                                l1_rw_w0, l1_rw_a0, jnp.zeros((3, W_D), F32)], axis=0),
        rw_w1=l1_rw_w1.astype(BF16), rw_w2=l1_rw_w2.astype(BF16),
        rw_a1=l1_rw_a1.astype(BF16), rw_a2=l1_rw_a2.astype(BF16),
        rw_g1=l1_rw_g1.astype(BF16), rw_g2=l1_rw_g2.astype(BF16),
        ones_hd=ones_hd, avg_hd=(ones_hd.astype(F32) / HD_D).astype(BF16),
        ln_wb=jnp.stack([l1_ln_w, l1_ln_b], axis=0),
    )
    y_prompt, ctx_states = _trunk(x_prompt, 0, False, False, (None, None, None, None), mods, p)
    y_sample, _ = _trunk(x_sample, SUBLANES - nlat, True, True,
                         (state_l0_ret, state_l0_lru, state_l1_s5, state_l1_rwkv), mods, p)
    return (y_prompt, y_sample) + ctx_states
```

```python
import functools
import math

import jax
import jax.numpy as jnp
from jax import lax
from jax.experimental import pallas as pl
from jax.experimental.pallas import tpu as pltpu

F32 = jnp.float32
BF16 = jnp.bfloat16

N_MOD = 9
NORM_EPS = 1e-6
HEAD_NORM_EPS = 1e-5
RWKV_LN_EPS = 64e-5
ROPE_BASE = 10000.0
GRID_W = 64
H_A = 4
DK_A = 128
W_A = 512
W_B = 512
NB_B = 8
LRU_C = 8.0
W_C = 512
GS_C = 16
G_C = 32
N_C = 64
W_D = 512
HD_D = 64
H_D = 8
S5_STATE = 2 * G_C * N_C
RWKV_CHUNK = 64
RWKV_PAIR = 2 * HD_D

SUBLANES = 8
LANES = 128
VMEM_LIMIT = 56 * 1024 * 1024
ROW_TILE = 512


def _cparams(sem):
    return pltpu.CompilerParams(dimension_semantics=sem, vmem_limit_bytes=VMEM_LIMIT)


def _dot(a, b):
    return jnp.dot(a, b, preferred_element_type=F32)


def _dot_nt(a, b):
    return lax.dot_general(a, b, (((1,), (1,)), ((), ())), preferred_element_type=F32)


def _dot_tn(a, b):
    return lax.dot_general(a, b, (((0,), (0,)), ((), ())), preferred_element_type=F32)


def _bdot(a, b):
    return _dot(a.astype(BF16), b.astype(BF16))


def _sigmoid(x):
    return 1.0 / (1.0 + jnp.exp(-x))


def _silu(x):
    return x * _sigmoid(x)


def _softplus(x):
    return jnp.maximum(x, 0.0) + jnp.log1p(jnp.exp(-jnp.abs(x)))


def _log_sigmoid(x):
    return -_softplus(-x)


def _gelu_tanh(x):
    return 0.5 * x * (1.0 + jnp.tanh(math.sqrt(2.0 / math.pi) * (x + 0.044715 * (x * x * x))))


def _rms(x):
    return x * lax.rsqrt(jnp.mean(x * x, axis=-1, keepdims=True) + NORM_EPS)


def _rms_mod(x, g, scale, shift):
    return (_rms(x) * g) * (1.0 + scale) + shift


def _const_spec(shape):
    nd = len(shape)
    return pl.BlockSpec(shape, lambda *_: (0,) * nd, pipeline_mode=pl.Buffered(1))


def _mod_kernel(c_ref, w_ref, b_ref, o_ref):
    o_ref[0] = _bdot(_silu(c_ref[...]), w_ref[0]) + b_ref[0]


def _modulation(cond8, mod_w, mod_b):
    depth, d, nd = mod_w.shape
    return pl.pallas_call(
        _mod_kernel,
        out_shape=jax.ShapeDtypeStruct((depth, SUBLANES, nd), F32),
        grid=(depth, nd // d),
        in_specs=[pl.BlockSpec((SUBLANES, d), lambda l, j: (0, 0)),
                  pl.BlockSpec((1, d, d), lambda l, j: (l, 0, j)),
                  pl.BlockSpec((1, 1, d), lambda l, j: (l, 0, j))],
        out_specs=pl.BlockSpec((1, SUBLANES, d), lambda l, j: (l, 0, j)),
        compiler_params=_cparams(("arbitrary", "arbitrary")),
        name="modulation",
    )(cond8, mod_w, mod_b.reshape(depth, 1, nd)).reshape(depth, SUBLANES, N_MOD, d)


class _Tiles:
    def __init__(self, batch, seq, cond_base, per_batch_cond, row_tile=ROW_TILE):
        self.batch, self.seq = batch, seq
        self.tm = min(row_tile, batch * seq)
        if seq >= self.tm:
            self.nb, self.tl = 1, self.tm
        else:
            self.nb, self.tl = self.tm // seq, seq
        assert seq % self.tl == 0 and batch % self.nb == 0
        self.tiles_per_seq = seq // self.tl
        self.n_tiles = batch * seq // self.tm
        self.cond_base = cond_base
        self.tiles_per_cond = self.tiles_per_seq if per_batch_cond else self.n_tiles

    def cond_row(self, i):
        return self.cond_base + i // self.tiles_per_cond

    def mod_spec(self, d):
        return pl.BlockSpec((1, N_MOD, d), lambda i: (self.cond_row(i), 0, 0))

    def row_spec(self, width, col=0):
        return pl.BlockSpec((self.tm, width), lambda i: (i, col))


def _ffn_kernel(*refs, mix, first, final, ff_chunks):
    it = iter(refs)
    x_ref, m_ref, g_ref, w13_ref, w2_ref = (next(it) for _ in range(5))
    if mix == "even":
        ya_ref, gb_ref, hf_ref, hb_ref, wo_ref = (next(it) for _ in range(5))
    elif mix == "odd":
        yc_ref, yf_ref, yb_ref, bon_ref, gg_ref, avg_ref, ln_ref, wo_ref = (next(it) for _ in range(8))
    fin_ref = next(it) if final else None
    o_ref = next(it)
    m = m_ref[0]
    r0 = 0 if first else 6
    gi = 0 if first else 2
    dff = w2_ref.shape[1]
    fc = dff // ff_chunks
    x = x_ref[...]
    if mix == "even":
        ya = ya_ref[...]
        yb = _gelu_tanh(gb_ref[...]) * (hf_ref[...] + hb_ref[...])
    elif mix == "odd":
        ya = yc_ref[...]
        ys = jnp.concatenate([yf_ref[q] + yb_ref[q] for q in range(yf_ref.shape[0])], axis=1)
        avg = avg_ref[...]
        oc = ys - _bdot(ys, avg)
        yd = oc * lax.rsqrt(_bdot(oc * oc, avg) + RWKV_LN_EPS)
        yd = yd * ln_ref[0:1] + ln_ref[1:2]
        yb = (yd + bon_ref[...]) * gg_ref[...]
    if mix is not None:
        wa = ya.shape[1]
        y = _dot(ya.astype(BF16), wo_ref[0:wa, :]) + _dot(yb.astype(BF16), wo_ref[wa:, :])
        x = x + m[5:6] * y
    h = _rms_mod(x, g_ref[gi:gi + 1], m[r0 + 1:r0 + 2], m[r0:r0 + 1]).astype(BF16)
    acc = jnp.zeros(x.shape, F32)
    for j in range(ff_chunks):
        gate = _dot(h, w13_ref[0, :, j * fc:(j + 1) * fc])
        up = _dot(h, w13_ref[0, :, dff + j * fc:dff + (j + 1) * fc])
        acc = acc + _dot((_silu(gate) * up).astype(BF16), w2_ref[0, j * fc:(j + 1) * fc, :])
    x = x + 0.5 * m[r0 + 2:r0 + 3] * acc
    if final:
        x = _rms(x) * fin_ref[...]
    o_ref[...] = x


def _layer_spec(shape, layer):
    nd = len(shape)
    return pl.BlockSpec((1,) + tuple(shape[1:]), lambda *_: (layer,) + (0,) * (nd - 1),
                        pipeline_mode=pl.Buffered(1))


def _ffn(tiles, x, mods, norm_g, w13, w2, layer, *, first, mix=None, mix_args=(), wo=None, final_g=None):
    n, d = x.shape
    dff = w2.shape[1]
    args = [x, mods, norm_g, w13, w2]
    specs = [tiles.row_spec(d), tiles.mod_spec(d), _const_spec(norm_g.shape),
             _layer_spec(w13.shape, layer), _layer_spec(w2.shape, layer)]
    if mix == "even":
        ya, gx, hf, hb = mix_args
        args += [ya, gx, hf, hb, wo]
        specs += [tiles.row_spec(W_A), tiles.row_spec(W_B, 0), tiles.row_spec(W_B), tiles.row_spec(W_B),
                  _const_spec(wo.shape)]
    elif mix == "odd":
        yc_tm, yf, yb, bon, gg, avg, ln = mix_args
        args += [yc_tm, yf, yb, bon, gg, avg, ln, wo]
        pspec = pl.BlockSpec((yf.shape[0], tiles.tm, yf.shape[2]), lambda i: (0, i, 0))
        specs += [tiles.row_spec(W_C), pspec, pspec] + [tiles.row_spec(W_D)] * 2 + [
            _const_spec(avg.shape), _const_spec(ln.shape), _const_spec(wo.shape)]
    if final_g is not None:
        args.append(final_g)
        specs.append(_const_spec(final_g.shape))
    kern = functools.partial(_ffn_kernel, mix=mix, first=first, final=final_g is not None,
                             ff_chunks=dff // (2 * LANES) if dff % (2 * LANES) == 0 else 1)
    return pl.pallas_call(
        kern, out_shape=jax.ShapeDtypeStruct((n, d), F32), grid=(tiles.n_tiles,),
        in_specs=specs, out_specs=tiles.row_spec(d),
        compiler_params=_cparams(("parallel",)), name="ffn",
    )(*args)


def _proj_even_kernel(*refs, rope):
    it = iter(refs)
    x_ref, m_ref, g_ref, w_ref = (next(it) for _ in range(4))
    cos_ref, sin_ref = (next(it), next(it)) if rope else (None, None)
    qkvg_ref, gx_ref = next(it), next(it)
    m = m_ref[0]
    h = _rms_mod(x_ref[...], g_ref[1:2], m[4:5], m[3:4]).astype(BF16)
    p = _dot(h, w_ref[...])
    nq = qkvg_ref.shape[1]
    if rope:
        qk = p[:, :2 * W_A]
        reps = 2 * W_A // DK_A
        cs = jnp.concatenate([cos_ref[...]] * reps, axis=1)
        sn = jnp.concatenate([sin_ref[...]] * reps, axis=1)
        lane = lax.broadcasted_iota(jnp.int32, qk.shape, 1)
        first_half = (lane & (DK_A // 2 - 1)) < DK_A // 4
        quarter = DK_A // 4
        partner = jnp.where(first_half, pltpu.roll(qk, 2 * W_A - quarter, axis=1), pltpu.roll(qk, quarter, axis=1))
        qkvg_ref[:, :2 * W_A] = qk * cs + partner * sn
        qkvg_ref[:, 2 * W_A:] = p[:, 2 * W_A:nq]
    else:
        qkvg_ref[...] = p[:, :nq]
    gx_ref[...] = p[:, nq:]


def _proj_odd_kernel(x_ref, m_ref, g_ref, w_ref, u_ref, rkvx_ref):
    nb, tl, _ = x_ref.shape
    hs = []
    for b in range(nb):
        m = m_ref[b if m_ref.shape[0] > 1 else 0]
        hs.append(_rms_mod(x_ref[b], g_ref[1:2], m[4:5], m[3:4]).astype(BF16))
    p = _dot(jnp.concatenate(hs, axis=0), w_ref[...])
    for b in range(nb):
        pb = p[b * tl:(b + 1) * tl]
        for j in range(u_ref.shape[0]):
            u_ref[j, pl.ds(b, tl, stride=nb), :] = pb[:, j * LANES:(j + 1) * LANES]
        rkvx_ref[b] = pb[:, W_C:]


def _proj_even(tiles, x, mods, norm_g, w_in, rope_tabs=None):
    n, d = x.shape
    args = [x, mods, norm_g, w_in]
    specs = [tiles.row_spec(d), tiles.mod_spec(d), _const_spec(norm_g.shape), _const_spec(w_in.shape)]
    if rope_tabs is not None:
        assert tiles.nb == 1
        args += list(rope_tabs)
        specs += [pl.BlockSpec((tiles.tm, DK_A), lambda i: (i % tiles.tiles_per_seq, 0))] * 2
    return pl.pallas_call(
        functools.partial(_proj_even_kernel, rope=rope_tabs is not None),
        out_shape=(jax.ShapeDtypeStruct((n, 4 * W_A), F32), jax.ShapeDtypeStruct((n, 2 * W_B), F32)),
        grid=(tiles.n_tiles,),
        in_specs=specs,
        out_specs=(tiles.row_spec(4 * W_A), tiles.row_spec(2 * W_B)),
        compiler_params=_cparams(("parallel",)), name="proj_even",
    )(*args)


def _time_tile(batch, seq):
    tl = max(min(2 * ROW_TILE, batch * seq) // batch, SUBLANES)
    assert seq % tl == 0 and tl % SUBLANES == 0
    return tl


def _proj_odd(tiles, x, mods, norm_g, w_in):
    n, d = x.shape
    batch, seq = tiles.batch, tiles.seq
    tl = _time_tile(batch, seq)
    if tiles.tiles_per_cond == tiles.n_tiles:
        mspec = pl.BlockSpec((1, N_MOD, d), lambda i: (tiles.cond_base, 0, 0))
    else:
        assert tiles.cond_base % batch == 0
        mspec = pl.BlockSpec((batch, N_MOD, d), lambda i: (tiles.cond_base // batch, 0, 0))
    u_tm, rkvx = pl.pallas_call(
        _proj_odd_kernel,
        out_shape=(jax.ShapeDtypeStruct((W_C // LANES, seq * batch, LANES), F32),
                   jax.ShapeDtypeStruct((batch, seq, 4 * W_D), F32)),
        grid=(seq // tl,),
        in_specs=[pl.BlockSpec((batch, tl, d), lambda i: (0, i, 0)), mspec,
                  _const_spec(norm_g.shape), _const_spec(w_in.shape)],
        out_specs=(pl.BlockSpec((W_C // LANES, tl * batch, LANES), lambda i: (0, i, 0)),
                   pl.BlockSpec((batch, tl, 4 * W_D), lambda i: (0, i, 0))),
        compiler_params=_cparams(("parallel",)), name="proj_odd",
    )(x.reshape(batch, seq, d), mods, norm_g, w_in)
    return u_tm, rkvx.reshape(n, 4 * W_D)


def _ret_kernel(*refs, seq, chunk, hp, has_init):
    it = iter(refs)
    q_ref, k_ref, v_ref, g_ref, dec_ref = (next(it) for _ in range(5))
    s0_ref = next(it) if has_init else None
    y_ref, sf_ref, o_scr = next(it), next(it), next(it)
    n = seq // chunk
    dk = DK_A
    ri = lax.broadcasted_iota(jnp.int32, (chunk, dk), 0).astype(F32)
    rel = (lax.broadcasted_iota(jnp.int32, (chunk, chunk), 0)
           - lax.broadcasted_iota(jnp.int32, (chunk, chunk), 1)).astype(F32)
    reps = chunk // dk
    tabs = []
    for h in range(hp):
        lgf = _log_sigmoid(dec_ref[0, h])
        lgb = _log_sigmoid(dec_ref[1, h])
        lgf_c = jnp.concatenate([lgf] * reps, axis=1) if reps > 1 else lgf[:, :chunk]
        lgb_c = jnp.concatenate([lgb] * reps, axis=1) if reps > 1 else lgb[:, :chunk]
        tabs.append(dict(
            qf=jnp.exp(lgf * (ri + 1.0)), qb=jnp.exp(lgb * (chunk - ri)),
            kf=jnp.exp(lgf * (chunk - 1.0 - ri)), kb=jnp.exp(lgb * ri),
            dmask=(jnp.where(rel >= 0, jnp.exp(lgf_c * jnp.maximum(rel, 0.0)), 0.0)
                   + jnp.where(rel <= 0, jnp.exp(lgb_c * jnp.maximum(-rel, 0.0)), 0.0)),
            gcf=jnp.exp(lgf * float(chunk)), gcb=jnp.exp(lgb * float(chunk))))

    def load_qkv(c, h):
        rows = pl.ds(pl.multiple_of(c * chunk, chunk), chunk)
        cols = slice(h * dk, (h + 1) * dk)
        return rows, cols, q_ref[rows, cols] * (dk ** -0.5), k_ref[rows, cols], v_ref[rows, cols]

    def fwd(c, states):
        out = []
        for h, (s, t) in enumerate(zip(states, tabs)):
            rows, cols, q, k, v = load_qkv(c, h)
            scores = _dot_nt(q.astype(BF16), k.astype(BF16)) * t['dmask']
            o_scr[rows, cols] = _bdot(scores, v) + _bdot(q * t['qf'], s)
            out.append(s * t['gcf'] + _dot_tn((k * t['kf']).astype(BF16), v.astype(BF16)))
        return tuple(out)

    def bwd(j, states):
        out = []
        for h, (s, t) in enumerate(zip(states, tabs)):
            rows, cols, q, k, v = load_qkv(n - 1 - j, h)
            o = o_scr[rows, cols] + _bdot(q * t['qb'], s)
            oc = o - jnp.mean(o, axis=-1, keepdims=True)
            on = oc * lax.rsqrt(jnp.mean(oc * oc, axis=-1, keepdims=True) + HEAD_NORM_EPS)
            y_ref[rows, cols] = on * _silu(g_ref[rows, cols])
            out.append(s * t['gcb'] + _dot_tn((k * t['kb']).astype(BF16), v.astype(BF16)))
        return tuple(out)

    zero = jnp.zeros((dk, dk), F32)
    unroll = 2 if n % 2 == 0 else 1
    sf = lax.fori_loop(0, n, fwd, tuple(s0_ref[0, 0, h] if has_init else zero for h in range(hp)), unroll=unroll)
    sb = lax.fori_loop(0, n, bwd, tuple(s0_ref[0, 1, h] if has_init else zero for h in range(hp)), unroll=unroll)
    for h in range(hp):
        sf_ref[0, 0, h] = sf[h]
        sf_ref[0, 1, h] = sb[h]


def _retention(qkvg, dec, batch, seq, s0=None):
    dk = DK_A
    chunk = min(seq, 256)
    hp = H_A if seq <= 512 else 2
    ng = H_A // hp
    args = [qkvg] * 4 + [dec]
    specs = [pl.BlockSpec((seq, hp * dk), lambda b, hh, j=j: (b, j * ng + hh)) for j in range(4)]
    specs.append(pl.BlockSpec((2, hp, 1, dk), lambda b, hh: (0, hh, 0, 0)))
    sspec = pl.BlockSpec((1, 2, hp, dk, dk), lambda b, hh: (b, 0, hh, 0, 0))
    if s0 is not None:
        args.append(s0)
        specs.append(sspec)
    kern = functools.partial(_ret_kernel, seq=seq, chunk=chunk, hp=hp, has_init=s0 is not None)
    return pl.pallas_call(
        kern,
        out_shape=(jax.ShapeDtypeStruct((batch * seq, W_A), F32),
                   jax.ShapeDtypeStruct((batch, 2, H_A, dk, dk), F32)),
        grid=(batch, ng),
        in_specs=specs,
        out_specs=(pl.BlockSpec((seq, hp * dk), lambda b, hh: (b, hh)), sspec),
        scratch_shapes=[pltpu.VMEM((seq, hp * dk), F32)],
        compiler_params=_cparams(("parallel", "arbitrary")), name="retention",
    )(*args)


def _halo_specs(tiles, width, col, n_rows):
    tm = tiles.tm
    nblk = n_rows // SUBLANES
    per = tm // SUBLANES
    prev = pl.BlockSpec((SUBLANES, width), lambda i: (jnp.maximum(i * per - 1, 0), col))
    nxt = pl.BlockSpec((SUBLANES, width), lambda i: (jnp.minimum((i + 1) * per, nblk - 1), col))
    return prev, nxt


def _seq_shift(ext, s, pos, seq, tm):
    sh = pltpu.roll(ext, s % ext.shape[0], axis=0)[SUBLANES:SUBLANES + tm]
    ok = (pos >= s) if s > 0 else (pos < seq + s)
    return jnp.where(ok, sh, 0.0)


def _seq_pos(shape, tm, seq):
    return (pl.program_id(0) * tm + lax.broadcasted_iota(jnp.int32, shape, 0)) & (seq - 1)


def _lru_prep_kernel(x_ref, xp_ref, xn_ref, cw_ref, wg_ref, bg_ref, lam_ref,
                     af_ref, bf_ref, ab_ref, bb_ref, *, seq):
    x = x_ref[...]
    tm = x.shape[0]
    ext = jnp.concatenate([xp_ref[...], x, xn_ref[...]], axis=0)
    pos = _seq_pos(x.shape, tm, seq)
    xc = (cw_ref[4:5] + cw_ref[0:1] * _seq_shift(ext, 2, pos, seq, tm)
          + cw_ref[1:2] * _seq_shift(ext, 1, pos, seq, tm) + cw_ref[2:3] * x
          + cw_ref[3:4] * _seq_shift(ext, -1, pos, seq, tm))
    gates = _dot(xc.astype(BF16), wg_ref[...]) + bg_ref[...]
    w = x.shape[1]
    for d, (a_ref, b_ref) in enumerate(((af_ref, bf_ref), (ab_ref, bb_ref))):
        r = _sigmoid(gates[:, (2 * d) * w:(2 * d + 1) * w])
        i = _sigmoid(gates[:, (2 * d + 1) * w:(2 * d + 2) * w])
        log_a = LRU_C * r * _log_sigmoid(lam_ref[d:d + 1])
        a = jnp.exp(log_a)
        a_ref[...] = a
        b_ref[...] = jnp.sqrt(1.0 - a * a) * (i * xc)


def _lru_prep(tiles, gx, conv_wb, wg, bg, lam):
    n = gx.shape[0]
    prev, nxt = _halo_specs(tiles, W_B, 1, n)
    out = jax.ShapeDtypeStruct((n, W_B), F32)
    return pl.pallas_call(
        functools.partial(_lru_prep_kernel, seq=tiles.seq),
        out_shape=(out,) * 4, grid=(tiles.n_tiles,),
        in_specs=[tiles.row_spec(W_B, 1), prev, nxt, _const_spec(conv_wb.shape), _const_spec(wg.shape),
                  _const_spec(bg.shape), _const_spec(lam.shape)],
        out_specs=(tiles.row_spec(W_B),) * 4,
        compiler_params=_cparams(("parallel",)), name="lru_prep",
    )(gx, gx, gx, conv_wb, wg, bg, lam)


def _lru_scan_kernel(*refs, nbt, tc, has_init):
    it = iter(refs)
    af_ref, bf_ref, ab_ref, bb_ref = (next(it) for _ in range(4))
    h0_ref = next(it) if has_init else None
    hf_ref, hb_ref, fin_ref, st_ref = (next(it) for _ in range(4))
    c = pl.program_id(1)

    @pl.when(c == 0)
    def _():
        st_ref[...] = h0_ref[...] if has_init else jnp.zeros(st_ref.shape, F32)

    def body(t, carry):
        tb = tc - 1 - t
        new = []
        for j in range(nbt):
            hf, hb = carry[2 * j], carry[2 * j + 1]
            hf = af_ref[j, pl.ds(t, 1), :] * hf + bf_ref[j, pl.ds(t, 1), :]
            hb = ab_ref[j, pl.ds(tb, 1), :] * hb + bb_ref[j, pl.ds(tb, 1), :]
            hf_ref[j, pl.ds(t, 1), :] = hf
            hb_ref[j, pl.ds(tb, 1), :] = hb
            new += [hf, hb]
        return tuple(new)

    init = tuple(st_ref[j, d:d + 1, :] for j in range(nbt) for d in range(2))
    out = lax.fori_loop(0, tc, body, init)
    for j in range(nbt):
        for d in range(2):
            st_ref[j, d:d + 1, :] = out[2 * j + d]
    fin_ref[...] = st_ref[...]


def _lru_scan(af, bf, ab, bb, batch, seq, h0=None):
    nbt = min(batch, 4)
    tc = min(seq, 256)
    n = seq // tc
    shp = (batch, seq, W_B)
    fspec = pl.BlockSpec((nbt, tc, W_B), lambda g, c: (g, c, 0))
    bspec = pl.BlockSpec((nbt, tc, W_B), lambda g, c: (g, n - 1 - c, 0))
    sspec = pl.BlockSpec((nbt, 2, W_B), lambda g, c: (g, 0, 0))
    args = [af.reshape(shp), bf.reshape(shp), ab.reshape(shp), bb.reshape(shp)]
    specs = [fspec, fspec, bspec, bspec]
    if h0 is not None:
        args.append(h0)
        specs.append(sspec)
    hf, hb, fin = pl.pallas_call(
        functools.partial(_lru_scan_kernel, nbt=nbt, tc=tc, has_init=h0 is not None),
        out_shape=(jax.ShapeDtypeStruct(shp, F32), jax.ShapeDtypeStruct(shp, F32),
                   jax.ShapeDtypeStruct((batch, 2, W_B), F32)),
        grid=(batch // nbt, n), in_specs=specs, out_specs=(fspec, bspec, sspec),
        scratch_shapes=[pltpu.VMEM((nbt, 2, W_B), F32)],
        compiler_params=_cparams(("parallel", "arbitrary")), name="lru_scan",
    )(*args)
    return hf.reshape(batch * seq, W_B), hb.reshape(batch * seq, W_B), fin


S5_BLOCKS = 2


def _s5_param_kernel(prm_ref, bre_ref, bim_ref, bbar_ref, abar_ref):
    a_re, a_im, ldt = prm_ref[0, 0:1], prm_ref[0, 1:2], prm_ref[0, 2:3]
    dt = jnp.exp(ldt)
    mag = jnp.exp(a_re * dt)
    abr = mag * jnp.cos(a_im * dt)
    abi = mag * jnp.sin(a_im * dt)
    den = a_re * a_re + a_im * a_im
    fr = ((abr - 1.0) * a_re + abi * a_im) / den
    fi = (abi * a_re - (abr - 1.0) * a_im) / den
    bre, bim = bre_ref[0, 0], bim_ref[0, 0]
    ns = bre.shape[1]
    bbar_ref[0, 0, :, 0:ns] = (fr * bre - fi * bim).astype(BF16)
    bbar_ref[0, 0, :, ns:] = (fr * bim + fi * bre).astype(BF16)
    abar_ref[0, 0:1] = abr
    abar_ref[0, 1:2] = abi


def _s5_params(prm, bre, bim):
    gn = prm.shape[2]
    _, nblk, cb, ns = bre.shape
    return pl.pallas_call(
        _s5_param_kernel,
        out_shape=(jax.ShapeDtypeStruct((2, nblk, cb, 2 * ns), BF16), jax.ShapeDtypeStruct((2, 2, gn), F32)),
        grid=(2, nblk),
        in_specs=[pl.BlockSpec((1, 3, ns), lambda d, j: (d, 0, j)),
                  pl.BlockSpec((1, 1, cb, ns), lambda d, j: (d, j, 0, 0)),
                  pl.BlockSpec((1, 1, cb, ns), lambda d, j: (d, j, 0, 0))],
        out_specs=(pl.BlockSpec((1, 1, cb, 2 * ns), lambda d, j: (d, j, 0, 0)),
                   pl.BlockSpec((1, 2, ns), lambda d, j: (d, 0, j))),
        compiler_params=_cparams(("arbitrary", "arbitrary")), name="s5_params",
    )(prm, bre, bim)


def _s5_scan_kernel(*refs, bp, tc, cw, has_init):
    it = iter(refs)
    uf_ref, ub_ref, bbar_ref, cc_ref, abar_ref = (next(it) for _ in range(5))
    s0_ref = next(it) if has_init else None
    yf_ref, yb_ref, fin_ref, buf_f, buf_b, st_ref = (next(it) for _ in range(6))
    c = pl.program_id(0)
    gn = abar_ref.shape[2]
    nblk, cb = bbar_ref.shape[1], bbar_ref.shape[2]
    ns = gn // nblk
    tr = st_ref.shape[1]
    sub = tr // bp
    n_it = tc // sub

    @pl.when(c == 0)
    def _():
        st_ref[...] = s0_ref[...] if has_init else jnp.zeros(st_ref.shape, F32)

    for d, (u_ref, buf) in enumerate(((uf_ref, buf_f), (ub_ref, buf_b))):
        u = jnp.concatenate([u_ref[j] for j in range(u_ref.shape[0])], axis=1).astype(BF16)
        for j in range(nblk):
            bu = _dot(u[:, j * cb:(j + 1) * cb], bbar_ref[d, j])
            buf[:, j * ns:(j + 1) * ns] = bu[:, :ns]
            buf[:, gn + j * ns:gn + (j + 1) * ns] = bu[:, ns:]

    first_f = lax.broadcasted_iota(jnp.int32, (tr, cw), 0) < bp

    def cmul_add(ar, ai, hr, hi, xr, xi):
        return ar * hr - ai * hi + xr, ar * hi + ai * hr + xi

    for kb in range(gn // cw):
        re = slice(kb * cw, (kb + 1) * cw)
        im = slice(gn + kb * cw, gn + (kb + 1) * cw)
        coef = []
        for d in range(2):
            coef.append((jnp.broadcast_to(abar_ref[d, 0:1, re], (tr, cw)),
                         jnp.broadcast_to(abar_ref[d, 1:2, re], (tr, cw))))

        def body(i, carry, re=re, im=im, coef=coef):
            out = []
            for d, buf in enumerate((buf_f, buf_b)):
                hr, hi = carry[2 * d], carry[2 * d + 1]
                ar, ai = coef[d]
                tile = i if d == 0 else n_it - 1 - i
                rows = pl.ds(pl.multiple_of(tile * tr, tr), tr)
                xr, xi = buf[rows, re], buf[rows, im]
                r1, i1 = cmul_add(ar, ai, hr, hi, xr, xi)
                if sub == 1:
                    outr, outi, nr, ni = r1, i1, r1, i1
                else:
                    first = first_f if d == 0 else jnp.logical_not(first_f)
                    r2, i2 = cmul_add(ar, ai, pltpu.roll(r1, bp, axis=0), pltpu.roll(i1, bp, axis=0), xr, xi)
                    outr, outi = jnp.where(first, r1, r2), jnp.where(first, i1, i2)
                    nr = jnp.where(first, pltpu.roll(r2, bp, axis=0), r2)
                    ni = jnp.where(first, pltpu.roll(i2, bp, axis=0), i2)
                buf[rows, re] = outr
                buf[rows, im] = outi
                out += [nr, ni]
            return tuple(out)

        init = (st_ref[0, :, re], st_ref[0, :, im], st_ref[1, :, re], st_ref[1, :, im])
        fr_, fi_, br_, bi_ = lax.fori_loop(0, n_it, body, init)
        st_ref[0, :, re] = fr_
        st_ref[0, :, im] = fi_
        st_ref[1, :, re] = br_
        st_ref[1, :, im] = bi_

    for d, (y_ref, buf) in enumerate(((yf_ref, buf_f), (yb_ref, buf_b))):
        for j in range(nblk):
            hre = buf[:, j * ns:(j + 1) * ns].astype(BF16)
            him = buf[:, gn + j * ns:gn + (j + 1) * ns].astype(BF16)
            yj = _dot(hre, cc_ref[d, j, 0:ns, :]) + _dot(him, cc_ref[d, j, ns:, :])
            for k in range(cb // LANES):
                y_ref[j * (cb // LANES) + k] = yj[:, k * LANES:(k + 1) * LANES]
    fin_ref[...] = st_ref[...]


def _s5_scan(u_tm, bbar, cc, abar, bp, seq, s0=None):
    assert bp % SUBLANES == 0 or 2 * bp == SUBLANES
    tr = max(bp, SUBLANES)
    rows = min(512, seq * bp)
    tc = rows // bp
    n = seq // tc
    cw = 128 if bp > SUBLANES else 256
    ng = W_C // LANES
    fspec = pl.BlockSpec((ng, rows, LANES), lambda c: (0, c, 0))
    bspec = pl.BlockSpec((ng, rows, LANES), lambda c: (0, n - 1 - c, 0))
    args = [u_tm, u_tm, bbar, cc, abar]
    specs = [fspec, bspec, _const_spec(bbar.shape), _const_spec(cc.shape), _const_spec(abar.shape)]
    if s0 is not None:
        args.append(s0)
        specs.append(_const_spec(s0.shape))
    return pl.pallas_call(
        functools.partial(_s5_scan_kernel, bp=bp, tc=tc, cw=cw, has_init=s0 is not None),
        out_shape=(jax.ShapeDtypeStruct(u_tm.shape, F32), jax.ShapeDtypeStruct(u_tm.shape, F32),
                   jax.ShapeDtypeStruct((2, tr, S5_STATE), F32)),
        grid=(n,), in_specs=specs,
        out_specs=(fspec, bspec, _const_spec((2, tr, S5_STATE))),
        scratch_shapes=[pltpu.VMEM((rows, S5_STATE), F32), pltpu.VMEM((rows, S5_STATE), F32),
                        pltpu.VMEM((2, tr, S5_STATE), F32)],
        compiler_params=_cparams(("arbitrary",)), name="s5_scan",
    )(*args)


def _s5_post_kernel(yf_ref, yb_ref, u_ref, d_ref, w_ref, b_ref, o_ref):
    nb, tl, _ = o_ref.shape
    zs = []
    for b in range(nb):
        rows = pl.ds(b, tl, stride=nb)
        yf, yb, u = (jnp.concatenate([ref[j, rows, :] for j in range(ref.shape[0])], axis=1)
                     for ref in (yf_ref, yb_ref, u_ref))
        zs.append(_gelu_tanh(yf + yb + d_ref[...] * u))
    z = jnp.concatenate(zs, axis=0)
    y = z * _sigmoid(_dot(z.astype(BF16), w_ref[...]) + b_ref[...])
    for b in range(nb):
        o_ref[b] = y[b * tl:(b + 1) * tl]


def _s5_post(yf, yb, u_tm, s5_d, glu_w, glu_b, batch, seq):
    tl = _time_tile(batch, seq)
    spec = pl.BlockSpec((W_C // LANES, tl * batch, LANES), lambda i: (0, i, 0))
    return pl.pallas_call(
        _s5_post_kernel, out_shape=jax.ShapeDtypeStruct((batch, seq, W_C), F32), grid=(seq // tl,),
        in_specs=[spec, spec, spec, _const_spec(s5_d.shape), _const_spec(glu_w.shape), _const_spec(glu_b.shape)],
        out_specs=pl.BlockSpec((batch, tl, W_C), lambda i: (0, i, 0)),
        compiler_params=_cparams(("parallel",)), name="s5_post",
    )(yf, yb, u_tm, s5_d, glu_w, glu_b).reshape(batch * seq, W_C)


def _rwkv_prep_kernel(x_ref, xp_ref, xn_ref, vec_ref, w1_ref, w2_ref, a1_ref, a2_ref, g1_ref, g2_ref,
                      ones_ref, *out_refs, seq):
    (g_ref, bon_ref, v_ref, kk_ref, r_ref) = out_refs[:5]
    dir_refs = (out_refs[5:8], out_refs[8:11])

    def put_pairs(ref, val):
        for q in range(ref.shape[0]):
            ref[q] = val[:, q * RWKV_PAIR:(q + 1) * RWKV_PAIR]

    x = x_ref[...]
    tm = x.shape[0]
    ext = jnp.concatenate([xp_ref[...], x, xn_ref[...]], axis=0)
    pos = _seq_pos(x.shape, tm, seq)
    dx = 0.5 * (_seq_shift(ext, 1, pos, seq, tm) + _seq_shift(ext, -1, pos, seq, tm)) - x
    w = W_D
    r, k, v, xd = (x[:, j * w:(j + 1) * w] for j in range(4))
    dr, dk_, dv, dxd = (dx[:, j * w:(j + 1) * w] for j in range(4))
    vec = vec_ref[...]
    mu = vec[0:6]
    r = r + dr * mu[0:1]
    k = k + dk_ * mu[1:2]
    v = v + dv * mu[2:3]
    xw = xd + dxd * mu[3:4]
    xa = xd + dxd * mu[4:5]
    xg = xd + dxd * mu[5:6]
    rw_kk, rw_ka, rw_rk = vec[6:7], vec[7:8], vec[8:9]
    ones = ones_ref[...]
    g_ref[...] = _bdot(_sigmoid(_bdot(xg, g1_ref[...])), g2_ref[...])
    kkr = k * rw_kk
    kk = kkr / jnp.maximum(jnp.sqrt(_bdot(kkr * kkr, ones)), 1e-12)
    put_pairs(v_ref, v)
    put_pairs(kk_ref, kk)
    put_pairs(r_ref, r)
    bonus = jnp.zeros(v.shape, F32)
    for d in range(2):
        w0, a0 = vec[9 + d:10 + d], vec[11 + d:12 + d]
        wl = -_softplus(-(w0 + _bdot(jnp.tanh(_bdot(xw, w1_ref[d])), w2_ref[d]))) - 0.5
        a = _sigmoid(a0 + _bdot(_bdot(xa, a1_ref[d]), a2_ref[d]))
        kd = k * (1.0 + (a - 1.0) * rw_ka)
        lw_ref, ka_ref, kd_ref = dir_refs[d]
        put_pairs(lw_ref, -jnp.exp(wl))
        put_pairs(ka_ref, kk * a)
        put_pairs(kd_ref, kd)
        bonus = bonus + _bdot(r * kd * rw_rk, ones) * v
    bon_ref[...] = bonus


N_RWKV_SEQS = 9


def _rwkv_prep(tiles, rkvx, vec, w1, w2, a1, a2, g1, g2, ones):
    n = rkvx.shape[0]
    npair = W_D // RWKV_PAIR
    prev, nxt = _halo_specs(tiles, 4 * W_D, 0, n)
    flat = jax.ShapeDtypeStruct((n, W_D), F32)
    paired = jax.ShapeDtypeStruct((npair, n, RWKV_PAIR), F32)
    pspec = pl.BlockSpec((npair, tiles.tm, RWKV_PAIR), lambda i: (0, i, 0))
    consts = [vec, w1, w2, a1, a2, g1, g2, ones]
    return pl.pallas_call(
        functools.partial(_rwkv_prep_kernel, seq=tiles.seq),
        out_shape=(flat,) * 2 + (paired,) * N_RWKV_SEQS, grid=(tiles.n_tiles,),
        in_specs=[tiles.row_spec(4 * W_D), prev, nxt] + [_const_spec(c.shape) for c in consts],
        out_specs=(tiles.row_spec(W_D),) * 2 + (pspec,) * N_RWKV_SEQS,
        compiler_params=_cparams(("parallel",)), name="rwkv_prep",
    )(rkvx, rkvx, rkvx, *consts)


def _bmm(a, b, precision=None):
    return jnp.einsum('umk,ukn->umn', a, b, precision=precision, preferred_element_type=F32)


def _bmm_nt(a, b):
    return jnp.einsum('umk,unk->umn', a, b, preferred_element_type=F32)


def _bmm_tn(a, b):
    return jnp.stack([_dot_tn(a[u], b[u]) for u in range(a.shape[0])], axis=0)


def _bf(a):
    return a.astype(BF16)


def _rwkv_masks(t, nbt):
    n2 = 2 * t
    ri = lax.broadcasted_iota(jnp.int32, (n2, n2), 0)
    ci = lax.broadcasted_iota(jnp.int32, (n2, n2), 1)
    same_head = (ri >= t) == (ci >= t)
    tt, ss = ri & (t - 1), ci & (t - 1)
    ti = lax.broadcasted_iota(jnp.int32, (t, t), 0)
    si = lax.broadcasted_iota(jnp.int32, (t, t), 1)

    def per_unit(fwd, bwd, dtype):
        return jnp.concatenate([jnp.broadcast_to(m.astype(dtype)[None], (nbt,) + m.shape) for m in (fwd, bwd)],
                               axis=0)

    return dict(
        strict=per_unit(same_head & (ss < tt), same_head & (ss > tt), F32),
        incl=per_unit(same_head & (ss <= tt), same_head & (ss >= tt), F32),
        tri=per_unit(si <= ti, si >= ti, BF16),
        same16=(ri >> 4) == (ci >> 4),
        off16=((ri >> 5) == (ci >> 5)) & ((ri >> 4) != (ci >> 4)),
        off32=(ri >> 5) != (ci >> 5),
        eye=(ri == ci).astype(F32),
        head0=lax.broadcasted_iota(jnp.int32, (t, RWKV_PAIR), 1) < HD_D,
    )


def _split(a):
    hi = a.astype(BF16)
    return hi, (a - hi.astype(F32)).astype(BF16)


def _bmm3(a, b):
    (ah, al), (bh, bl) = a, b
    n = bh.shape[2]
    lhs = jnp.concatenate([ah, al], axis=2)
    rhs = jnp.concatenate([jnp.concatenate([bh, bl], axis=2),
                           jnp.concatenate([bh, jnp.zeros_like(bl)], axis=2)], axis=1)
    out = _bmm(lhs, rhs)
    return out[:, :, :n] + out[:, :, n:]


def _unit_tri_inverse(a, mk):
    n1f = -jnp.where(mk['same16'], a, 0.0)
    n1 = _split(n1f)
    n2 = _split(_bmm3(n1, n1))
    n4 = _split(_bmm3(n2, n2))
    n8 = _split(_bmm3(n4, n4))
    p = mk['eye'] + n1f
    p = p + _bmm3(_split(p), n2)
    p = p + _bmm3(_split(p), n4)
    p = p + _bmm3(_split(p), n8)
    for off in ('off16', 'off32'):
        ps = _split(p)
        p = p - _bmm3(_split(_bmm3(ps, _split(jnp.where(mk[off], a, 0.0)))), ps)
    return _split(p)


def _rwkv_unit(r, v, kap, lw, beta, kd, s, mk):
    u, t, _ = r.shape
    lw_hi = lw.astype(BF16)
    lw_rest = lw - lw_hi.astype(F32)
    lw_mid = lw_rest.astype(BF16)
    lw_lo = (lw_rest - lw_mid.astype(F32)).astype(BF16)
    c3 = _bmm(mk['tri'], jnp.concatenate([lw_hi, lw_mid, lw_lo], axis=2))
    w = lw.shape[2]
    c = c3[:, :, :w] + c3[:, :, w:2 * w] + c3[:, :, 2 * w:]
    c_end = jnp.concatenate([c[:u // 2, t - 1:t], c[u // 2:, 0:1]], axis=0)
    e_neg = jnp.exp(-c)
    e_end = jnp.exp(c_end - c)
    head0 = mk['head0']

    def stack(x):
        return jnp.concatenate([jnp.where(head0, x, 0.0), jnp.where(head0, 0.0, x)], axis=1)

    rs = stack(r * jnp.exp(c))
    ks = stack(kap * jnp.exp(c - lw))
    bs = stack(beta * e_neg)
    kds = stack(kd * e_neg)
    vs = stack(v)
    bps = stack(beta * e_end)
    kps = stack(kd * e_end)
    n2 = 2 * t
    g = _bmm_nt(_bf(jnp.concatenate([ks, rs], axis=1)), _bf(jnp.concatenate([bs, kds], axis=1)))
    a_ab = g[:, :n2, :n2] * mk['strict']
    a_ak = g[:, :n2, n2:] * mk['strict']
    a_rb = g[:, n2:, :n2] * mk['incl']
    a_rk = g[:, n2:, n2:] * mk['incl']
    m = _unit_tri_inverse(a_ab, mk)
    vsb = _bf(vs)
    sol = _bmm3(m, _split(jnp.concatenate([ks, _bmm(_bf(a_ak), vsb)], axis=2)))
    wy = jnp.concatenate([rs, _bmm(_bf(a_rk), vsb)], axis=2) - _bmm(_bf(a_rb), _bf(sol))
    wy = wy[:, :t] + wy[:, t:]
    pg = _bmm_tn(_bf(sol), _bf(bps))
    gam = _bmm_tn(vsb, _bf(kps)) - pg[:, RWKV_PAIR:]
    y = _bmm_nt(_bf(wy[:, :, :RWKV_PAIR]), _bf(s)) + wy[:, :, RWKV_PAIR:]
    s_new = s * jnp.exp(c_end) - _bmm(_bf(s), _bf(pg[:, :RWKV_PAIR])) + gam
    return y, s_new


def _rwkv_chunk_kernel(*refs, nbt, has_init):
    it = iter(refs)
    ins = [tuple(next(it) for _ in range(6)) for _ in range(2)]
    s0_ref = next(it) if has_init else None
    yf_ref, yb_ref, fin_ref, st_ref = (next(it) for _ in range(4))
    c = pl.program_id(2)

    @pl.when(c == 0)
    def _():
        for d in range(2):
            st_ref[d * nbt:(d + 1) * nbt] = (s0_ref[:, d, 0] if has_init
                                             else jnp.zeros((nbt,) + st_ref.shape[1:], F32))

    t = RWKV_CHUNK
    nsub = yf_ref.shape[2] // t
    mk = _rwkv_masks(t, nbt)

    def chunk(j, carry):
        rows_f = pl.ds(pl.multiple_of(j * t, t), t)
        rows_b = pl.ds(pl.multiple_of((nsub - 1 - j) * t, t), t)
        ops = [jnp.concatenate([f[0, :, rows_f, :], b[0, :, rows_b, :]], axis=0) for f, b in zip(*ins)]
        y, s_new = _rwkv_unit(*ops, st_ref[...], mk)
        yf_ref[0, :, rows_f, :] = y[:nbt]
        yb_ref[0, :, rows_b, :] = y[nbt:]
        st_ref[...] = s_new
        return carry

    lax.fori_loop(0, nsub, chunk, 0)
    for d in range(2):
        for h in range(2):
            fin_ref[:, d, h] = st_ref[d * nbt:(d + 1) * nbt, h * HD_D:(h + 1) * HD_D, h * HD_D:(h + 1) * HD_D]


def _rwkv_scan(shared, fdir, bdir, batch, seq, s0=None):
    nbt = min(batch, 4)
    tb = min(seq, 4 * RWKV_CHUNK)
    n = seq // tb
    npair = W_D // RWKV_PAIR
    shp = (npair, batch, seq, RWKV_PAIR)
    fspec = pl.BlockSpec((1, nbt, tb, RWKV_PAIR), lambda g, q, c: (q, g, c, 0))
    bspec = pl.BlockSpec((1, nbt, tb, RWKV_PAIR), lambda g, q, c: (q, g, n - 1 - c, 0))
    sspec = pl.BlockSpec((nbt, 2, 1, RWKV_PAIR, RWKV_PAIR), lambda g, q, c: (g, 0, q, 0, 0))
    args = [a.reshape(shp) for a in tuple(shared) + tuple(fdir) + tuple(shared) + tuple(bdir)]
    specs = [fspec] * 6 + [bspec] * 6
    if s0 is not None:
        args.append(s0)
        specs.append(sspec)
    yf, yb, fin = pl.pallas_call(
        functools.partial(_rwkv_chunk_kernel, nbt=nbt, has_init=s0 is not None),
        out_shape=(jax.ShapeDtypeStruct(shp, F32), jax.ShapeDtypeStruct(shp, F32),
                   jax.ShapeDtypeStruct((batch, 2, H_D, HD_D, HD_D), F32)),
        grid=(batch // nbt, npair, n), in_specs=specs,
        out_specs=(fspec, bspec, pl.BlockSpec((nbt, 2, 2, HD_D, HD_D), lambda g, q, c: (g, 0, q, 0, 0))),
        scratch_shapes=[pltpu.VMEM((2 * nbt, RWKV_PAIR, RWKV_PAIR), F32)],
        compiler_params=_cparams(("parallel", "parallel", "arbitrary")), name="rwkv_scan",
    )(*args)
    pshape = (npair, batch * seq, RWKV_PAIR)
    return yf.reshape(pshape), yb.reshape(pshape), fin


def _block_diag(w):
    n, a, b = w.shape
    return jnp.einsum('nab,nm->namb', w, jnp.eye(n, dtype=w.dtype)).reshape(n * a, n * b)


def _rope_tables(seq, dk):
    t = jnp.arange(seq)
    row = (t // GRID_W).astype(F32)
    col = (t % GRID_W).astype(F32)
    nf = dk // 4
    freqs = ROPE_BASE ** (-jnp.arange(nf, dtype=F32) / nf)
    ar, ac = row[:, None] * freqs, col[:, None] * freqs
    cos = jnp.concatenate([jnp.cos(ar), jnp.cos(ar), jnp.cos(ac), jnp.cos(ac)], axis=1)
    sin = jnp.concatenate([-jnp.sin(ar), jnp.sin(ar), -jnp.sin(ac), jnp.sin(ac)], axis=1)
    return cos, sin


def _trunk(x3, cond_base, per_batch_cond, is_latent, states, mods, p):
    batch, seq, d = x3.shape
    tiles = _Tiles(batch, seq, cond_base, per_batch_cond)
    x = x3.reshape(batch * seq, d)
    s_ret, s_lru, s_s5, s_rwkv = states

    m0, g0 = mods[0], p['norm_g'][0]
    x = _ffn(tiles, x, m0, g0, p['f1a'], p['f1b'], 0, first=True)
    rope = _rope_tables(seq, DK_A) if is_latent else None
    qkvg, gx = _proj_even(tiles, x, m0, g0, p['l0_w_in'], rope_tabs=rope)
    y_a, new_ret = _retention(qkvg, p['ret_dec'], batch, seq, s0=s_ret)
    af, bf, ab, bb = _lru_prep(tiles, gx, p['conv_wb'], p['lru_wg'], p['lru_bg'], p['lru_lam'])
    hf, hb, new_lru = _lru_scan(af, bf, ab, bb, batch, seq, h0=s_lru)
    x = _ffn(tiles, x, m0, g0, p['f2a'], p['f2b'], 0, first=False, mix="even",
             mix_args=(y_a, gx, hf, hb), wo=p['l0_w_out'])

    m1, g1 = mods[1], p['norm_g'][1]
    x = _ffn(tiles, x, m1, g1, p['f1a'], p['f1b'], 1, first=True)
    u_tm, rkvx = _proj_odd(tiles, x, m1, g1, p['l1_w_in'])
    bp = batch
    s5_0 = None
    if s_s5 is not None:
        s5_0 = jnp.transpose(s_s5.reshape(batch, 2, S5_STATE), (1, 0, 2))
        s5_0 = jnp.concatenate([s5_0] * (max(bp, SUBLANES) // bp), axis=1)
    yf5, yb5, fin5 = _s5_scan(u_tm, p['s5_bbar'], p['s5_cc'], p['s5_abar'], bp, seq, s0=s5_0)
    yc = _s5_post(yf5, yb5, u_tm, p['s5_d'], p['glu_w'], p['glu_b'], batch, seq)
    new_s5 = jnp.transpose(fin5[:, :batch], (1, 0, 2)).reshape(batch, 2, 2, G_C, N_C)
    outs = _rwkv_prep(tiles, rkvx, p['rw_vec'], p['rw_w1'], p['rw_w2'], p['rw_a1'], p['rw_a2'],
                      p['rw_g1'], p['rw_g2'], p['ones_hd'])
    gg, bonus, v, kk, r = outs[:5]
    npair = H_D // 2
    eye2 = jnp.eye(2, dtype=F32)
    rw0 = None
    if s_rwkv is not None:
        rw0 = jnp.einsum('bdqjvk,ji->bdqjvik', s_rwkv.reshape(batch, 2, npair, 2, HD_D, HD_D), eye2)
        rw0 = rw0.reshape(batch, 2, npair, RWKV_PAIR, RWKV_PAIR)
    yfr, ybr, new_rwkv = _rwkv_scan((r, v, kk), outs[5:8], outs[8:11], batch, seq, s0=rw0)
    x = _ffn(tiles, x, m1, g1, p['f2a'], p['f2b'], 1, first=False, mix="odd",
             mix_args=(yc, yfr, ybr, bonus, gg, p['avg_hd'], p['ln_wb']), wo=p['l1_w_out'],
             final_g=p['final_norm'])
    return x.reshape(batch, seq, d), (new_ret, new_lru, new_s5, new_rwkv)


def kernel(x_prompt, x_sample, state_l0_ret, state_l0_lru, state_l1_s5, state_l1_rwkv, c, c_ctx, mod_w, mod_b, norm_g, ffn1_w13, ffn1_w2, ffn2_w13, ffn2_w2, final_norm, l0_w_in, l0_w_out, l0_ret_decay, l0_conv_w, l0_conv_b, l0_lru_lam, l0_lru_wa, l0_lru_ba, l0_lru_wx, l0_lru_bx, l1_w_in, l1_w_out, l1_s5_a_re, l1_s5_a_im, l1_s5_log_dt, l1_s5_b_re, l1_s5_b_im, l1_s5_c_re, l1_s5_c_im, l1_s5_d, l1_glu_w, l1_glu_b, l1_rw_mu, l1_rw_w0, l1_rw_w1, l1_rw_w2, l1_rw_a0, l1_rw_a1, l1_rw_a2, l1_rw_g1, l1_rw_g2, l1_rw_kk, l1_rw_ka, l1_rw_rk, l1_ln_w, l1_ln_b):
    d = x_prompt.shape[-1]
    nlat = c.shape[0]
    cond8 = jnp.concatenate([c_ctx[None, :], jnp.zeros((SUBLANES - 1 - nlat, d), F32), c], axis=0)
    mods = _modulation(cond8, mod_w, mod_b)

    gn = G_C * N_C
    gb = G_C // S5_BLOCKS
    eye_g = jnp.eye(gb, dtype=F32)

    def embed(w, spec):
        w = w.reshape((2, S5_BLOCKS, gb) + w.shape[2:])
        return jnp.einsum(spec, w, eye_g)

    bre = embed(l1_s5_b_re, 'djgns,gh->djgshn').reshape(2, S5_BLOCKS, gb * GS_C, gb * N_C)
    bim = embed(l1_s5_b_im, 'djgns,gh->djgshn').reshape(2, S5_BLOCKS, gb * GS_C, gb * N_C)
    cre = embed(l1_s5_c_re, 'djgsn,gh->djgnhs').reshape(2, S5_BLOCKS, gb * N_C, gb * GS_C)
    cim = embed(l1_s5_c_im, 'djgsn,gh->djgnhs').reshape(2, S5_BLOCKS, gb * N_C, gb * GS_C)
    prm = jnp.stack([l1_s5_a_re.reshape(2, gn), l1_s5_a_im.reshape(2, gn),
                     jnp.repeat(l1_s5_log_dt, N_C, axis=1)], axis=1)
    s5_bbar, s5_abar = _s5_params(prm, bre, bim)
    head_id = jnp.arange(W_D) // HD_D
    ones_hd = (head_id[:, None] == head_id[None, :]).astype(BF16)
    p = dict(
        norm_g=norm_g, final_norm=final_norm.reshape(1, d),
        f1a=ffn1_w13.astype(BF16), f1b=ffn1_w2.astype(BF16),
        f2a=ffn2_w13.astype(BF16), f2b=ffn2_w2.astype(BF16),
        l0_w_in=l0_w_in.astype(BF16), l0_w_out=l0_w_out.astype(BF16),
        l1_w_in=l1_w_in.astype(BF16), l1_w_out=l1_w_out.astype(BF16),
        ret_dec=jnp.broadcast_to(l0_ret_decay.astype(F32)[:, :, None, None], (2, H_A, 1, DK_A)),
        conv_wb=jnp.concatenate([l0_conv_w, l0_conv_b[None, :]], axis=0),
        lru_wg=jnp.concatenate([_block_diag(l0_lru_wa[0]), _block_diag(l0_lru_wx[0]),
                                _block_diag(l0_lru_wa[1]), _block_diag(l0_lru_wx[1])], axis=1).astype(BF16),
        lru_bg=jnp.concatenate([l0_lru_ba[0], l0_lru_bx[0], l0_lru_ba[1], l0_lru_bx[1]])[None, :],
        lru_lam=l0_lru_lam,
        s5_bbar=s5_bbar, s5_abar=s5_abar,
        s5_cc=jnp.concatenate([cre, -cim], axis=2).astype(BF16),
        s5_d=l1_s5_d.reshape(1, W_C), glu_w=l1_glu_w.astype(BF16), glu_b=l1_glu_b.reshape(1, W_C),
        rw_vec=jnp.concatenate([l1_rw_mu, l1_rw_kk[None], l1_rw_ka[None], l1_rw_rk[None],
                                l1_rw_w0, l1_rw_a0, jnp.zeros((3, W_D), F32)], axis=0),
        rw_w1=l1_rw_w1.astype(BF16), rw_w2=l1_rw_w2.astype(BF16),
        rw_a1=l1_rw_a1.astype(BF16), rw_a2=l1_rw_a2.astype(BF16),
        rw_g1=l1_rw_g1.astype(BF16), rw_g2=l1_rw_g2.astype(BF16),
        ones_hd=ones_hd, avg_hd=(ones_hd.astype(F32) / HD_D).astype(BF16),
        ln_wb=jnp.stack([l1_ln_w, l1_ln_b], axis=0),
    )
    y_prompt, ctx_states = _trunk(x_prompt, 0, False, False, (None, None, None, None), mods, p)
    y_sample, _ = _trunk(x_sample, SUBLANES - nlat, True, True,
                         (state_l0_ret, state_l0_lru, state_l1_s5, state_l1_rwkv), mods, p)
    return (y_prompt, y_sample) + ctx_states
```

```python
import functools
import math

import jax
import jax.numpy as jnp
from jax import lax
from jax.experimental import pallas as pl
from jax.experimental.pallas import tpu as pltpu

F32 = jnp.float32
BF16 = jnp.bfloat16

N_MOD = 9
NORM_EPS = 1e-6
HEAD_NORM_EPS = 1e-5
RWKV_LN_EPS = 64e-5
ROPE_BASE = 10000.0
GRID_W = 64
H_A = 4
DK_A = 128
W_A = 512
W_B = 512
NB_B = 8
LRU_C = 8.0
W_C = 512
GS_C = 16
G_C = 32
N_C = 64
W_D = 512
HD_D = 64
H_D = 8
S5_STATE = 2 * G_C * N_C
RWKV_CHUNK = 64
RWKV_PAIR = 2 * HD_D

SUBLANES = 8
LANES = 128
VMEM_LIMIT = 56 * 1024 * 1024
ROW_TILE = 512


def _cparams(sem):
    return pltpu.CompilerParams(dimension_semantics=sem, vmem_limit_bytes=VMEM_LIMIT)


def _dot(a, b):
    return jnp.dot(a, b, preferred_element_type=F32)


def _dot_nt(a, b):
    return lax.dot_general(a, b, (((1,), (1,)), ((), ())), preferred_element_type=F32)


def _dot_tn(a, b):
    return lax.dot_general(a, b, (((0,), (0,)), ((), ())), preferred_element_type=F32)


def _bdot(a, b):
    return _dot(a.astype(BF16), b.astype(BF16))


def _sigmoid(x):
    return 1.0 / (1.0 + jnp.exp(-x))


def _silu(x):
    return x * _sigmoid(x)


def _softplus(x):
    return jnp.maximum(x, 0.0) + jnp.log(1.0 + jnp.exp(-jnp.abs(x)))


def _log_sigmoid(x):
    return -_softplus(-x)


def _gelu_tanh(x):
    return 0.5 * x * (1.0 + jnp.tanh(math.sqrt(2.0 / math.pi) * (x + 0.044715 * (x * x * x))))


def _rms(x):
    return x * lax.rsqrt(jnp.mean(x * x, axis=-1, keepdims=True) + NORM_EPS)


def _rms_mod(x, g, scale, shift):
    return (_rms(x) * g) * (1.0 + scale) + shift


def _const_spec(shape):
    nd = len(shape)
    return pl.BlockSpec(shape, lambda *_: (0,) * nd, pipeline_mode=pl.Buffered(1))


def _mod_kernel(c_ref, w_ref, b_ref, o_ref):
    o_ref[0] = _bdot(_silu(c_ref[...]), w_ref[0]) + b_ref[0]


def _modulation(cond8, mod_w, mod_b):
    depth, d, nd = mod_w.shape
    return pl.pallas_call(
        _mod_kernel,
        out_shape=jax.ShapeDtypeStruct((depth, SUBLANES, nd), F32),
        grid=(depth, nd // d),
        in_specs=[pl.BlockSpec((SUBLANES, d), lambda l, j: (0, 0)),
                  pl.BlockSpec((1, d, d), lambda l, j: (l, 0, j)),
                  pl.BlockSpec((1, 1, d), lambda l, j: (l, 0, j))],
        out_specs=pl.BlockSpec((1, SUBLANES, d), lambda l, j: (l, 0, j)),
        compiler_params=_cparams(("arbitrary", "arbitrary")),
        name="modulation",
    )(cond8, mod_w, mod_b.reshape(depth, 1, nd)).reshape(depth, SUBLANES, N_MOD, d)


class _Tiles:
    def __init__(self, batch, seq, cond_base, per_batch_cond, row_tile=ROW_TILE):
        self.batch, self.seq = batch, seq
        self.tm = min(row_tile, batch * seq)
        if seq >= self.tm:
            self.nb, self.tl = 1, self.tm
        else:
            self.nb, self.tl = self.tm // seq, seq
        assert seq % self.tl == 0 and batch % self.nb == 0
        self.tiles_per_seq = seq // self.tl
        self.n_tiles = batch * seq // self.tm
        self.cond_base = cond_base
        self.tiles_per_cond = self.tiles_per_seq if per_batch_cond else self.n_tiles

    def cond_row(self, i):
        return self.cond_base + i // self.tiles_per_cond

    def mod_spec(self, d):
        return pl.BlockSpec((1, N_MOD, d), lambda i: (self.cond_row(i), 0, 0))

    def row_spec(self, width, col=0):
        return pl.BlockSpec((self.tm, width), lambda i: (i, col))


def _ffn_kernel(*refs, mix, first, final, ff_chunks):
    it = iter(refs)
    x_ref, m_ref, g_ref, w13_ref, w2_ref = (next(it) for _ in range(5))
    if mix == "even":
        ya_ref, gb_ref, hf_ref, hb_ref, wo_ref = (next(it) for _ in range(5))
    elif mix == "odd":
        yc_ref, yf_ref, yb_ref, bon_ref, gg_ref, avg_ref, ln_ref, wo_ref = (next(it) for _ in range(8))
    fin_ref = next(it) if final else None
    o_ref = next(it)
    m = m_ref[0]
    r0 = 0 if first else 6
    gi = 0 if first else 2
    dff = w2_ref.shape[1]
    fc = dff // ff_chunks
    x = x_ref[...]
    if mix == "even":
        ya = ya_ref[...]
        yb = _gelu_tanh(gb_ref[...]) * (hf_ref[...] + hb_ref[...])
    elif mix == "odd":
        ya = yc_ref[...]
        ys = jnp.concatenate([yf_ref[q] + yb_ref[q] for q in range(yf_ref.shape[0])], axis=1)
        avg = avg_ref[...]
        oc = ys - _bdot(ys, avg)
        yd = oc * lax.rsqrt(_bdot(oc * oc, avg) + RWKV_LN_EPS)
        yd = yd * ln_ref[0:1] + ln_ref[1:2]
        yb = (yd + bon_ref[...]) * gg_ref[...]
    if mix is not None:
        wa = ya.shape[1]
        y = _dot(ya.astype(BF16), wo_ref[0:wa, :]) + _dot(yb.astype(BF16), wo_ref[wa:, :])
        x = x + m[5:6] * y
    h = _rms_mod(x, g_ref[gi:gi + 1], m[r0 + 1:r0 + 2], m[r0:r0 + 1]).astype(BF16)
    acc = jnp.zeros(x.shape, F32)
    for j in range(ff_chunks):
        gate = _dot(h, w13_ref[0, :, j * fc:(j + 1) * fc])
        up = _dot(h, w13_ref[0, :, dff + j * fc:dff + (j + 1) * fc])
        acc = acc + _dot((_silu(gate) * up).astype(BF16), w2_ref[0, j * fc:(j + 1) * fc, :])
    x = x + 0.5 * m[r0 + 2:r0 + 3] * acc
    if final:
        x = _rms(x) * fin_ref[...]
    o_ref[...] = x


def _layer_spec(shape, layer):
    nd = len(shape)
    return pl.BlockSpec((1,) + tuple(shape[1:]), lambda *_: (layer,) + (0,) * (nd - 1),
                        pipeline_mode=pl.Buffered(1))


def _ffn(tiles, x, mods, norm_g, w13, w2, layer, *, first, mix=None, mix_args=(), wo=None, final_g=None):
    n, d = x.shape
    dff = w2.shape[1]
    args = [x, mods, norm_g, w13, w2]
    specs = [tiles.row_spec(d), tiles.mod_spec(d), _const_spec(norm_g.shape),
             _layer_spec(w13.shape, layer), _layer_spec(w2.shape, layer)]
    if mix == "even":
        ya, gx, hf, hb = mix_args
        args += [ya, gx, hf, hb, wo]
        specs += [tiles.row_spec(W_A), tiles.row_spec(W_B, 0), tiles.row_spec(W_B), tiles.row_spec(W_B),
                  _const_spec(wo.shape)]
    elif mix == "odd":
        yc_tm, yf, yb, bon, gg, avg, ln = mix_args
        args += [yc_tm, yf, yb, bon, gg, avg, ln, wo]
        pspec = pl.BlockSpec((yf.shape[0], tiles.tm, yf.shape[2]), lambda i: (0, i, 0))
        specs += [tiles.row_spec(W_C), pspec, pspec] + [tiles.row_spec(W_D)] * 2 + [
            _const_spec(avg.shape), _const_spec(ln.shape), _const_spec(wo.shape)]
    if final_g is not None:
        args.append(final_g)
        specs.append(_const_spec(final_g.shape))
    kern = functools.partial(_ffn_kernel, mix=mix, first=first, final=final_g is not None,
                             ff_chunks=dff // (2 * LANES) if dff % (2 * LANES) == 0 else 1)
    return pl.pallas_call(
        kern, out_shape=jax.ShapeDtypeStruct((n, d), F32), grid=(tiles.n_tiles,),
        in_specs=specs, out_specs=tiles.row_spec(d),
        compiler_params=_cparams(("parallel",)), name="ffn",
    )(*args)


def _proj_even_kernel(*refs, rope):
    it = iter(refs)
    x_ref, m_ref, g_ref, w_ref = (next(it) for _ in range(4))
    cos_ref, sin_ref = (next(it), next(it)) if rope else (None, None)
    qkvg_ref, gx_ref = next(it), next(it)
    m = m_ref[0]
    h = _rms_mod(x_ref[...], g_ref[1:2], m[4:5], m[3:4]).astype(BF16)
    p = _dot(h, w_ref[...])
    nq = qkvg_ref.shape[1]
    if rope:
        qk = p[:, :2 * W_A]
        reps = 2 * W_A // DK_A
        cs = jnp.concatenate([cos_ref[...]] * reps, axis=1)
        sn = jnp.concatenate([sin_ref[...]] * reps, axis=1)
        lane = lax.broadcasted_iota(jnp.int32, qk.shape, 1)
        first_half = (lane & (DK_A // 2 - 1)) < DK_A // 4
        quarter = DK_A // 4
        partner = jnp.where(first_half, pltpu.roll(qk, 2 * W_A - quarter, axis=1), pltpu.roll(qk, quarter, axis=1))
        qkvg_ref[:, :2 * W_A] = qk * cs + partner * sn
        qkvg_ref[:, 2 * W_A:] = p[:, 2 * W_A:nq]
    else:
        qkvg_ref[...] = p[:, :nq]
    gx_ref[...] = p[:, nq:]


def _proj_odd_kernel(x_ref, m_ref, g_ref, w_ref, u_ref, rkvx_ref):
    nb, tl, _ = x_ref.shape
    hs = []
    for b in range(nb):
        m = m_ref[b if m_ref.shape[0] > 1 else 0]
        hs.append(_rms_mod(x_ref[b], g_ref[1:2], m[4:5], m[3:4]).astype(BF16))
    p = _dot(jnp.concatenate(hs, axis=0), w_ref[...])
    for b in range(nb):
        pb = p[b * tl:(b + 1) * tl]
        for j in range(u_ref.shape[0]):
            u_ref[j, pl.ds(b, tl, stride=nb), :] = pb[:, j * LANES:(j + 1) * LANES]
        rkvx_ref[b] = pb[:, W_C:]


def _proj_even(tiles, x, mods, norm_g, w_in, rope_tabs=None):
    n, d = x.shape
    args = [x, mods, norm_g, w_in]
    specs = [tiles.row_spec(d), tiles.mod_spec(d), _const_spec(norm_g.shape), _const_spec(w_in.shape)]
    if rope_tabs is not None:
        assert tiles.nb == 1
        args += list(rope_tabs)
        specs += [pl.BlockSpec((tiles.tm, DK_A), lambda i: (i % tiles.tiles_per_seq, 0))] * 2
    return pl.pallas_call(
        functools.partial(_proj_even_kernel, rope=rope_tabs is not None),
        out_shape=(jax.ShapeDtypeStruct((n, 4 * W_A), F32), jax.ShapeDtypeStruct((n, 2 * W_B), F32)),
        grid=(tiles.n_tiles,),
        in_specs=specs,
        out_specs=(tiles.row_spec(4 * W_A), tiles.row_spec(2 * W_B)),
        compiler_params=_cparams(("parallel",)), name="proj_even",
    )(*args)


def _time_tile(batch, seq):
    tl = max(min(2 * ROW_TILE, batch * seq) // batch, SUBLANES)
    assert seq % tl == 0 and tl % SUBLANES == 0
    return tl


def _proj_odd(tiles, x, mods, norm_g, w_in):
    n, d = x.shape
    batch, seq = tiles.batch, tiles.seq
    tl = _time_tile(batch, seq)
    if tiles.tiles_per_cond == tiles.n_tiles:
        mspec = pl.BlockSpec((1, N_MOD, d), lambda i: (tiles.cond_base, 0, 0))
    else:
        assert tiles.cond_base % batch == 0
        mspec = pl.BlockSpec((batch, N_MOD, d), lambda i: (tiles.cond_base // batch, 0, 0))
    u_tm, rkvx = pl.pallas_call(
        _proj_odd_kernel,
        out_shape=(jax.ShapeDtypeStruct((W_C // LANES, seq * batch, LANES), F32),
                   jax.ShapeDtypeStruct((batch, seq, 4 * W_D), F32)),
        grid=(seq // tl,),
        in_specs=[pl.BlockSpec((batch, tl, d), lambda i: (0, i, 0)), mspec,
                  _const_spec(norm_g.shape), _const_spec(w_in.shape)],
        out_specs=(pl.BlockSpec((W_C // LANES, tl * batch, LANES), lambda i: (0, i, 0)),
                   pl.BlockSpec((batch, tl, 4 * W_D), lambda i: (0, i, 0))),
        compiler_params=_cparams(("parallel",)), name="proj_odd",
    )(x.reshape(batch, seq, d), mods, norm_g, w_in)
    return u_tm, rkvx.reshape(n, 4 * W_D)


def _ret_kernel(*refs, seq, chunk, hp, has_init):
    it = iter(refs)
    q_ref, k_ref, v_ref, g_ref, dec_ref = (next(it) for _ in range(5))
    s0_ref = next(it) if has_init else None
    y_ref, sf_ref, o_scr = next(it), next(it), next(it)
    n = seq // chunk
    dk = DK_A
    ri = lax.broadcasted_iota(jnp.int32, (chunk, dk), 0).astype(F32)
    rel = (lax.broadcasted_iota(jnp.int32, (chunk, chunk), 0)
           - lax.broadcasted_iota(jnp.int32, (chunk, chunk), 1)).astype(F32)
    reps = chunk // dk
    tabs = []
    for h in range(hp):
        lgf = _log_sigmoid(dec_ref[0, h])
        lgb = _log_sigmoid(dec_ref[1, h])
        lgf_c = jnp.concatenate([lgf] * reps, axis=1) if reps > 1 else lgf[:, :chunk]
        lgb_c = jnp.concatenate([lgb] * reps, axis=1) if reps > 1 else lgb[:, :chunk]
        tabs.append(dict(
            qf=jnp.exp(lgf * (ri + 1.0)), qb=jnp.exp(lgb * (chunk - ri)),
            kf=jnp.exp(lgf * (chunk - 1.0 - ri)), kb=jnp.exp(lgb * ri),
            dmask=(jnp.where(rel >= 0, jnp.exp(lgf_c * jnp.maximum(rel, 0.0)), 0.0)
                   + jnp.where(rel <= 0, jnp.exp(lgb_c * jnp.maximum(-rel, 0.0)), 0.0)),
            gcf=jnp.exp(lgf * float(chunk)), gcb=jnp.exp(lgb * float(chunk))))

    def load_qkv(c, h):
        rows = pl.ds(pl.multiple_of(c * chunk, chunk), chunk)
        cols = slice(h * dk, (h + 1) * dk)
        return rows, cols, q_ref[rows, cols] * (dk ** -0.5), k_ref[rows, cols], v_ref[rows, cols]

    def fwd(c, states):
        out = []
        for h, (s, t) in enumerate(zip(states, tabs)):
            rows, cols, q, k, v = load_qkv(c, h)
            scores = _dot_nt(q.astype(BF16), k.astype(BF16)) * t['dmask']
            o_scr[rows, cols] = _bdot(scores, v) + _bdot(q * t['qf'], s)
            out.append(s * t['gcf'] + _dot_tn((k * t['kf']).astype(BF16), v.astype(BF16)))
        return tuple(out)

    def bwd(j, states):
        out = []
        for h, (s, t) in enumerate(zip(states, tabs)):
            rows, cols, q, k, v = load_qkv(n - 1 - j, h)
            o = o_scr[rows, cols] + _bdot(q * t['qb'], s)
            oc = o - jnp.mean(o, axis=-1, keepdims=True)
            on = oc * lax.rsqrt(jnp.mean(oc * oc, axis=-1, keepdims=True) + HEAD_NORM_EPS)
            y_ref[rows, cols] = on * _silu(g_ref[rows, cols])
            out.append(s * t['gcb'] + _dot_tn((k * t['kb']).astype(BF16), v.astype(BF16)))
        return tuple(out)

    zero = jnp.zeros((dk, dk), F32)
    unroll = 2 if n % 2 == 0 else 1
    sf = lax.fori_loop(0, n, fwd, tuple(s0_ref[0, 0, h] if has_init else zero for h in range(hp)), unroll=unroll)
    sb = lax.fori_loop(0, n, bwd, tuple(s0_ref[0, 1, h] if has_init else zero for h in range(hp)), unroll=unroll)
    for h in range(hp):
        sf_ref[0, 0, h] = sf[h]
        sf_ref[0, 1, h] = sb[h]


def _retention(qkvg, dec, batch, seq, s0=None):
    dk = DK_A
    chunk = min(seq, 256)
    hp = H_A if seq <= 512 else 2
    ng = H_A // hp
    args = [qkvg] * 4 + [dec]
    specs = [pl.BlockSpec((seq, hp * dk), lambda b, hh, j=j: (b, j * ng + hh)) for j in range(4)]
    specs.append(pl.BlockSpec((2, hp, 1, dk), lambda b, hh: (0, hh, 0, 0)))
    sspec = pl.BlockSpec((1, 2, hp, dk, dk), lambda b, hh: (b, 0, hh, 0, 0))
    if s0 is not None:
        args.append(s0)
        specs.append(sspec)
    kern = functools.partial(_ret_kernel, seq=seq, chunk=chunk, hp=hp, has_init=s0 is not None)
    return pl.pallas_call(
        kern,
        out_shape=(jax.ShapeDtypeStruct((batch * seq, W_A), F32),
                   jax.ShapeDtypeStruct((batch, 2, H_A, dk, dk), F32)),
        grid=(batch, ng),
        in_specs=specs,
        out_specs=(pl.BlockSpec((seq, hp * dk), lambda b, hh: (b, hh)), sspec),
        scratch_shapes=[pltpu.VMEM((seq, hp * dk), F32)],
        compiler_params=_cparams(("parallel", "arbitrary")), name="retention",
    )(*args)


def _halo_specs(tiles, width, col, n_rows):
    tm = tiles.tm
    nblk = n_rows // SUBLANES
    per = tm // SUBLANES
    prev = pl.BlockSpec((SUBLANES, width), lambda i: (jnp.maximum(i * per - 1, 0), col))
    nxt = pl.BlockSpec((SUBLANES, width), lambda i: (jnp.minimum((i + 1) * per, nblk - 1), col))
    return prev, nxt


def _seq_shift(ext, s, pos, seq, tm):
    sh = pltpu.roll(ext, s % ext.shape[0], axis=0)[SUBLANES:SUBLANES + tm]
    ok = (pos >= s) if s > 0 else (pos < seq + s)
    return jnp.where(ok, sh, 0.0)


def _seq_pos(shape, tm, seq):
    return (pl.program_id(0) * tm + lax.broadcasted_iota(jnp.int32, shape, 0)) & (seq - 1)


def _lru_prep_kernel(x_ref, xp_ref, xn_ref, cw_ref, wg_ref, bg_ref, lam_ref,
                     af_ref, bf_ref, ab_ref, bb_ref, *, seq):
    x = x_ref[...]
    tm = x.shape[0]
    ext = jnp.concatenate([xp_ref[...], x, xn_ref[...]], axis=0)
    pos = _seq_pos(x.shape, tm, seq)
    xc = (cw_ref[4:5] + cw_ref[0:1] * _seq_shift(ext, 2, pos, seq, tm)
          + cw_ref[1:2] * _seq_shift(ext, 1, pos, seq, tm) + cw_ref[2:3] * x
          + cw_ref[3:4] * _seq_shift(ext, -1, pos, seq, tm))
    gates = _dot(xc.astype(BF16), wg_ref[...]) + bg_ref[...]
    w = x.shape[1]
    for d, (a_ref, b_ref) in enumerate(((af_ref, bf_ref), (ab_ref, bb_ref))):
        r = _sigmoid(gates[:, (2 * d) * w:(2 * d + 1) * w])
        i = _sigmoid(gates[:, (2 * d + 1) * w:(2 * d + 2) * w])
        log_a = LRU_C * r * _log_sigmoid(lam_ref[d:d + 1])
        a = jnp.exp(log_a)
        a_ref[...] = a
        b_ref[...] = jnp.sqrt(1.0 - a * a) * (i * xc)


def _lru_prep(tiles, gx, conv_wb, wg, bg, lam):
    n = gx.shape[0]
    prev, nxt = _halo_specs(tiles, W_B, 1, n)
    out = jax.ShapeDtypeStruct((n, W_B), F32)
    return pl.pallas_call(
        functools.partial(_lru_prep_kernel, seq=tiles.seq),
        out_shape=(out,) * 4, grid=(tiles.n_tiles,),
        in_specs=[tiles.row_spec(W_B, 1), prev, nxt, _const_spec(conv_wb.shape), _const_spec(wg.shape),
                  _const_spec(bg.shape), _const_spec(lam.shape)],
        out_specs=(tiles.row_spec(W_B),) * 4,
        compiler_params=_cparams(("parallel",)), name="lru_prep",
    )(gx, gx, gx, conv_wb, wg, bg, lam)


def _lru_scan_kernel(*refs, nbt, tc, has_init):
    it = iter(refs)
    af_ref, bf_ref, ab_ref, bb_ref = (next(it) for _ in range(4))
    h0_ref = next(it) if has_init else None
    hf_ref, hb_ref, fin_ref, st_ref = (next(it) for _ in range(4))
    c = pl.program_id(1)

    @pl.when(c == 0)
    def _():
        st_ref[...] = h0_ref[...] if has_init else jnp.zeros(st_ref.shape, F32)

    def body(t, carry):
        tb = tc - 1 - t
        new = []
        for j in range(nbt):
            hf, hb = carry[2 * j], carry[2 * j + 1]
            hf = af_ref[j, pl.ds(t, 1), :] * hf + bf_ref[j, pl.ds(t, 1), :]
            hb = ab_ref[j, pl.ds(tb, 1), :] * hb + bb_ref[j, pl.ds(tb, 1), :]
            hf_ref[j, pl.ds(t, 1), :] = hf
            hb_ref[j, pl.ds(tb, 1), :] = hb
            new += [hf, hb]
        return tuple(new)

    init = tuple(st_ref[j, d:d + 1, :] for j in range(nbt) for d in range(2))
    out = lax.fori_loop(0, tc, body, init)
    for j in range(nbt):
        for d in range(2):
            st_ref[j, d:d + 1, :] = out[2 * j + d]
    fin_ref[...] = st_ref[...]


def _lru_scan(af, bf, ab, bb, batch, seq, h0=None):
    nbt = min(batch, 4)
    tc = min(seq, 256)
    n = seq // tc
    shp = (batch, seq, W_B)
    fspec = pl.BlockSpec((nbt, tc, W_B), lambda g, c: (g, c, 0))
    bspec = pl.BlockSpec((nbt, tc, W_B), lambda g, c: (g, n - 1 - c, 0))
    sspec = pl.BlockSpec((nbt, 2, W_B), lambda g, c: (g, 0, 0))
    args = [af.reshape(shp), bf.reshape(shp), ab.reshape(shp), bb.reshape(shp)]
    specs = [fspec, fspec, bspec, bspec]
    if h0 is not None:
        args.append(h0)
        specs.append(sspec)
    hf, hb, fin = pl.pallas_call(
        functools.partial(_lru_scan_kernel, nbt=nbt, tc=tc, has_init=h0 is not None),
        out_shape=(jax.ShapeDtypeStruct(shp, F32), jax.ShapeDtypeStruct(shp, F32),
                   jax.ShapeDtypeStruct((batch, 2, W_B), F32)),
        grid=(batch // nbt, n), in_specs=specs, out_specs=(fspec, bspec, sspec),
        scratch_shapes=[pltpu.VMEM((nbt, 2, W_B), F32)],
        compiler_params=_cparams(("parallel", "arbitrary")), name="lru_scan",
    )(*args)
    return hf.reshape(batch * seq, W_B), hb.reshape(batch * seq, W_B), fin


S5_BLOCKS = 2


def _s5_param_kernel(prm_ref, bre_ref, bim_ref, bbar_ref, abar_ref):
    a_re, a_im, ldt = prm_ref[0, 0:1], prm_ref[0, 1:2], prm_ref[0, 2:3]
    dt = jnp.exp(ldt)
    mag = jnp.exp(a_re * dt)
    abr = mag * jnp.cos(a_im * dt)
    abi = mag * jnp.sin(a_im * dt)
    den = a_re * a_re + a_im * a_im
    fr = ((abr - 1.0) * a_re + abi * a_im) / den
    fi = (abi * a_re - (abr - 1.0) * a_im) / den
    bre, bim = bre_ref[0, 0], bim_ref[0, 0]
    ns = bre.shape[1]
    bbar_ref[0, 0, :, 0:ns] = (fr * bre - fi * bim).astype(BF16)
    bbar_ref[0, 0, :, ns:] = (fr * bim + fi * bre).astype(BF16)
    abar_ref[0, 0:1] = abr
    abar_ref[0, 1:2] = abi


def _s5_params(prm, bre, bim):
    gn = prm.shape[2]
    _, nblk, cb, ns = bre.shape
    return pl.pallas_call(
        _s5_param_kernel,
        out_shape=(jax.ShapeDtypeStruct((2, nblk, cb, 2 * ns), BF16), jax.ShapeDtypeStruct((2, 2, gn), F32)),
        grid=(2, nblk),
        in_specs=[pl.BlockSpec((1, 3, ns), lambda d, j: (d, 0, j)),
                  pl.BlockSpec((1, 1, cb, ns), lambda d, j: (d, j, 0, 0)),
                  pl.BlockSpec((1, 1, cb, ns), lambda d, j: (d, j, 0, 0))],
        out_specs=(pl.BlockSpec((1, 1, cb, 2 * ns), lambda d, j: (d, j, 0, 0)),
                   pl.BlockSpec((1, 2, ns), lambda d, j: (d, 0, j))),
        compiler_params=_cparams(("arbitrary", "arbitrary")), name="s5_params",
    )(prm, bre, bim)


def _s5_scan_kernel(*refs, bp, tc, cw, has_init):
    it = iter(refs)
    uf_ref, ub_ref, bbar_ref, cc_ref, abar_ref = (next(it) for _ in range(5))
    s0_ref = next(it) if has_init else None
    yf_ref, yb_ref, fin_ref, buf_f, buf_b, st_ref = (next(it) for _ in range(6))
    c = pl.program_id(0)
    gn = abar_ref.shape[2]
    nblk, cb = bbar_ref.shape[1], bbar_ref.shape[2]
    ns = gn // nblk
    tr = st_ref.shape[1]
    sub = tr // bp
    n_it = tc // sub

    @pl.when(c == 0)
    def _():
        st_ref[...] = s0_ref[...] if has_init else jnp.zeros(st_ref.shape, F32)

    for d, (u_ref, buf) in enumerate(((uf_ref, buf_f), (ub_ref, buf_b))):
        u = jnp.concatenate([u_ref[j] for j in range(u_ref.shape[0])], axis=1).astype(BF16)
        for j in range(nblk):
            bu = _dot(u[:, j * cb:(j + 1) * cb], bbar_ref[d, j])
            buf[:, j * ns:(j + 1) * ns] = bu[:, :ns]
            buf[:, gn + j * ns:gn + (j + 1) * ns] = bu[:, ns:]

    first_f = lax.broadcasted_iota(jnp.int32, (tr, cw), 0) < bp

    def cmul_add(ar, ai, hr, hi, xr, xi):
        return ar * hr - ai * hi + xr, ar * hi + ai * hr + xi

    for kb in range(gn // cw):
        re = slice(kb * cw, (kb + 1) * cw)
        im = slice(gn + kb * cw, gn + (kb + 1) * cw)
        coef = []
        for d in range(2):
            coef.append((jnp.broadcast_to(abar_ref[d, 0:1, re], (tr, cw)),
                         jnp.broadcast_to(abar_ref[d, 1:2, re], (tr, cw))))

        def body(i, carry, re=re, im=im, coef=coef):
            out = []
            for d, buf in enumerate((buf_f, buf_b)):
                hr, hi = carry[2 * d], carry[2 * d + 1]
                ar, ai = coef[d]
                tile = i if d == 0 else n_it - 1 - i
                rows = pl.ds(pl.multiple_of(tile * tr, tr), tr)
                xr, xi = buf[rows, re], buf[rows, im]
                r1, i1 = cmul_add(ar, ai, hr, hi, xr, xi)
                if sub == 1:
                    outr, outi, nr, ni = r1, i1, r1, i1
                else:
                    first = first_f if d == 0 else jnp.logical_not(first_f)
                    r2, i2 = cmul_add(ar, ai, pltpu.roll(r1, bp, axis=0), pltpu.roll(i1, bp, axis=0), xr, xi)
                    outr, outi = jnp.where(first, r1, r2), jnp.where(first, i1, i2)
                    nr, ni = pltpu.roll(r2, bp, axis=0), pltpu.roll(i2, bp, axis=0)
                buf[rows, re] = outr
                buf[rows, im] = outi
                out += [nr, ni]
            return tuple(out)

        init = (st_ref[0, :, re], st_ref[0, :, im], st_ref[1, :, re], st_ref[1, :, im])
        fr_, fi_, br_, bi_ = lax.fori_loop(0, n_it, body, init)
        st_ref[0, :, re] = fr_
        st_ref[0, :, im] = fi_
        st_ref[1, :, re] = br_
        st_ref[1, :, im] = bi_

    for d, (y_ref, buf) in enumerate(((yf_ref, buf_f), (yb_ref, buf_b))):
        for j in range(nblk):
            hre = buf[:, j * ns:(j + 1) * ns].astype(BF16)
            him = buf[:, gn + j * ns:gn + (j + 1) * ns].astype(BF16)
            yj = _dot(hre, cc_ref[d, j, 0:ns, :]) + _dot(him, cc_ref[d, j, ns:, :])
            for k in range(cb // LANES):
                y_ref[j * (cb // LANES) + k] = yj[:, k * LANES:(k + 1) * LANES]
    fin_ref[...] = st_ref[...]


def _s5_scan(u_tm, bbar, cc, abar, bp, seq, s0=None):
    assert bp % SUBLANES == 0 or 2 * bp == SUBLANES
    tr = max(bp, SUBLANES)
    rows = min(512, seq * bp)
    tc = rows // bp
    n = seq // tc
    cw = 128 if bp > SUBLANES else 256
    ng = W_C // LANES
    fspec = pl.BlockSpec((ng, rows, LANES), lambda c: (0, c, 0))
    bspec = pl.BlockSpec((ng, rows, LANES), lambda c: (0, n - 1 - c, 0))
    args = [u_tm, u_tm, bbar, cc, abar]
    specs = [fspec, bspec, _const_spec(bbar.shape), _const_spec(cc.shape), _const_spec(abar.shape)]
    if s0 is not None:
        args.append(s0)
        specs.append(_const_spec(s0.shape))
    return pl.pallas_call(
        functools.partial(_s5_scan_kernel, bp=bp, tc=tc, cw=cw, has_init=s0 is not None),
        out_shape=(jax.ShapeDtypeStruct(u_tm.shape, F32), jax.ShapeDtypeStruct(u_tm.shape, F32),
                   jax.ShapeDtypeStruct((2, tr, S5_STATE), F32)),
        grid=(n,), in_specs=specs,
        out_specs=(fspec, bspec, _const_spec((2, tr, S5_STATE))),
        scratch_shapes=[pltpu.VMEM((rows, S5_STATE), F32), pltpu.VMEM((rows, S5_STATE), F32),
                        pltpu.VMEM((2, tr, S5_STATE), F32)],
        compiler_params=_cparams(("arbitrary",)), name="s5_scan",
    )(*args)


def _s5_post_kernel(yf_ref, yb_ref, u_ref, d_ref, w_ref, b_ref, o_ref):
    nb, tl, _ = o_ref.shape
    zs = []
    for b in range(nb):
        rows = pl.ds(b, tl, stride=nb)
        yf, yb, u = (jnp.concatenate([ref[j, rows, :] for j in range(ref.shape[0])], axis=1)
                     for ref in (yf_ref, yb_ref, u_ref))
        zs.append(_gelu_tanh(yf + yb + d_ref[...] * u))
    z = jnp.concatenate(zs, axis=0)
    y = z * _sigmoid(_dot(z.astype(BF16), w_ref[...]) + b_ref[...])
    for b in range(nb):
        o_ref[b] = y[b * tl:(b + 1) * tl]


def _s5_post(yf, yb, u_tm, s5_d, glu_w, glu_b, batch, seq):
    tl = _time_tile(batch, seq)
    spec = pl.BlockSpec((W_C // LANES, tl * batch, LANES), lambda i: (0, i, 0))
    return pl.pallas_call(
        _s5_post_kernel, out_shape=jax.ShapeDtypeStruct((batch, seq, W_C), F32), grid=(seq // tl,),
        in_specs=[spec, spec, spec, _const_spec(s5_d.shape), _const_spec(glu_w.shape), _const_spec(glu_b.shape)],
        out_specs=pl.BlockSpec((batch, tl, W_C), lambda i: (0, i, 0)),
        compiler_params=_cparams(("parallel",)), name="s5_post",
    )(yf, yb, u_tm, s5_d, glu_w, glu_b).reshape(batch * seq, W_C)


def _rwkv_prep_kernel(x_ref, xp_ref, xn_ref, vec_ref, w1_ref, w2_ref, a1_ref, a2_ref, g1_ref, g2_ref,
                      ones_ref, *out_refs, seq):
    (g_ref, bon_ref, v_ref, kk_ref, r_ref) = out_refs[:5]
    dir_refs = (out_refs[5:8], out_refs[8:11])

    def put_pairs(ref, val):
        for q in range(ref.shape[0]):
            ref[q] = val[:, q * RWKV_PAIR:(q + 1) * RWKV_PAIR]

    x = x_ref[...]
    tm = x.shape[0]
    ext = jnp.concatenate([xp_ref[...], x, xn_ref[...]], axis=0)
    pos = _seq_pos(x.shape, tm, seq)
    dx = 0.5 * (_seq_shift(ext, 1, pos, seq, tm) + _seq_shift(ext, -1, pos, seq, tm)) - x
    w = W_D
    r, k, v, xd = (x[:, j * w:(j + 1) * w] for j in range(4))
    dr, dk_, dv, dxd = (dx[:, j * w:(j + 1) * w] for j in range(4))
    vec = vec_ref[...]
    mu = vec[0:6]
    r = r + dr * mu[0:1]
    k = k + dk_ * mu[1:2]
    v = v + dv * mu[2:3]
    xw = xd + dxd * mu[3:4]
    xa = xd + dxd * mu[4:5]
    xg = xd + dxd * mu[5:6]
    rw_kk, rw_ka, rw_rk = vec[6:7], vec[7:8], vec[8:9]
    ones = ones_ref[...]
    g_ref[...] = _bdot(_sigmoid(_bdot(xg, g1_ref[...])), g2_ref[...])
    kkr = k * rw_kk
    kk = kkr / jnp.maximum(jnp.sqrt(_bdot(kkr * kkr, ones)), 1e-12)
    put_pairs(v_ref, v)
    put_pairs(kk_ref, kk)
    put_pairs(r_ref, r)
    bonus = jnp.zeros(v.shape, F32)
    for d in range(2):
        w0, a0 = vec[9 + d:10 + d], vec[11 + d:12 + d]
        wl = -_softplus(-(w0 + _bdot(jnp.tanh(_bdot(xw, w1_ref[d])), w2_ref[d]))) - 0.5
        a = _sigmoid(a0 + _bdot(_bdot(xa, a1_ref[d]), a2_ref[d]))
        kd = k * (1.0 + (a - 1.0) * rw_ka)
        lw_ref, ka_ref, kd_ref = dir_refs[d]
        put_pairs(lw_ref, -jnp.exp(wl))
        put_pairs(ka_ref, kk * a)
        put_pairs(kd_ref, kd)
        bonus = bonus + _bdot(r * kd * rw_rk, ones) * v
    bon_ref[...] = bonus


N_RWKV_SEQS = 9


def _rwkv_prep(tiles, rkvx, vec, w1, w2, a1, a2, g1, g2, ones):
    n = rkvx.shape[0]
    npair = W_D // RWKV_PAIR
    prev, nxt = _halo_specs(tiles, 4 * W_D, 0, n)
    flat = jax.ShapeDtypeStruct((n, W_D), F32)
    paired = jax.ShapeDtypeStruct((npair, n, RWKV_PAIR), F32)
    pspec = pl.BlockSpec((npair, tiles.tm, RWKV_PAIR), lambda i: (0, i, 0))
    consts = [vec, w1, w2, a1, a2, g1, g2, ones]
    return pl.pallas_call(
        functools.partial(_rwkv_prep_kernel, seq=tiles.seq),
        out_shape=(flat,) * 2 + (paired,) * N_RWKV_SEQS, grid=(tiles.n_tiles,),
        in_specs=[tiles.row_spec(4 * W_D), prev, nxt] + [_const_spec(c.shape) for c in consts],
        out_specs=(tiles.row_spec(W_D),) * 2 + (pspec,) * N_RWKV_SEQS,
        compiler_params=_cparams(("parallel",)), name="rwkv_prep",
    )(rkvx, rkvx, rkvx, *consts)


def _bmm(a, b, precision=None):
    return jnp.einsum('umk,ukn->umn', a, b, precision=precision, preferred_element_type=F32)


def _bmm_nt(a, b):
    return jnp.einsum('umk,unk->umn', a, b, preferred_element_type=F32)


def _bmm_tn(a, b):
    return jnp.stack([_dot_tn(a[u], b[u]) for u in range(a.shape[0])], axis=0)


def _bf(a):
    return a.astype(BF16)


def _rwkv_masks(t, nbt):
    n2 = 2 * t
    ri = lax.broadcasted_iota(jnp.int32, (n2, n2), 0)
    ci = lax.broadcasted_iota(jnp.int32, (n2, n2), 1)
    same_head = (ri >= t) == (ci >= t)
    tt, ss = ri & (t - 1), ci & (t - 1)
    ti = lax.broadcasted_iota(jnp.int32, (t, t), 0)
    si = lax.broadcasted_iota(jnp.int32, (t, t), 1)

    def per_unit(fwd, bwd, dtype):
        return jnp.concatenate([jnp.broadcast_to(m.astype(dtype)[None], (nbt,) + m.shape) for m in (fwd, bwd)],
                               axis=0)

    return dict(
        strict=per_unit(same_head & (ss < tt), same_head & (ss > tt), F32),
        incl=per_unit(same_head & (ss <= tt), same_head & (ss >= tt), F32),
        tri=per_unit(si <= ti, si >= ti, BF16),
        same16=(ri >> 4) == (ci >> 4),
        off16=((ri >> 5) == (ci >> 5)) & ((ri >> 4) != (ci >> 4)),
        off32=(ri >> 5) != (ci >> 5),
        eye=(ri == ci).astype(F32),
        head0=lax.broadcasted_iota(jnp.int32, (t, RWKV_PAIR), 1) < HD_D,
    )


def _split(a):
    hi = a.astype(BF16)
    return hi, (a - hi.astype(F32)).astype(BF16)


def _bmm3(a, b):
    (ah, al), (bh, bl) = a, b
    n = bh.shape[2]
    lhs = jnp.concatenate([ah, al], axis=2)
    rhs = jnp.concatenate([jnp.concatenate([bh, bl], axis=2),
                           jnp.concatenate([bh, jnp.zeros_like(bl)], axis=2)], axis=1)
    out = _bmm(lhs, rhs)
    return out[:, :, :n] + out[:, :, n:]


def _unit_tri_inverse(a, mk):
    n1f = -jnp.where(mk['same16'], a, 0.0)
    n1 = _split(n1f)
    n2 = _split(_bmm3(n1, n1))
    n4 = _split(_bmm3(n2, n2))
    n8 = _split(_bmm3(n4, n4))
    p = mk['eye'] + n1f
    p = p + _bmm3(_split(p), n2)
    p = p + _bmm3(_split(p), n4)
    p = p + _bmm3(_split(p), n8)
    for off in ('off16', 'off32'):
        ps = _split(p)
        p = p - _bmm3(_split(_bmm3(ps, _split(jnp.where(mk[off], a, 0.0)))), ps)
    return _split(p)


def _rwkv_unit(r, v, kap, lw, beta, kd, s, mk):
    u, t, _ = r.shape
    lw_hi = lw.astype(BF16)
    lw_rest = lw - lw_hi.astype(F32)
    lw_mid = lw_rest.astype(BF16)
    lw_lo = (lw_rest - lw_mid.astype(F32)).astype(BF16)
    c3 = _bmm(mk['tri'], jnp.concatenate([lw_hi, lw_mid, lw_lo], axis=2))
    w = lw.shape[2]
    c = c3[:, :, :w] + c3[:, :, w:2 * w] + c3[:, :, 2 * w:]
    c_end = jnp.concatenate([c[:u // 2, t - 1:t], c[u // 2:, 0:1]], axis=0)
    e_neg = jnp.exp(-c)
    e_end = jnp.exp(c_end - c)
    head0 = mk['head0']

    def stack(x):
        return jnp.concatenate([jnp.where(head0, x, 0.0), jnp.where(head0, 0.0, x)], axis=1)

    rs = stack(r * jnp.exp(c))
    ks = stack(kap * jnp.exp(c - lw))
    bs = stack(beta * e_neg)
    kds = stack(kd * e_neg)
    vs = stack(v)
    bps = stack(beta * e_end)
    kps = stack(kd * e_end)
    n2 = 2 * t
    g = _bmm_nt(_bf(jnp.concatenate([ks, rs], axis=1)), _bf(jnp.concatenate([bs, kds], axis=1)))
    a_ab = g[:, :n2, :n2] * mk['strict']
    a_ak = g[:, :n2, n2:] * mk['strict']
    a_rb = g[:, n2:, :n2] * mk['incl']
    a_rk = g[:, n2:, n2:] * mk['incl']
    m = _unit_tri_inverse(a_ab, mk)
    vsb = _bf(vs)
    sol = _bmm3(m, _split(jnp.concatenate([ks, _bmm(_bf(a_ak), vsb)], axis=2)))
    wy = jnp.concatenate([rs, _bmm(_bf(a_rk), vsb)], axis=2) - _bmm(_bf(a_rb), _bf(sol))
    wy = wy[:, :t] + wy[:, t:]
    pg = _bmm_tn(_bf(sol), _bf(bps))
    gam = _bmm_tn(vsb, _bf(kps)) - pg[:, RWKV_PAIR:]
    y = _bmm_nt(_bf(wy[:, :, :RWKV_PAIR]), _bf(s)) + wy[:, :, RWKV_PAIR:]
    s_new = s * jnp.exp(c_end) - _bmm(_bf(s), _bf(pg[:, :RWKV_PAIR])) + gam
    return y, s_new


def _rwkv_chunk_kernel(*refs, nbt, has_init):
    it = iter(refs)
    ins = [tuple(next(it) for _ in range(6)) for _ in range(2)]
    s0_ref = next(it) if has_init else None
    yf_ref, yb_ref, fin_ref, st_ref = (next(it) for _ in range(4))
    c = pl.program_id(2)

    @pl.when(c == 0)
    def _():
        for d in range(2):
            st_ref[d * nbt:(d + 1) * nbt] = (s0_ref[:, d, 0] if has_init
                                             else jnp.zeros((nbt,) + st_ref.shape[1:], F32))

    t = RWKV_CHUNK
    nsub = yf_ref.shape[2] // t
    mk = _rwkv_masks(t, nbt)

    def chunk(j, carry):
        rows_f = pl.ds(pl.multiple_of(j * t, t), t)
        rows_b = pl.ds(pl.multiple_of((nsub - 1 - j) * t, t), t)
        ops = [jnp.concatenate([f[0, :, rows_f, :], b[0, :, rows_b, :]], axis=0) for f, b in zip(*ins)]
        y, s_new = _rwkv_unit(*ops, st_ref[...], mk)
        yf_ref[0, :, rows_f, :] = y[:nbt]
        yb_ref[0, :, rows_b, :] = y[nbt:]
        st_ref[...] = s_new
        return carry

    lax.fori_loop(0, nsub, chunk, 0)
    for d in range(2):
        for h in range(2):
            fin_ref[:, d, h] = st_ref[d * nbt:(d + 1) * nbt, h * HD_D:(h + 1) * HD_D, h * HD_D:(h + 1) * HD_D]


def _rwkv_scan(shared, fdir, bdir, batch, seq, s0=None):
    nbt = min(batch, 8)
    tb = min(seq, 4 * RWKV_CHUNK)
    n = seq // tb
    npair = W_D // RWKV_PAIR
    shp = (npair, batch, seq, RWKV_PAIR)
    fspec = pl.BlockSpec((1, nbt, tb, RWKV_PAIR), lambda g, q, c: (q, g, c, 0))
    bspec = pl.BlockSpec((1, nbt, tb, RWKV_PAIR), lambda g, q, c: (q, g, n - 1 - c, 0))
    sspec = pl.BlockSpec((nbt, 2, 1, RWKV_PAIR, RWKV_PAIR), lambda g, q, c: (g, 0, q, 0, 0))
    args = [a.reshape(shp) for a in tuple(shared) + tuple(fdir) + tuple(shared) + tuple(bdir)]
    specs = [fspec] * 6 + [bspec] * 6
    if s0 is not None:
        args.append(s0)
        specs.append(sspec)
    yf, yb, fin = pl.pallas_call(
        functools.partial(_rwkv_chunk_kernel, nbt=nbt, has_init=s0 is not None),
        out_shape=(jax.ShapeDtypeStruct(shp, F32), jax.ShapeDtypeStruct(shp, F32),
                   jax.ShapeDtypeStruct((batch, 2, H_D, HD_D, HD_D), F32)),
        grid=(batch // nbt, npair, n), in_specs=specs,
        out_specs=(fspec, bspec, pl.BlockSpec((nbt, 2, 2, HD_D, HD_D), lambda g, q, c: (g, 0, q, 0, 0))),
        scratch_shapes=[pltpu.VMEM((2 * nbt, RWKV_PAIR, RWKV_PAIR), F32)],
        compiler_params=_cparams(("parallel", "parallel", "arbitrary")), name="rwkv_scan",
    )(*args)
    pshape = (npair, batch * seq, RWKV_PAIR)
    return yf.reshape(pshape), yb.reshape(pshape), fin


def _block_diag(w):
    n, a, b = w.shape
    return jnp.einsum('nab,nm->namb', w, jnp.eye(n, dtype=w.dtype)).reshape(n * a, n * b)


def _rope_tables(seq, dk):
    t = jnp.arange(seq)
    row = (t // GRID_W).astype(F32)
    col = (t % GRID_W).astype(F32)
    nf = dk // 4
    freqs = ROPE_BASE ** (-jnp.arange(nf, dtype=F32) / nf)
    ar, ac = row[:, None] * freqs, col[:, None] * freqs
    cos = jnp.concatenate([jnp.cos(ar), jnp.cos(ar), jnp.cos(ac), jnp.cos(ac)], axis=1)
    sin = jnp.concatenate([-jnp.sin(ar), jnp.sin(ar), -jnp.sin(ac), jnp.sin(ac)], axis=1)
    return cos, sin


def _trunk(x3, cond_base, per_batch_cond, is_latent, states, mods, p):
    batch, seq, d = x3.shape
    tiles = _Tiles(batch, seq, cond_base, per_batch_cond)
    x = x3.reshape(batch * seq, d)
    s_ret, s_lru, s_s5, s_rwkv = states

    m0, g0 = mods[0], p['norm_g'][0]
    x = _ffn(tiles, x, m0, g0, p['f1a'], p['f1b'], 0, first=True)
    rope = _rope_tables(seq, DK_A) if is_latent else None
    qkvg, gx = _proj_even(tiles, x, m0, g0, p['l0_w_in'], rope_tabs=rope)
    y_a, new_ret = _retention(qkvg, p['ret_dec'], batch, seq, s0=s_ret)
    af, bf, ab, bb = _lru_prep(tiles, gx, p['conv_wb'], p['lru_wg'], p['lru_bg'], p['lru_lam'])
    hf, hb, new_lru = _lru_scan(af, bf, ab, bb, batch, seq, h0=s_lru)
    x = _ffn(tiles, x, m0, g0, p['f2a'], p['f2b'], 0, first=False, mix="even",
             mix_args=(y_a, gx, hf, hb), wo=p['l0_w_out'])

    m1, g1 = mods[1], p['norm_g'][1]
    x = _ffn(tiles, x, m1, g1, p['f1a'], p['f1b'], 1, first=True)
    u_tm, rkvx = _proj_odd(tiles, x, m1, g1, p['l1_w_in'])
    bp = batch
    s5_0 = None
    if s_s5 is not None:
        s5_0 = jnp.transpose(s_s5.reshape(batch, 2, S5_STATE), (1, 0, 2))
        s5_0 = jnp.concatenate([s5_0] * (max(bp, SUBLANES) // bp), axis=1)
    yf5, yb5, fin5 = _s5_scan(u_tm, p['s5_bbar'], p['s5_cc'], p['s5_abar'], bp, seq, s0=s5_0)
    yc = _s5_post(yf5, yb5, u_tm, p['s5_d'], p['glu_w'], p['glu_b'], batch, seq)
    new_s5 = jnp.stack([fin5[0, :batch], fin5[1, fin5.shape[1] - batch:]], axis=1)
    new_s5 = new_s5.reshape(batch, 2, 2, G_C, N_C)
    outs = _rwkv_prep(tiles, rkvx, p['rw_vec'], p['rw_w1'], p['rw_w2'], p['rw_a1'], p['rw_a2'],
                      p['rw_g1'], p['rw_g2'], p['ones_hd'])
    gg, bonus, v, kk, r = outs[:5]
    npair = H_D // 2
    eye2 = jnp.eye(2, dtype=F32)
    rw0 = None
    if s_rwkv is not None:
        rw0 = jnp.einsum('bdqjvk,ji->bdqjvik', s_rwkv.reshape(batch, 2, npair, 2, HD_D, HD_D), eye2)
        rw0 = rw0.reshape(batch, 2, npair, RWKV_PAIR, RWKV_PAIR)
    yfr, ybr, new_rwkv = _rwkv_scan((r, v, kk), outs[5:8], outs[8:11], batch, seq, s0=rw0)
    x = _ffn(tiles, x, m1, g1, p['f2a'], p['f2b'], 1, first=False, mix="odd",
             mix_args=(yc, yfr, ybr, bonus, gg, p['avg_hd'], p['ln_wb']), wo=p['l1_w_out'],
             final_g=p['final_norm'])
    return x.reshape(batch, seq, d), (new_ret, new_lru, new_s5, new_rwkv)


def kernel(x_prompt, x_sample, state_l0_ret, state_l0_lru, state_l1_s5, state_l1_rwkv, c, c_ctx, mod_w, mod_b, norm_g, ffn1_w13, ffn1_w2, ffn2_w13, ffn2_w2, final_norm, l0_w_in, l0_w_out, l0_ret_decay, l0_conv_w, l0_conv_b, l0_lru_lam, l0_lru_wa, l0_lru_ba, l0_lru_wx, l0_lru_bx, l1_w_in, l1_w_out, l1_s5_a_re, l1_s5_a_im, l1_s5_log_dt, l1_s5_b_re, l1_s5_b_im, l1_s5_c_re, l1_s5_c_im, l1_s5_d, l1_glu_w, l1_glu_b, l1_rw_mu, l1_rw_w0, l1_rw_w1, l1_rw_w2, l1_rw_a0, l1_rw_a1, l1_rw_a2, l1_rw_g1, l1_rw_g2, l1_rw_kk, l1_rw_ka, l1_rw_rk, l1_ln_w, l1_ln_b):
    d = x_prompt.shape[-1]
    nlat = c.shape[0]
    cond8 = jnp.concatenate([c_ctx[None, :], jnp.zeros((SUBLANES - 1 - nlat, d), F32), c], axis=0)
    mods = _modulation(cond8, mod_w, mod_b)

    gn = G_C * N_C
    gb = G_C // S5_BLOCKS
    eye_g = jnp.eye(gb, dtype=F32)

    def embed(w, spec):
        w = w.reshape((2, S5_BLOCKS, gb) + w.shape[2:])
        return jnp.einsum(spec, w, eye_g)

    bre = embed(l1_s5_b_re, 'djgns,gh->djgshn').reshape(2, S5_BLOCKS, gb * GS_C, gb * N_C)
    bim = embed(l1_s5_b_im, 'djgns,gh->djgshn').reshape(2, S5_BLOCKS, gb * GS_C, gb * N_C)
    cre = embed(l1_s5_c_re, 'djgsn,gh->djgnhs').reshape(2, S5_BLOCKS, gb * N_C, gb * GS_C)
    cim = embed(l1_s5_c_im, 'djgsn,gh->djgnhs').reshape(2, S5_BLOCKS, gb * N_C, gb * GS_C)
    prm = jnp.stack([l1_s5_a_re.reshape(2, gn), l1_s5_a_im.reshape(2, gn),
                     jnp.repeat(l1_s5_log_dt, N_C, axis=1)], axis=1)
    s5_bbar, s5_abar = _s5_params(prm, bre, bim)
    head_id = jnp.arange(W_D) // HD_D
    ones_hd = (head_id[:, None] == head_id[None, :]).astype(BF16)
    p = dict(
        norm_g=norm_g, final_norm=final_norm.reshape(1, d),
        f1a=ffn1_w13.astype(BF16), f1b=ffn1_w2.astype(BF16),
        f2a=ffn2_w13.astype(BF16), f2b=ffn2_w2.astype(BF16),
        l0_w_in=l0_w_in.astype(BF16), l0_w_out=l0_w_out.astype(BF16),
        l1_w_in=l1_w_in.astype(BF16), l1_w_out=l1_w_out.astype(BF16),
        ret_dec=jnp.broadcast_to(l0_ret_decay.astype(F32)[:, :, None, None], (2, H_A, 1, DK_A)),
        conv_wb=jnp.concatenate([l0_conv_w, l0_conv_b[None, :]], axis=0),
        lru_wg=jnp.concatenate([_block_diag(l0_lru_wa[0]), _block_diag(l0_lru_wx[0]),
                                _block_diag(l0_lru_wa[1]), _block_diag(l0_lru_wx[1])], axis=1).astype(BF16),
        lru_bg=jnp.concatenate([l0_lru_ba[0], l0_lru_bx[0], l0_lru_ba[1], l0_lru_bx[1]])[None, :],
        lru_lam=l0_lru_lam,
        s5_bbar=s5_bbar, s5_abar=s5_abar,
        s5_cc=jnp.concatenate([cre, -cim], axis=2).astype(BF16),
        s5_d=l1_s5_d.reshape(1, W_C), glu_w=l1_glu_w.astype(BF16), glu_b=l1_glu_b.reshape(1, W_C),
        rw_vec=jnp.concatenate([l1_rw_mu, l1_rw_kk[None], l1_rw_ka[None], l1_rw_rk[None],
                                l1_rw_w0, l1_rw_a0, jnp.zeros((3, W_D), F32)], axis=0),
        rw_w1=l1_rw_w1.astype(BF16), rw_w2=l1_rw_w2.astype(BF16),
        rw_a1=l1_rw_a1.astype(BF16), rw_a2=l1_rw_a2.astype(BF16),
        rw_g1=l1_rw_g1.astype(BF16), rw_g2=l1_rw_g2.astype(BF16),
        ones_hd=ones_hd, avg_hd=(ones_hd.astype(F32) / HD_D).astype(BF16),
        ln_wb=jnp.stack([l1_ln_w, l1_ln_b], axis=0),
    )
    y_prompt, ctx_states = _trunk(x_prompt, 0, False, False, (None, None, None, None), mods, p)
    y_sample, _ = _trunk(x_sample, SUBLANES - nlat, True, True,
                         (state_l0_ret, state_l0_lru, state_l1_s5, state_l1_rwkv), mods, p)
    return (y_prompt, y_sample) + ctx_states
```
